```python
import math
import jax
import jax.numpy as jnp
from jax import lax
import numpy as np

D_MODEL = 1024
BATCH = 8
SEQ = 4096
DEPTH = 2

GRID_W = 64
CTX_LEN = 256
D_FF = 2816
N_SUB = 3
N_MOD = 3 * N_SUB
MACARON_W = 0.5
ROPE_THETA = 10000.0
Q_BLOCK = 128
NEG_INF = -1e30
EPS = 1e-6

MLA_HEADS = 4
MLA_NOPE = 64
MLA_ROPE = 32
MLA_V = 64
MLA_Q_LORA = 192
MLA_KV_LORA = 128
MLA_W = MLA_HEADS * MLA_V
MLA_SCALE = (MLA_NOPE + MLA_ROPE) ** -0.5

HY_W = 256
HY_ORDER = 2
HY_EMB = 33
HY_FH = 64
HY_DECAY_TARGET = 1e-2
HY_FAST_PCT = 0.3
HY_SLOW_PCT = 1.5
SHORT_K = 3

SWA_HEADS = 4
SWA_KV_HEADS = 2
SWA_HD = 64
SWA_W = SWA_HEADS * SWA_HD
WINDOW = 128
SWA_BLOCK = WINDOW
SWA_SCALE = SWA_HD ** -0.5

S5_W = 256
S5_GC = 16
S5_GROUPS = S5_W // S5_GC
S5_P = 64
S5_DT_MIN = 1e-3
S5_DT_MAX = 1e-1

IN_SPLITS = (
    ('mla_ckv', MLA_KV_LORA),
    ('mla_krope', MLA_ROPE),
    ('swa_k', SWA_KV_HEADS * SWA_HD),
    ('swa_v', SWA_KV_HEADS * SWA_HD),
    ('s5_u', S5_W),
    ('mla_cq', MLA_Q_LORA),
    ('swa_q', SWA_W),
    ('hy', (HY_ORDER + 1) * HY_W),
)
N_SIDE = MLA_KV_LORA + MLA_ROPE + 2 * SWA_KV_HEADS * SWA_HD + S5_W
N_IN = N_SIDE + MLA_Q_LORA + SWA_W + (HY_ORDER + 1) * HY_W
BRANCH_PROJ = ('w_br_mla', 'w_br_hy', 'w_br_swa', 'w_br_s5')
N_BRANCH = len(BRANCH_PROJ)

kernel_name = 'hybrid_dit_parallel_mixers'


def rmsnorm(x, g):
    xf = x.astype(jnp.float32)
    y = xf * lax.rsqrt(jnp.mean(xf * xf, axis=-1, keepdims=True) + EPS)
    return (y * g.astype(jnp.float32)).astype(x.dtype)


def swiglu(u, w_up, w_down):
    a, b = jnp.split(u @ w_up, 2, axis=-1)
    return (jax.nn.silu(a) * b) @ w_down


def modulation(cvec, p):
    m = jax.nn.silu(cvec) @ p['w_ada'] + p['b_ada']
    return m.reshape(m.shape[:-1] + (N_MOD, D_MODEL))


def pre_mod(x, g, m, i):
    return rmsnorm(x, g) * (1.0 + m[..., None, 3 * i + 1, :]) + m[..., None, 3 * i, :]


def post_add(x, f, g, m, i, res_w):
    return x + res_w * m[..., None, 3 * i + 2, :] * rmsnorm(f, g)


def ffn_sublayer(x, m, p, i, w_up, w_down):
    u = pre_mod(x, p['norm_pre'][i], m, i)
    return post_add(x, swiglu(u, w_up, w_down), p['norm_post'][i], m, i, MACARON_W)


def split_cols(t):
    parts, off = {}, 0
    for name, size in IN_SPLITS:
        if off >= t.shape[-1]:
            break
        parts[name] = t[..., off:off + size]
        off += size
    return parts


def axial_rope(n_tokens, rot_dim):
    rows = n_tokens // GRID_W
    r = jnp.repeat(jnp.arange(rows, dtype=jnp.float32), GRID_W)
    col = jnp.tile(jnp.arange(GRID_W, dtype=jnp.float32), rows)
    n_freq = rot_dim // 4
    freqs = ROPE_THETA ** (-jnp.arange(n_freq, dtype=jnp.float32) / n_freq)
    ang = jnp.concatenate([r[:, None] * freqs, col[:, None] * freqs], axis=-1)
    return jnp.cos(ang), jnp.sin(ang)


def apply_rope(x, cos, sin):
    half = x.shape[-1] // 2
    x1, x2 = x[..., :half], x[..., half:]
    cos, sin = cos.astype(x.dtype), sin.astype(x.dtype)
    return jnp.concatenate([x1 * cos - x2 * sin, x1 * sin + x2 * cos], axis=-1)


def block_attention(q, k, v, scale):
    B, L, H, dq = q.shape
    nb = L // Q_BLOCK
    qb = jnp.moveaxis(q.reshape(B, nb, Q_BLOCK, H, dq), 1, 0)

    def one_block(qblk):
        s = jnp.einsum('bqhd,bkhd->bhqk', qblk, k).astype(jnp.float32) * scale
        pr = jax.nn.softmax(s, axis=-1).astype(v.dtype)
        return jnp.einsum('bhqk,bkhd->bqhd', pr, v)

    o = lax.map(one_block, qb)
    return jnp.moveaxis(o, 0, 1).reshape(B, L, -1)


def mla_queries(cq, p, rope):
    B, L, _ = cq.shape
    q = (rmsnorm(cq, p['mla_q_norm']) @ p['mla_w_uq']).reshape(B, L, MLA_HEADS, MLA_NOPE + MLA_ROPE)
    if rope is not None:
        q_rope = apply_rope(q[..., MLA_NOPE:], rope[0][:, None], rope[1][:, None])
        q = jnp.concatenate([q[..., :MLA_NOPE], q_rope], axis=-1)
    return q


def mla_keys_values(ckv, krope, p, rope):
    B, L, _ = ckv.shape
    kv = (rmsnorm(ckv, p['mla_kv_norm']) @ p['mla_w_ukv']).reshape(B, L, MLA_HEADS, MLA_NOPE + MLA_V)
    if rope is not None:
        krope = apply_rope(krope, rope[0], rope[1])
    k_rope = jnp.broadcast_to(krope[:, :, None, :], (B, L, MLA_HEADS, MLA_ROPE))
    k = jnp.concatenate([kv[..., :MLA_NOPE], k_rope], axis=-1)
    return k, kv[..., MLA_NOPE:]


def short_conv(x, w, b):
    L = x.shape[1]
    r = SHORT_K // 2
    xp = jnp.pad(x, ((0, 0), (r, r), (0, 0)))
    y = b
    for j in range(SHORT_K):
        y = y + xp[:, j:j + L] * w[j]
    return y


def hyena_filter_response(L, p):
    f32 = jnp.float32
    t = jnp.linspace(0.0, 1.0, L, dtype=f32)[:, None]
    bands = (HY_EMB - 1) // 2
    w = 2.0 * math.pi * jnp.arange(L, dtype=f32) / L
    fr = jnp.linspace(1e-4, bands - 1, bands, dtype=f32)
    ang = w[:, None] * fr[None, :]
    z = jnp.concatenate([t, jnp.cos(ang), -jnp.sin(ang)], axis=-1)
    freq = p['hy_f_freq'].astype(f32)
    h = jnp.sin(freq[0] * (z @ p['hy_f_w1'].astype(f32) + p['hy_f_b1'].astype(f32)))
    h = jnp.sin(freq[1] * (h @ p['hy_f_w2'].astype(f32) + p['hy_f_b2'].astype(f32)))
    h = (h @ p['hy_f_w3'].astype(f32)).reshape(L, HY_ORDER, 2, HY_W)
    deltas = jnp.abs(jnp.linspace(math.log(HY_DECAY_TARGET) / HY_SLOW_PCT,
                                  math.log(HY_DECAY_TARGET) / HY_FAST_PCT, HY_W, dtype=f32))
    h = h * jnp.exp(-t[:, :, None, None] * deltas)
    h_fwd, h_bwd = h[:, :, 0], h[:, :, 1]
    k_full = jnp.concatenate([h_fwd, jnp.zeros_like(h_fwd[:1]), h_bwd[1:][::-1]], axis=0)
    return jnp.fft.rfft(k_full, axis=0)


def fft_long_conv(z, kf, bias):
    L = z.shape[1]
    zf32 = z.astype(jnp.float32)
    y = jnp.fft.irfft(jnp.fft.rfft(zf32, n=2 * L, axis=1) * kf[None], n=2 * L, axis=1)[:, :L]
    return (y + zf32 * bias.astype(jnp.float32)).astype(z.dtype)


def hyena(proj, p):
    L = proj.shape[1]
    u = short_conv(proj, p['hy_conv_w'], p['hy_conv_b'])
    v, *gates = jnp.split(u, HY_ORDER + 1, axis=-1)
    kf = hyena_filter_response(L, p)
    z = v
    for o in range(HY_ORDER):
        z = gates[o] * fft_long_conv(z, kf[:, o], p['hy_bias'][o])
    return z


def sink_attention(q, k, v, sink):
    B, L, _, _ = q.shape
    g = SWA_HEADS // SWA_KV_HEADS
    qg = q.reshape(B, L, SWA_KV_HEADS, g, SWA_HD)
    s = jnp.einsum('bqkgd,bjkd->bkgqj', qg, k).astype(jnp.float32) * SWA_SCALE
    snk = jnp.broadcast_to(sink.astype(jnp.float32).reshape(SWA_KV_HEADS, g)[None, :, :, None, None],
                           s.shape[:-1] + (1,))
    pr = jax.nn.softmax(jnp.concatenate([s, snk], axis=-1), axis=-1)[..., :-1].astype(v.dtype)
    return jnp.einsum('bkgqj,bjkd->bqkgd', pr, v).reshape(B, L, SWA_W)


def banded_sink_attention(q, k, v, k_ctx, v_ctx, sink):
    B, S, _, _ = q.shape
    g = SWA_HEADS // SWA_KV_HEADS
    nb = S // SWA_BLOCK
    span = 3 * SWA_BLOCK
    qb = q.reshape(B, nb, SWA_BLOCK, SWA_KV_HEADS, g, SWA_HD)

    def band(t):
        tp = jnp.pad(t, ((0, 0), (SWA_BLOCK, SWA_BLOCK), (0, 0), (0, 0)))
        tp = tp.reshape(B, nb + 2, SWA_BLOCK, SWA_KV_HEADS, SWA_HD)
        return jnp.concatenate([tp[:, :-2], tp[:, 1:-1], tp[:, 2:]], axis=2)

    kw, vw = band(k), band(v)
    qpos = jnp.arange(S).reshape(nb, SWA_BLOCK)
    kpos = jnp.arange(nb)[:, None] * SWA_BLOCK - SWA_BLOCK + jnp.arange(span)[None, :]
    valid = ((jnp.abs(qpos[:, :, None] - kpos[:, None, :]) <= WINDOW)
             & (kpos[:, None, :] >= 0) & (kpos[:, None, :] < S))
    s_w = jnp.einsum('bnqkgd,bnjkd->bnkgqj', qb, kw).astype(jnp.float32) * SWA_SCALE
    s_w = jnp.where(valid[None, :, None, None], s_w, NEG_INF)
    s_c = jnp.einsum('bnqkgd,bjkd->bnkgqj', qb, k_ctx).astype(jnp.float32) * SWA_SCALE
    snk = jnp.broadcast_to(sink.astype(jnp.float32).reshape(SWA_KV_HEADS, g)[None, None, :, :, None, None],
                           s_w.shape[:-1] + (1,))
    pr = jax.nn.softmax(jnp.concatenate([s_w, s_c, snk], axis=-1), axis=-1).astype(v.dtype)
    n_ctx = k_ctx.shape[1]
    o = (jnp.einsum('bnkgqj,bnjkd->bnqkgd', pr[..., :span], vw)
         + jnp.einsum('bnkgqj,bjkd->bnqkgd', pr[..., span:span + n_ctx], v_ctx))
    return o.reshape(B, S, SWA_W)


def s5_discretize(p, d):
    f32 = jnp.float32
    a_re = p['s5_a_re'][d].astype(f32)
    a_im = p['s5_a_im'][d].astype(f32)
    dt = jnp.exp(p['s5_log_dt'][d].astype(f32))[:, None]
    mag = jnp.exp(dt * a_re)
    ab_re, ab_im = mag * jnp.cos(dt * a_im), mag * jnp.sin(dt * a_im)
    den = a_re * a_re + a_im * a_im
    f_re = ((ab_re - 1.0) * a_re + ab_im * a_im) / den
    f_im = (ab_im * a_re - (ab_re - 1.0) * a_im) / den
    return ab_re, ab_im, f_re, f_im


def complex_scan(ab_re, ab_im, x_re, x_im, h0, reverse):
    a_re = jnp.broadcast_to(ab_re, x_re.shape)
    a_im = jnp.broadcast_to(ab_im, x_re.shape)

    def combine(e1, e2):
        a1r, a1i, b1r, b1i = e1
        a2r, a2i, b2r, b2i = e2
        return (a2r * a1r - a2i * a1i, a2r * a1i + a2i * a1r,
                a2r * b1r - a2i * b1i + b2r, a2r * b1i + a2i * b1r + b2i)

    cr, ci, hr, hi = lax.associative_scan(combine, (a_re, a_im, x_re, x_im), reverse=reverse, axis=1)
    if h0 is not None:
        h0r, h0i = h0[0][:, None], h0[1][:, None]
        hr, hi = hr + cr * h0r - ci * h0i, hi + cr * h0i + ci * h0r
    return hr, hi


def s5_run(u, p, init):
    B, L, _ = u.shape
    ug = u.astype(jnp.float32).reshape(B, L, S5_GROUPS, S5_GC)
    states = []
    for d in range(2):
        ab_re, ab_im, f_re, f_im = s5_discretize(p, d)
        bu_re = jnp.einsum('blgc,gpc->blgp', ug, p['s5_b_re'][d].astype(jnp.float32))
        bu_im = jnp.einsum('blgc,gpc->blgp', ug, p['s5_b_im'][d].astype(jnp.float32))
        x_re = f_re * bu_re - f_im * bu_im
        x_im = f_re * bu_im + f_im * bu_re
        states.append(complex_scan(ab_re, ab_im, x_re, x_im, init[d], reverse=(d == 1)))
    return states


def s5_readout(u, states, p):
    B, L, _ = u.shape
    f32 = jnp.float32
    y = u.astype(f32) * p['s5_d'].astype(f32)
    for d, (hr, hi) in enumerate(states):
        yd = (jnp.einsum('blgp,gcp->blgc', hr, p['s5_c_re'][d].astype(f32))
              - jnp.einsum('blgp,gcp->blgc', hi, p['s5_c_im'][d].astype(f32)))
        y = y + yd.reshape(B, L, S5_W)
    g = jax.nn.gelu(y)
    out = g * jax.nn.sigmoid(g @ p['s5_glu_w'].astype(f32) + p['s5_glu_b'].astype(f32))
    return out.astype(u.dtype)


def merge_branches(u, outs, p):
    m = None
    for i, (o, name) in enumerate(zip(outs, BRANCH_PROJ)):
        gate = jax.nn.sigmoid(u @ p['w_gate'][i] + p['b_gate'][i])
        term = gate * (o @ p[name])
        m = term if m is None else m + term
    return m @ p['w_out']


def mixer_context(u, p, with_outputs):
    B, C, _ = u.shape
    w_in = p['w_in'] if with_outputs else p['w_in'][:, :N_SIDE]
    parts = split_cols(u @ w_in)
    k_mla, v_mla = mla_keys_values(parts['mla_ckv'], parts['mla_krope'], p, None)
    k_swa = parts['swa_k'].reshape(B, C, SWA_KV_HEADS, SWA_HD)
    v_swa = parts['swa_v'].reshape(B, C, SWA_KV_HEADS, SWA_HD)
    states = s5_run(parts['s5_u'], p, (None, None))
    (fr, fi), (br, bi) = states
    finals = ((fr[:, -1], fi[:, -1]), (br[:, 0], bi[:, 0]))
    side = (k_mla, v_mla, k_swa, v_swa, finals)
    if not with_outputs:
        return side, None
    o_mla = block_attention(mla_queries(parts['mla_cq'], p, None), k_mla, v_mla, MLA_SCALE)
    o_hy = hyena(parts['hy'], p)
    o_swa = sink_attention(parts['swa_q'].reshape(B, C, SWA_HEADS, SWA_HD), k_swa, v_swa, p['swa_sink'])
    o_s5 = s5_readout(parts['s5_u'], states, p)
    return side, merge_branches(u, (o_mla, o_hy, o_swa, o_s5), p)


def mixer_latent(u, side, p):
    B, S, _ = u.shape
    k_mla_c, v_mla_c, k_swa_c, v_swa_c, s5_h0 = side
    parts = split_cols(u @ p['w_in'])
    rope_m = axial_rope(S, MLA_ROPE)
    rope_w = axial_rope(S, SWA_HD)
    k_l, v_l = mla_keys_values(parts['mla_ckv'], parts['mla_krope'], p, rope_m)
    q_m = mla_queries(parts['mla_cq'], p, rope_m)
    o_mla = block_attention(q_m, jnp.concatenate([k_l, k_mla_c], axis=1),
                            jnp.concatenate([v_l, v_mla_c], axis=1), MLA_SCALE)
    o_hy = hyena(parts['hy'], p)
    cw, sw = rope_w[0][:, None], rope_w[1][:, None]
    q_w = apply_rope(parts['swa_q'].reshape(B, S, SWA_HEADS, SWA_HD), cw, sw)
    k_w = apply_rope(parts['swa_k'].reshape(B, S, SWA_KV_HEADS, SWA_HD), cw, sw)
    v_w = parts['swa_v'].reshape(B, S, SWA_KV_HEADS, SWA_HD)
    o_swa = banded_sink_attention(q_w, k_w, v_w, k_swa_c, v_swa_c, p['swa_sink'])
    states = s5_run(parts['s5_u'], p, s5_h0)
    o_s5 = s5_readout(parts['s5_u'], states, p)
    return merge_branches(u, (o_mla, o_hy, o_swa, o_s5), p)


def layer(xl, xc, ml, mc, p, ctx_out):
    xl = ffn_sublayer(xl, ml, p, 0, p['ffn1_up'], p['ffn1_down'])
    xc = ffn_sublayer(xc, mc, p, 0, p['ffn1_up'], p['ffn1_down'])
    ul = pre_mod(xl, p['norm_pre'][1], ml, 1)
    uc = pre_mod(xc, p['norm_pre'][1], mc, 1)
    side, oc = mixer_context(uc, p, ctx_out)
    ol = mixer_latent(ul, side, p)
    xl = post_add(xl, ol, p['norm_post'][1], ml, 1, 1.0)
    xl = ffn_sublayer(xl, ml, p, 2, p['ffn2_up'], p['ffn2_down'])
    if not ctx_out:
        return xl, None
    xc = post_add(xc, oc, p['norm_post'][1], mc, 1, 1.0)
    xc = ffn_sublayer(xc, mc, p, 2, p['ffn2_up'], p['ffn2_down'])
    return xl, xc


def setup_inputs(seed: int = 0) -> dict:
    key = jax.random.key(seed)
    keys = iter(jax.random.split(key, 64))
    f32 = jnp.float32

    def nrm(shape, std):
        return std * jax.random.normal(next(keys), shape, f32)

    def gain(shape):
        return 1.0 + nrm(shape, 0.02)

    inp = {}
    inp['x'] = nrm((BATCH, SEQ, D_MODEL), 1.0)
    inp['c'] = nrm((BATCH, D_MODEL), 1.0)
    inp['ctx'] = nrm((BATCH, CTX_LEN, D_MODEL), 1.0)
    inp['c_ctx'] = nrm((D_MODEL,), 1.0)
    inp['w_ada'] = nrm((DEPTH, D_MODEL, N_MOD * D_MODEL), 0.5 * D_MODEL ** -0.5)
    inp['b_ada'] = nrm((DEPTH, N_MOD * D_MODEL), 0.02)
    inp['norm_pre'] = gain((DEPTH, N_SUB, D_MODEL))
    inp['norm_post'] = gain((DEPTH, N_SUB, D_MODEL))
    inp['ffn1_up'] = nrm((DEPTH, D_MODEL, 2 * D_FF), D_MODEL ** -0.5)
    inp['ffn1_down'] = nrm((DEPTH, D_FF, D_MODEL), D_FF ** -0.5)
    inp['ffn2_up'] = nrm((DEPTH, D_MODEL, 2 * D_FF), D_MODEL ** -0.5)
    inp['ffn2_down'] = nrm((DEPTH, D_FF, D_MODEL), D_FF ** -0.5)
    inp['w_in'] = nrm((DEPTH, D_MODEL, N_IN), D_MODEL ** -0.5)
    inp['mla_q_norm'] = gain((DEPTH, MLA_Q_LORA))
    inp['mla_kv_norm'] = gain((DEPTH, MLA_KV_LORA))
    inp['mla_w_uq'] = nrm((DEPTH, MLA_Q_LORA, MLA_HEADS * (MLA_NOPE + MLA_ROPE)), MLA_Q_LORA ** -0.5)
    inp['mla_w_ukv'] = nrm((DEPTH, MLA_KV_LORA, MLA_HEADS * (MLA_NOPE + MLA_V)), MLA_KV_LORA ** -0.5)
    inp['hy_conv_w'] = nrm((DEPTH, SHORT_K, (HY_ORDER + 1) * HY_W), SHORT_K ** -0.5)
    inp['hy_conv_b'] = nrm((DEPTH, (HY_ORDER + 1) * HY_W), 0.02)
    inp['hy_f_w1'] = nrm((DEPTH, HY_EMB, HY_FH), HY_EMB ** -0.5)
    inp['hy_f_b1'] = nrm((DEPTH, HY_FH), 0.02)
    inp['hy_f_freq'] = gain((DEPTH, 2, HY_FH))
    inp['hy_f_w2'] = nrm((DEPTH, HY_FH, HY_FH), HY_FH ** -0.5)
    inp['hy_f_b2'] = nrm((DEPTH, HY_FH), 0.02)
    inp['hy_f_w3'] = nrm((DEPTH, HY_FH, HY_ORDER * 2 * HY_W), 0.1 * HY_FH ** -0.5)
    inp['hy_bias'] = nrm((DEPTH, HY_ORDER, HY_W), 0.5)
    inp['swa_sink'] = nrm((DEPTH, SWA_HEADS), 0.5)
    inp['s5_a_re'] = -0.5 + nrm((DEPTH, 2, S5_GROUPS, S5_P), 0.01)
    inp['s5_a_im'] = math.pi * jnp.arange(S5_P, dtype=f32) + nrm((DEPTH, 2, S5_GROUPS, S5_P), 0.01)
    inp['s5_log_dt'] = jax.random.uniform(next(keys), (DEPTH, 2, S5_GROUPS), f32,
                                          math.log(S5_DT_MIN), math.log(S5_DT_MAX))
    inp['s5_b_re'] = nrm((DEPTH, 2, S5_GROUPS, S5_P, S5_GC), (2 * S5_GC) ** -0.5)
    inp['s5_b_im'] = nrm((DEPTH, 2, S5_GROUPS, S5_P, S5_GC), (2 * S5_GC) ** -0.5)
    inp['s5_c_re'] = nrm((DEPTH, 2, S5_GROUPS, S5_GC, S5_P), S5_P ** -0.5)
    inp['s5_c_im'] = nrm((DEPTH, 2, S5_GROUPS, S5_GC, S5_P), S5_P ** -0.5)
    inp['s5_d'] = nrm((DEPTH, S5_W), 1.0)
    inp['s5_glu_w'] = nrm((DEPTH, S5_W, S5_W), S5_W ** -0.5)
    inp['s5_glu_b'] = nrm((DEPTH, S5_W), 0.02)
    inp['w_gate'] = nrm((DEPTH, N_BRANCH, D_MODEL, D_MODEL), D_MODEL ** -0.5)
    inp['b_gate'] = nrm((DEPTH, N_BRANCH, D_MODEL), 0.02)
    inp['w_br_mla'] = nrm((DEPTH, MLA_W, D_MODEL), MLA_W ** -0.5)
    inp['w_br_hy'] = nrm((DEPTH, HY_W, D_MODEL), HY_W ** -0.5)
    inp['w_br_swa'] = nrm((DEPTH, SWA_W, D_MODEL), SWA_W ** -0.5)
    inp['w_br_s5'] = nrm((DEPTH, S5_W, D_MODEL), S5_W ** -0.5)
    inp['w_out'] = nrm((DEPTH, D_MODEL, D_MODEL), D_MODEL ** -0.5)
    return inp


def reference(x, c, ctx, c_ctx, w_ada, b_ada, norm_pre, norm_post, ffn1_up, ffn1_down, ffn2_up, ffn2_down,
              w_in, mla_q_norm, mla_kv_norm, mla_w_uq, mla_w_ukv, hy_conv_w, hy_conv_b, hy_f_w1, hy_f_b1,
              hy_f_freq, hy_f_w2, hy_f_b2, hy_f_w3, hy_bias, swa_sink, s5_a_re, s5_a_im, s5_log_dt,
              s5_b_re, s5_b_im, s5_c_re, s5_c_im, s5_d, s5_glu_w, s5_glu_b, w_gate, b_gate,
              w_br_mla, w_br_hy, w_br_swa, w_br_s5, w_out):
    stacked = dict(w_ada=w_ada, b_ada=b_ada, norm_pre=norm_pre, norm_post=norm_post,
                   ffn1_up=ffn1_up, ffn1_down=ffn1_down, ffn2_up=ffn2_up, ffn2_down=ffn2_down,
                   w_in=w_in, mla_q_norm=mla_q_norm, mla_kv_norm=mla_kv_norm, mla_w_uq=mla_w_uq,
                   mla_w_ukv=mla_w_ukv, hy_conv_w=hy_conv_w, hy_conv_b=hy_conv_b, hy_f_w1=hy_f_w1,
                   hy_f_b1=hy_f_b1, hy_f_freq=hy_f_freq, hy_f_w2=hy_f_w2, hy_f_b2=hy_f_b2, hy_f_w3=hy_f_w3,
                   hy_bias=hy_bias, swa_sink=swa_sink, s5_a_re=s5_a_re, s5_a_im=s5_a_im,
                   s5_log_dt=s5_log_dt, s5_b_re=s5_b_re, s5_b_im=s5_b_im, s5_c_re=s5_c_re,
                   s5_c_im=s5_c_im, s5_d=s5_d, s5_glu_w=s5_glu_w, s5_glu_b=s5_glu_b,
                   w_gate=w_gate, b_gate=b_gate, w_br_mla=w_br_mla, w_br_hy=w_br_hy,
                   w_br_swa=w_br_swa, w_br_s5=w_br_s5, w_out=w_out)
    xl, xc = x, ctx
    for l in range(DEPTH):
        p = {name: arr[l] for name, arr in stacked.items()}
        ml = modulation(c, p)
        mc = modulation(c_ctx, p)
        xl, xc = layer(xl, xc, ml, mc, p, l < DEPTH - 1)
    return xl
```

```python
import functools
import math

import numpy as np
import jax
import jax.numpy as jnp
from jax import lax
from jax.experimental import pallas as pl
from jax.experimental.pallas import tpu as pltpu

F32 = jnp.float32
CDT = jnp.bfloat16

D_MODEL = 1024
D_FF = 2816
N_SUB = 3
N_MOD = 3 * N_SUB
MACARON_W = 0.5
ROPE_THETA = 10000.0
GRID_W = 64
EPS = 1e-6
NEG_INF = -1e30

MLA_HEADS = 4
MLA_NOPE = 64
MLA_ROPE = 32
MLA_V = 64
MLA_Q_LORA = 192
MLA_KV_LORA = 128
MLA_SCALE = (MLA_NOPE + MLA_ROPE) ** -0.5

HY_W = 256
HY_ORDER = 2
HY_EMB = 33
HY_DECAY_TARGET = 1e-2
HY_FAST_PCT = 0.3
HY_SLOW_PCT = 1.5
SHORT_K = 3

SWA_HEADS = 4
SWA_KV_HEADS = 2
SWA_HD = 64
WINDOW = 128
SWA_SCALE = SWA_HD ** -0.5

S5_W = 256
S5_GC = 16
S5_GROUPS = S5_W // S5_GC
S5_P = 64
S5_T = 16

LANE = 128
MXU = 256
VMEM_LIMIT = 56 * 1024 * 1024

_IN_SIZES = (MLA_KV_LORA, MLA_ROPE, SWA_KV_HEADS * SWA_HD, SWA_KV_HEADS * SWA_HD, S5_W, MLA_Q_LORA,
             SWA_HEADS * SWA_HD, (HY_ORDER + 1) * HY_W)
_IN_OFF = np.concatenate([[0], np.cumsum(_IN_SIZES)])
(I_CKV, I_KROPE, I_SWK, I_SWV, I_S5, I_CQ, I_SWQ, I_HY) = [int(v) for v in _IN_OFF[:-1]]

O_KR, O_KRS, O_CKV, O_CQ = 0, 128, 256, 384
O_SQ, O_SQS = 640, 1152
O_SK, O_SKS, O_SV = 1664, 1920, 2176
O_S5, O_HY = 2432, 2688
N_BIG = O_HY + (HY_ORDER + 1) * HY_W


def _cparams(*sem):
    return pltpu.CompilerParams(dimension_semantics=sem, vmem_limit_bytes=VMEM_LIMIT)


def _resident(shape):
    nd = len(shape)
    return pl.BlockSpec(shape, lambda *_: (0,) * nd, pipeline_mode=pl.Buffered(1))


def _dot(a, b):
    return jnp.dot(a, b, preferred_element_type=F32)


def _dot_nt(a, b):
    return lax.dot_general(a, b, (((1,), (1,)), ((), ())), preferred_element_type=F32)


def _bdot(a, b):
    return lax.dot_general(a, b, (((2,), (1,)), ((0,), (0,))), preferred_element_type=F32)


def _rms(x, g):
    return x * lax.rsqrt(jnp.mean(x * x, axis=-1, keepdims=True) + EPS) * g


def _sigmoid(x):
    return 1.0 / (1.0 + jnp.exp(-x))


def _pre_mod(x, gpre, mod, sub):
    return _rms(x, gpre) * (1.0 + mod[3 * sub + 1:3 * sub + 2, :]) + mod[3 * sub:3 * sub + 1, :]


def _mod_kernel(c_ref, w_ref, b_ref, o_ref):
    c = c_ref[...]
    s = c * _sigmoid(c)
    w = w_ref[0]
    s_hi = s.astype(CDT)
    s_lo = (s - s_hi.astype(F32)).astype(CDT)
    w_hi = w.astype(CDT)
    w_lo = (w - w_hi.astype(F32)).astype(CDT)
    o_ref[0] = _dot(s_hi, w_hi) + _dot(s_hi, w_lo) + _dot(s_lo, w_hi) + b_ref[0]


def _modulation(cvec, w_ada, b_ada):
    depth, d, n = w_ada.shape
    rows = cvec.shape[0]
    tn = n // 8
    out = pl.pallas_call(
        _mod_kernel,
        out_shape=jax.ShapeDtypeStruct((depth, rows, n), F32),
        grid=(depth, n // tn),
        in_specs=[pl.BlockSpec((rows, d), lambda l, j: (0, 0)),
                  pl.BlockSpec((1, d, tn), lambda l, j: (l, 0, j)),
                  pl.BlockSpec((1, 1, tn), lambda l, j: (l, 0, j))],
        out_specs=pl.BlockSpec((1, rows, tn), lambda l, j: (l, 0, j)),
        compiler_params=_cparams("arbitrary", "arbitrary"),
        name="modulation",
    )(cvec, w_ada, b_ada.reshape(depth, 1, n))
    return out.reshape(depth, rows, N_MOD, D_MODEL)


FFN_CHUNK = 256


def _ffn_kernel(x_ref, mod_ref, gpre_ref, gpost_ref, wup_ref, wdn_ref, o_ref, *, sub):
    x = x_ref[0]
    mod = mod_ref[0]
    u = _pre_mod(x, gpre_ref[...], mod, sub).astype(CDT)
    acc = jnp.zeros(x.shape, F32)
    for c in range(D_FF // FFN_CHUNK):
        lo = c * FFN_CHUNK
        a = _dot(u, wup_ref[:, lo:lo + FFN_CHUNK])
        b = _dot(u, wup_ref[:, D_FF + lo:D_FF + lo + FFN_CHUNK])
        h = (a * _sigmoid(a) * b).astype(CDT)
        acc = acc + _dot(h, wdn_ref[lo:lo + FFN_CHUNK, :])
    gate = mod[3 * sub + 2:3 * sub + 3, :]
    o_ref[0] = x + MACARON_W * gate * _rms(acc, gpost_ref[...])


def _ffn(x, mod, mod_row, gpre, gpost, wup, wdn, sub):
    b, l, d = x.shape
    tm = min(512, l)
    return pl.pallas_call(
        functools.partial(_ffn_kernel, sub=sub),
        out_shape=jax.ShapeDtypeStruct(x.shape, F32),
        grid=(b, l // tm),
        in_specs=[pl.BlockSpec((1, tm, d), lambda i, j: (i, j, 0)),
                  pl.BlockSpec((1, N_MOD, d), lambda i, j: (mod_row(i), 0, 0)),
                  _resident((1, d)), _resident((1, d)),
                  _resident(wup.shape), _resident(wdn.shape)],
        out_specs=pl.BlockSpec((1, tm, d), lambda i, j: (i, j, 0)),
        compiler_params=_cparams("parallel", "parallel"),
        name="ffn_sublayer",
    )(x, mod, gpre, gpost, wup, wdn)


def _inproj_kernel(x_ref, mod_ref, gpre_ref, wbig_ref, gkv_ref, gq_ref, wukv_ref, wuq_ref,
                   cm_ref, sm_ref, cw_ref, sw_ref,
                   qm_ref, km_ref, vm_ref, qw_ref, kw_ref, vw_ref, s5_ref, hy_ref):
    u = _pre_mod(x_ref[0], gpre_ref[...], mod_ref[0], 1).astype(CDT)

    def seg(off, n):
        return _dot(u, wbig_ref[:, off:off + n])

    cm, sm = cm_ref[...], sm_ref[...]
    cw, sw = cw_ref[...], sw_ref[...]
    kr = seg(O_KR, LANE) * cm + seg(O_KRS, LANE) * sm
    ckv = seg(O_CKV, LANE)
    kvn = _rms(ckv, gkv_ref[...]).astype(CDT)
    kv = _dot(kvn, wukv_ref[...])
    for h in range(MLA_HEADS):
        km_ref[0, h] = (kv[:, h * LANE:(h + 1) * LANE] + kr).astype(CDT)
        vm_ref[0, h] = kv[:, (MLA_HEADS + h) * LANE:(MLA_HEADS + h + 1) * LANE].astype(CDT)
    cq = seg(O_CQ, 2 * LANE)
    cqn = (cq * lax.rsqrt(jnp.sum(cq * cq, axis=-1, keepdims=True) * (1.0 / MLA_Q_LORA) + EPS)
           * gq_ref[...]).astype(CDT)
    qq = _dot(cqn, wuq_ref[...])
    for h in range(MLA_HEADS):
        q = qq[:, h * LANE:(h + 1) * LANE] * cm + qq[:, (MLA_HEADS + h) * LANE:(MLA_HEADS + h + 1) * LANE] * sm
        qm_ref[0, h] = (q * MLA_SCALE).astype(CDT)
    sq, sqs = seg(O_SQ, SWA_HEADS * LANE), seg(O_SQS, SWA_HEADS * LANE)
    for h in range(SWA_HEADS):
        q = sq[:, h * LANE:(h + 1) * LANE] * cw + sqs[:, h * LANE:(h + 1) * LANE] * sw
        qw_ref[0, h] = (q * SWA_SCALE).astype(CDT)
    sk, sks = seg(O_SK, SWA_KV_HEADS * LANE), seg(O_SKS, SWA_KV_HEADS * LANE)
    sv = seg(O_SV, SWA_KV_HEADS * LANE)
    for h in range(SWA_KV_HEADS):
        kw_ref[0, h] = (sk[:, h * LANE:(h + 1) * LANE] * cw + sks[:, h * LANE:(h + 1) * LANE] * sw).astype(CDT)
        vw_ref[0, h] = sv[:, h * LANE:(h + 1) * LANE].astype(CDT)
    s5_ref[0] = seg(O_S5, S5_W).astype(CDT)
    hy_ref[0] = seg(O_HY, (HY_ORDER + 1) * HY_W).astype(CDT)


def _inproj(x, mod, mod_row, gpre, wp, rope):
    b, l, d = x.shape
    tm = min(512, l)
    cm, sm, cw, sw = rope
    head = lambda n: jax.ShapeDtypeStruct((b, n, l, LANE), CDT)
    head_spec = lambda n: pl.BlockSpec((1, n, tm, LANE), lambda i, j: (i, 0, j, 0))
    tab_spec = pl.BlockSpec((tm, LANE), lambda i, j: (j, 0))
    row_spec = lambda n: pl.BlockSpec((1, tm, n), lambda i, j: (i, j, 0))
    return pl.pallas_call(
        _inproj_kernel,
        out_shape=(head(MLA_HEADS), head(MLA_HEADS), head(MLA_HEADS), head(SWA_HEADS), head(SWA_KV_HEADS),
                   head(SWA_KV_HEADS), jax.ShapeDtypeStruct((b, l, S5_W), CDT),
                   jax.ShapeDtypeStruct((b, l, (HY_ORDER + 1) * HY_W), CDT)),
        grid=(b, l // tm),
        in_specs=[row_spec(d),
                  pl.BlockSpec((1, N_MOD, d), lambda i, j: (mod_row(i), 0, 0)),
                  _resident((1, d)), _resident(wp['w_big'].shape),
                  _resident((1, LANE)), _resident((1, 2 * LANE)),
                  _resident(wp['w_ukv'].shape), _resident(wp['w_uq'].shape),
                  tab_spec, tab_spec, tab_spec, tab_spec],
        out_specs=(head_spec(MLA_HEADS), head_spec(MLA_HEADS), head_spec(MLA_HEADS), head_spec(SWA_HEADS),
                   head_spec(SWA_KV_HEADS), head_spec(SWA_KV_HEADS), row_spec(S5_W),
                   row_spec((HY_ORDER + 1) * HY_W)),
        compiler_params=_cparams("parallel", "parallel"),
        name="premod_inproj",
    )(x, mod, gpre, wp['w_big'], wp['g_kv'], wp['g_q'], wp['w_ukv'], wp['w_uq'], cm, sm, cw, sw)


def _mla_kernel(*refs, n_src):
    q_ref, o_ref = refs[0], refs[-1]
    q = q_ref[0, 0]
    ks = [refs[1 + 2 * i][0, 0] for i in range(n_src)]
    vs = [refs[2 + 2 * i][0, 0] for i in range(n_src)]
    ss = [_dot_nt(q, k) for k in ks]
    m = ss[0].max(axis=-1, keepdims=True)
    for s in ss[1:]:
        m = jnp.maximum(m, s.max(axis=-1, keepdims=True))
    den = jnp.zeros_like(m)
    acc = jnp.zeros((q.shape[0], LANE), F32)
    for s, v in zip(ss, vs):
        p = jnp.exp(s - m)
        den = den + p.sum(axis=-1, keepdims=True)
        acc = acc + _dot(p.astype(CDT), v)
    o_ref[0] = (acc / den).astype(CDT)


def _mla(q, kvs):
    b, h, l, _ = q.shape
    tq = min(256, l)
    in_specs = [pl.BlockSpec((1, 1, tq, LANE), lambda i, j, t: (i, j, t, 0))]
    args = [q]
    for k, v in kvs:
        n = k.shape[2]
        in_specs += [pl.BlockSpec((1, 1, n, LANE), lambda i, j, t: (i, j, 0, 0))] * 2
        args += [k, v]
    return pl.pallas_call(
        functools.partial(_mla_kernel, n_src=len(kvs)),
        out_shape=jax.ShapeDtypeStruct((b, l, h * LANE), CDT),
        grid=(b, h, l // tq),
        in_specs=in_specs,
        out_specs=pl.BlockSpec((1, tq, LANE), lambda i, j, t: (i, t, j)),
        compiler_params=_cparams("parallel", "parallel", "parallel"),
        name="mla_attention",
    )(*args)


def _swa_kernel(*refs, band, tq, nblk):
    sink_ref, q_ref = refs[0], refs[1]
    o_ref = refs[-1]
    kv = pl.program_id(1)
    t = pl.program_id(2)
    g = SWA_HEADS // SWA_KV_HEADS
    q = q_ref[0].reshape(g * tq, LANE)
    row = lax.broadcasted_iota(jnp.int32, (g * tq, 1), 0)
    snk = jnp.where(row < tq, sink_ref[kv * g], sink_ref[kv * g + 1])
    ss, vs = [], []
    if band:
        kp, kc, kn, vp, vc, vn, kx, vx = [r[0, 0] for r in refs[2:10]]
        qi = jnp.where(row < tq, row, row - tq)
        jp = lax.broadcasted_iota(jnp.int32, (1, WINDOW), 1)
        jc = lax.broadcasted_iota(jnp.int32, (1, tq), 1)
        s_p = jnp.where((jp >= qi) & (t > 0), _dot_nt(q, kp), NEG_INF)
        s_c = jnp.where(jnp.abs(qi - jc) <= WINDOW, _dot_nt(q, kc), NEG_INF)
        s_n = jnp.where((jp <= qi - (tq - WINDOW)) & (t < nblk - 1), _dot_nt(q, kn), NEG_INF)
        ss += [s_p, s_c, s_n]
        vs += [vp, vc, vn]
    else:
        kx, vx = [r[0, 0] for r in refs[2:4]]
    ss.append(_dot_nt(q, kx))
    vs.append(vx)
    m = snk
    for s in ss:
        m = jnp.maximum(m, s.max(axis=-1, keepdims=True))
    den = jnp.exp(snk - m)
    acc = jnp.zeros((g * tq, LANE), F32)
    for s, v in zip(ss, vs):
        p = jnp.exp(s - m)
        den = den + p.sum(axis=-1, keepdims=True)
        acc = acc + _dot(p.astype(CDT), v)
    o = (acc / den).astype(CDT)
    for i in range(g):
        o_ref[0, :, i * LANE:(i + 1) * LANE] = o[i * tq:(i + 1) * tq]


def _swa(sink, q, k, v, kx, vx, band):
    b, _, l, _ = q.shape
    g = SWA_HEADS // SWA_KV_HEADS
    tq = min(256, l)
    nblk = l // tq
    r = tq // WINDOW
    nw = l // WINDOW
    in_specs = [pl.BlockSpec(memory_space=pltpu.SMEM),
                pl.BlockSpec((1, g, tq, LANE), lambda i, j, t: (i, j, t, 0))]
    args = [sink, q]
    if band:
        prev = pl.BlockSpec((1, 1, WINDOW, LANE), lambda i, j, t: (i, j, jnp.maximum(t * r - 1, 0), 0))
        cur = pl.BlockSpec((1, 1, tq, LANE), lambda i, j, t: (i, j, t, 0))
        nxt = pl.BlockSpec((1, 1, WINDOW, LANE), lambda i, j, t: (i, j, jnp.minimum((t + 1) * r, nw - 1), 0))
        in_specs += [prev, cur, nxt, prev, cur, nxt]
        args += [k, k, k, v, v, v]
    nx = kx.shape[2]
    in_specs += [pl.BlockSpec((1, 1, nx, LANE), lambda i, j, t: (i, j, 0, 0))] * 2
    args += [kx, vx]
    return pl.pallas_call(
        functools.partial(_swa_kernel, band=band, tq=tq, nblk=nblk),
        out_shape=jax.ShapeDtypeStruct((b, l, SWA_HEADS * LANE), CDT),
        grid=(b, SWA_KV_HEADS, nblk),
        in_specs=in_specs,
        out_specs=pl.BlockSpec((1, tq, g * LANE), lambda i, j, t: (i, t, j)),
        compiler_params=_cparams("parallel", "parallel", "parallel"),
        name="swa_attention",
    )(*args)


def _shortconv_kernel(x_ref, w_ref, b_ref, o_ref):
    x = x_ref[0].astype(F32)
    l = x.shape[0]
    t = lax.broadcasted_iota(jnp.int32, (l, 1), 0)
    prev = jnp.where(t == 0, 0.0, pltpu.roll(x, 1, 0))
    nxt = jnp.where(t == l - 1, 0.0, pltpu.roll(x, l - 1, 0))
    y = b_ref[...] + prev * w_ref[0:1, :] + x * w_ref[1:2, :] + nxt * w_ref[2:3, :]
    o_ref[0] = y.T.astype(CDT)


def _shortconv_t(x, w, bias):
    b, l, c = x.shape
    tc = 256
    return pl.pallas_call(
        _shortconv_kernel,
        out_shape=jax.ShapeDtypeStruct((b, c, l), CDT),
        grid=(b, c // tc),
        in_specs=[pl.BlockSpec((1, l, tc), lambda i, j: (i, 0, j)),
                  pl.BlockSpec((SHORT_K, tc), lambda i, j: (0, j)),
                  pl.BlockSpec((1, tc), lambda i, j: (0, j))],
        out_specs=pl.BlockSpec((1, tc, l), lambda i, j: (i, j, 0)),
        compiler_params=_cparams("parallel", "parallel"),
        name="hyena_shortconv",
    )(x, w, bias)


def _swap(x):
    return jnp.concatenate([x[..., LANE:], x[..., :LANE]], axis=-1)


def _comb(pq, n):
    p, q = pq[:, :n], pq[:, n:]
    return jnp.concatenate([p[..., :LANE] - q[..., LANE:], p[..., LANE:] + q[..., :LANE]], axis=-1)


def _hyena_kernel(v_ref, g1_ref, g2_ref, f1_ref, c1_ref, tw_ref, f2_ref, f2c_ref, ka_ref, kb_ref, bias_ref,
                  o_ref, *, ct, na, kin):
    def load(ref):
        return jnp.concatenate([ref[0, 0], ref[0, 1]], axis=-1).astype(F32)

    z = load(v_ref)
    gates = (g1_ref, g2_ref)
    twa, twb = tw_ref[0], tw_ref[1]
    for o in range(HY_ORDER):
        a = _comb(_bdot(f1_ref[...], z.astype(CDT)), na)
        a = a * twa + _swap(a) * twb
        x = _dot(a.reshape(ct * na, 2 * LANE).astype(CDT), f2_ref[...])
        y = x * ka_ref[o].reshape(ct * na, 2 * LANE) + _swap(x) * kb_ref[o].reshape(ct * na, 2 * LANE)
        bq = _dot(y.astype(CDT), f2c_ref[...]).reshape(ct, na, 2 * LANE)
        bq = bq * twa - _swap(bq) * twb
        yt = _comb(_bdot(c1_ref[...], bq.astype(CDT)), kin)
        z = load(gates[o]) * (yt + bias_ref[o] * z)
    o_ref[0, 0] = z[..., :LANE].astype(CDT)
    o_ref[0, 1] = z[..., LANE:].astype(CDT)


def _hyena_tables(na, kin, ct):
    n = na * LANE
    ka = np.arange(na)[:, None]
    a = np.arange(kin)[None, :]
    ang1 = -2.0 * np.pi * ((ka * a) % na) / na
    f1 = np.concatenate([np.cos(ang1), np.sin(ang1)], axis=0)
    c1 = np.concatenate([np.cos(ang1).T, -np.sin(ang1).T], axis=0)
    bb = np.arange(LANE)[None, :]
    angt = -2.0 * np.pi * ((ka * bb) % n) / n
    tr, ti = np.cos(angt), np.sin(angt)
    tw = np.stack([np.concatenate([tr, tr], 1), np.concatenate([-ti, ti], 1)])
    b2 = np.arange(LANE)
    ang2 = -2.0 * np.pi * ((b2[:, None] * b2[None, :]) % LANE) / LANE
    fr, fi = np.cos(ang2), np.sin(ang2)
    f2 = np.block([[fr, fi], [-fi, fr]])
    f2c = np.block([[fr, -fi], [fi, fr]])
    bc = lambda m: jnp.broadcast_to(jnp.asarray(m, F32).astype(CDT)[None], (ct,) + m.shape)
    return (bc(f1), bc(c1), jnp.asarray(tw, F32), jnp.asarray(f2, F32).astype(CDT),
            jnp.asarray(f2c, F32).astype(CDT))


def _hyena_filter(l, n, hp):
    t = jnp.linspace(0.0, 1.0, l, dtype=F32)[:, None]
    bands = (HY_EMB - 1) // 2
    w = 2.0 * math.pi * jnp.arange(l, dtype=F32) / l
    fr = jnp.linspace(1e-4, bands - 1, bands, dtype=F32)
    ang = w[:, None] * fr[None, :]
    z = jnp.concatenate([t, jnp.cos(ang), -jnp.sin(ang)], axis=-1)
    freq = hp['hy_f_freq']
    hi = lax.Precision.HIGHEST
    h = jnp.sin(freq[0] * (jnp.dot(z, hp['hy_f_w1'], precision=hi) + hp['hy_f_b1']))
    h = jnp.sin(freq[1] * (jnp.dot(h, hp['hy_f_w2'], precision=hi) + hp['hy_f_b2']))
    h = jnp.dot(h, hp['hy_f_w3'], precision=hi).reshape(l, HY_ORDER, 2, HY_W)
    deltas = jnp.abs(jnp.linspace(math.log(HY_DECAY_TARGET) / HY_SLOW_PCT,
                                  math.log(HY_DECAY_TARGET) / HY_FAST_PCT, HY_W, dtype=F32))
    h = h * jnp.exp(-t[:, :, None, None] * deltas)
    h_fwd, h_bwd = h[:, :, 0], h[:, :, 1]
    mid = jnp.zeros((n - 2 * l + 1,) + h_fwd.shape[1:], F32)
    k_full = jnp.concatenate([h_fwd, mid, h_bwd[1:][::-1]], axis=0)
    return k_full


def _hyena_filter_spectrum(k_full, na):
    n = k_full.shape[0]
    kf = jnp.fft.fft(k_full, axis=0) / n
    kf = kf.reshape(LANE, na, HY_ORDER, HY_W).transpose(2, 3, 1, 0)
    kr, ki = jnp.real(kf).astype(F32), jnp.imag(kf).astype(F32)
    return jnp.concatenate([kr, kr], -1), jnp.concatenate([-ki, ki], -1)


def _hyena(hy, hp, l_true):
    b, l, _ = hy.shape
    ut = _shortconv_t(hy, hp['hy_conv_w'], hp['hy_conv_b'].reshape(1, -1))
    kin = max(l // LANE, 16)
    na = 2 * kin
    lp = kin * LANE
    if lp != l:
        ut = jnp.pad(ut, ((0, 0), (0, 0), (0, lp - l)))
    ut = ut.reshape(b // 2, 2, (HY_ORDER + 1) * HY_W, kin, LANE)
    ct = 16
    nc = HY_W // ct
    f1, c1, tw, f2, f2c = _hyena_tables(na, kin, ct)
    ka, kb = hp['ka'], hp['kb']
    bias = jnp.broadcast_to(hp['hy_bias'].reshape(HY_ORDER, HY_W, 1, 1), (HY_ORDER, HY_W, 1, 2 * LANE))
    blk = lambda off: pl.BlockSpec((1, 2, ct, kin, LANE), lambda c, p: (p, 0, c + off * nc, 0, 0))
    out = pl.pallas_call(
        functools.partial(_hyena_kernel, ct=ct, na=na, kin=kin),
        out_shape=jax.ShapeDtypeStruct((b // 2, 2, HY_W, kin, LANE), CDT),
        grid=(nc, b // 2),
        in_specs=[blk(0), blk(1), blk(2),
                  _resident(f1.shape), _resident(c1.shape), _resident(tw.shape),
                  _resident(f2.shape), _resident(f2c.shape),
                  pl.BlockSpec((HY_ORDER, ct, na, 2 * LANE), lambda c, p: (0, c, 0, 0)),
                  pl.BlockSpec((HY_ORDER, ct, na, 2 * LANE), lambda c, p: (0, c, 0, 0)),
                  pl.BlockSpec((HY_ORDER, ct, 1, 2 * LANE), lambda c, p: (0, c, 0, 0))],
        out_specs=pl.BlockSpec((1, 2, ct, kin, LANE), lambda c, p: (p, 0, c, 0, 0)),
        compiler_params=_cparams("parallel", "arbitrary"),
        name="hyena_longconv",
    )(ut, ut, ut, f1, c1, tw, f2, f2c, ka, kb, bias)
    out = out.reshape(b, HY_W, lp)[:, :, :l]
    return jnp.swapaxes(out, 1, 2)


def _s5_kernel(u_ref, tloc_ref, wx_ref, wout_ref, d_ref, h0_ref, y_ref, hfin_ref, x_scr, h_scr, *, nchunk, nb):
    w = S5_T * S5_GC
    u0, u1 = u_ref[0], u_ref[1]
    x_scr[...] = _dot(jnp.concatenate([u0, u1], axis=-1), wx_ref[0])
    d = d_ref[0]
    dfr, dfi, dbr, dbi = d[0:1], d[1:2], d[2:3], d[3:4]

    def body(j, carry):
        hr, hi, gr, gi = carry
        rf = pl.multiple_of(j * nb, nb)
        rb = pl.multiple_of((nchunk - 1 - j) * nb, nb)
        h_scr[pl.ds(rf, nb), 0:LANE] = hr
        h_scr[pl.ds(rf, nb), LANE:2 * LANE] = hi
        h_scr[pl.ds(rb, nb), 2 * LANE:3 * LANE] = gr
        h_scr[pl.ds(rb, nb), 3 * LANE:4 * LANE] = gi
        xr = x_scr[pl.ds(rf, nb), 0:LANE]
        xi = x_scr[pl.ds(rf, nb), LANE:2 * LANE]
        yr = x_scr[pl.ds(rb, nb), 2 * LANE:3 * LANE]
        yi = x_scr[pl.ds(rb, nb), 3 * LANE:4 * LANE]
        return (dfr * hr - dfi * hi + xr, dfr * hi + dfi * hr + xi,
                dbr * gr - dbi * gi + yr, dbr * gi + dbi * gr + yi)

    fin = lax.fori_loop(0, nchunk, body, tuple(h0_ref[0, k] for k in range(4)))
    for k in range(4):
        hfin_ref[0, k] = fin[k]
    hs = h_scr[...].astype(CDT)
    wout = wout_ref[0]
    y_ref[0] = (_dot(u0, tloc_ref[0]) + _dot(hs, wout[:, :w])).astype(CDT)
    y_ref[1] = (_dot(u1, tloc_ref[1]) + _dot(hs, wout[:, w:])).astype(CDT)


def _s5(u, ops, h0):
    b, l, _ = u.shape
    nchunk = l // S5_T
    w = S5_T * S5_GC
    rows = nchunk * b
    npair = S5_GROUPS // 2
    ug = u.reshape(b, nchunk, S5_T, S5_GROUPS, S5_GC).transpose(3, 1, 0, 2, 4).reshape(S5_GROUPS, rows, w)
    y, hfin = pl.pallas_call(
        functools.partial(_s5_kernel, nchunk=nchunk, nb=b),
        out_shape=(jax.ShapeDtypeStruct((S5_GROUPS, rows, w), CDT),
                   jax.ShapeDtypeStruct((npair, 4, b, LANE), F32)),
        grid=(npair,),
        in_specs=[pl.BlockSpec((2, rows, w), lambda q: (q, 0, 0)),
                  pl.BlockSpec((2, w, w), lambda q: (q, 0, 0)),
                  pl.BlockSpec((1, 2 * w, 2 * w), lambda q: (q, 0, 0)),
                  pl.BlockSpec((1, 2 * w, 2 * w), lambda q: (q, 0, 0)),
                  pl.BlockSpec((1, 4, LANE), lambda q: (q, 0, 0)),
                  pl.BlockSpec((1, 4, b, LANE), lambda q: (q, 0, 0, 0))],
        out_specs=(pl.BlockSpec((2, rows, w), lambda q: (q, 0, 0)),
                   pl.BlockSpec((1, 4, b, LANE), lambda q: (q, 0, 0, 0))),
        scratch_shapes=[pltpu.VMEM((rows, 2 * w), F32), pltpu.VMEM((rows, 2 * w), F32)],
        compiler_params=_cparams("parallel"),
        name="s5_chunked",
    )(ug, ops['tloc'], ops['wx'], ops['wout'], ops['d16'], h0)
    y = y.reshape(S5_GROUPS, nchunk, b, S5_T, S5_GC).transpose(2, 1, 3, 0, 4).reshape(b, l, S5_W)
    return y, hfin


def _s5_operators(p):
    g, pp, gc, t = S5_GROUPS, S5_P, S5_GC, S5_T
    npair = g // 2
    n = jnp.arange(t + 1, dtype=F32)[:, None, None]
    tops, wxs, wouts, d16 = [], [], [], []
    sig = jnp.arange(t)
    for d in range(2):
        a_re, a_im = p['s5_a_re'][d], p['s5_a_im'][d]
        dt = jnp.exp(p['s5_log_dt'][d])[:, None]
        mag1 = jnp.exp(dt * a_re)
        ab_re, ab_im = mag1 * jnp.cos(dt * a_im), mag1 * jnp.sin(dt * a_im)
        den = a_re * a_re + a_im * a_im
        f_re = ((ab_re - 1.0) * a_re + ab_im * a_im) / den
        f_im = (ab_im * a_re - (ab_re - 1.0) * a_im) / den
        mag = jnp.exp(n * (dt * a_re)[None])
        pr, pi = mag * jnp.cos(n * (dt * a_im)[None]), mag * jnp.sin(n * (dt * a_im)[None])
        b_re, b_im = p['s5_b_re'][d], p['s5_b_im'][d]
        bt_re = f_re[..., None] * b_re - f_im[..., None] * b_im
        bt_im = f_re[..., None] * b_im + f_im[..., None] * b_re
        c_re, c_im = p['s5_c_re'][d], p['s5_c_im'][d]
        ca_re = c_re[None] * pr[:, :, None, :] - c_im[None] * pi[:, :, None, :]
        ca_im = c_re[None] * pi[:, :, None, :] + c_im[None] * pr[:, :, None, :]
        hi = lax.Precision.HIGHEST
        kk = (jnp.einsum('ngcp,gpk->ngck', ca_re[:t], bt_re, precision=hi)
              - jnp.einsum('ngcp,gpk->ngck', ca_im[:t], bt_im, precision=hi))
        lag = (sig[None, :] - sig[:, None]) if d == 0 else (sig[:, None] - sig[None, :])
        top = jnp.where((lag >= 0)[:, :, None, None, None], kk[jnp.clip(lag, 0, t - 1)], 0.0)
        tops.append(top.transpose(2, 0, 4, 1, 3).reshape(g, t * gc, t * gc))
        pw = (t - 1 - sig) if d == 0 else sig
        xr = pr[pw][..., None] * bt_re[None] - pi[pw][..., None] * bt_im[None]
        xi = pr[pw][..., None] * bt_im[None] + pi[pw][..., None] * bt_re[None]
        wxs.append((xr.transpose(1, 0, 3, 2).reshape(g, t * gc, pp), xi.transpose(1, 0, 3, 2).reshape(g, t * gc, pp)))
        po = (sig + 1) if d == 0 else (t - sig)
        wouts.append((ca_re[po].transpose(1, 3, 0, 2).reshape(g, pp, t * gc),
                      -ca_im[po].transpose(1, 3, 0, 2).reshape(g, pp, t * gc)))
        d16.append((pr[t], pi[t]))
    eye = jnp.eye(t * gc, dtype=F32).reshape(t, gc, t, gc)
    skip = (eye[None] * p['s5_d'].reshape(g, 1, 1, 1, gc)).reshape(g, t * gc, t * gc)
    tloc = tops[0] + tops[1] + skip

    def pair_cols(m):
        m = m.reshape(npair, 2, m.shape[1], m.shape[2])
        z = jnp.zeros_like(m[:, 0])
        return jnp.concatenate([jnp.concatenate([m[:, 0], z], -1), jnp.concatenate([z, m[:, 1]], -1)], -2)

    wx = jnp.concatenate([pair_cols(wxs[0][0]), pair_cols(wxs[0][1]), pair_cols(wxs[1][0]), pair_cols(wxs[1][1])],
                         axis=-1)
    wout = jnp.concatenate([pair_cols(wouts[0][0]), pair_cols(wouts[0][1]), pair_cols(wouts[1][0]),
                            pair_cols(wouts[1][1])], axis=-2)
    dd = jnp.stack([d16[0][0], d16[0][1], d16[1][0], d16[1][1]], axis=0)
    dd = dd.reshape(4, npair, 2 * pp).transpose(1, 0, 2)
    return dict(tloc=tloc.astype(CDT), wx=wx.astype(CDT), wout=wout.astype(CDT), d16=dd)


def _merge_kernel(x_ref, mod_ref, gpre_ref, gpost_ref, omla_ref, ohy_ref, oswa_ref, ys5_ref,
                  wgate_ref, bgate_ref, wmla_ref, why_ref, wswa_ref, ws5_ref, gluw_ref, glub_ref, wout_ref, o_ref):
    x = x_ref[0]
    mod = mod_ref[0]
    u = _pre_mod(x, gpre_ref[...], mod, 1).astype(CDT)
    y = ys5_ref[0].astype(F32)
    g = 0.5 * y * (1.0 + jnp.tanh(math.sqrt(2.0 / math.pi) * (y + 0.044715 * (y * y * y))))
    o_s5 = (g * _sigmoid(_dot(g.astype(CDT), gluw_ref[...]) + glub_ref[...])).astype(CDT)
    outs = (omla_ref[0], ohy_ref[0], oswa_ref[0], o_s5)
    wbr = (wmla_ref, why_ref, wswa_ref, ws5_ref)
    m = jnp.zeros(x.shape, F32)
    for i in range(4):
        gate = _sigmoid(_dot(u, wgate_ref[:, i * D_MODEL:(i + 1) * D_MODEL]) + bgate_ref[:, i * D_MODEL:(i + 1) * D_MODEL])
        m = m + gate * _dot(outs[i], wbr[i][...])
    f = _dot(m.astype(CDT), wout_ref[...])
    o_ref[0] = x + mod[5:6, :] * _rms(f, gpost_ref[...])


def _merge(x, mod, mod_row, gpre, gpost, o_mla, o_hy, o_swa, y_s5, wp):
    b, l, d = x.shape
    tm = min(512, l)
    row_spec = lambda n: pl.BlockSpec((1, tm, n), lambda i, j: (i, j, 0))
    names = ('w_gate', 'b_gate', 'w_br_mla', 'w_br_hy', 'w_br_swa', 'w_br_s5', 'glu_w', 'glu_b', 'w_out')
    return pl.pallas_call(
        _merge_kernel,
        out_shape=jax.ShapeDtypeStruct(x.shape, F32),
        grid=(b, l // tm),
        in_specs=[row_spec(d), pl.BlockSpec((1, N_MOD, d), lambda i, j: (mod_row(i), 0, 0)),
                  _resident((1, d)), _resident((1, d)),
                  row_spec(o_mla.shape[-1]), row_spec(o_hy.shape[-1]), row_spec(o_swa.shape[-1]),
                  row_spec(y_s5.shape[-1])] + [_resident(wp[k].shape) for k in names],
        out_specs=row_spec(d),
        compiler_params=_cparams("parallel", "parallel"),
        name="merge_out",
    )(x, mod, gpre, gpost, o_mla, o_hy, o_swa, y_s5, *[wp[k] for k in names])


def _rot_partner(w, half):
    return jnp.concatenate([-w[:, half:], w[:, :half]], axis=1)


def _pad_cols(w, n):
    return jnp.pad(w, ((0, 0), (0, n - w.shape[1])))


def _pad_rows(w, n):
    return jnp.pad(w, ((0, n - w.shape[0]), (0, 0)))


def _prep_inproj(w_in, w_ukv, w_uq, g_kv, g_q):
    d = w_in.shape[0]
    zeros = lambda n: jnp.zeros((d, n), F32)
    krope = w_in[:, I_KROPE:I_KROPE + MLA_ROPE]
    kr = jnp.concatenate([zeros(MLA_NOPE), krope, zeros(LANE - MLA_NOPE - MLA_ROPE)], 1)
    krs = jnp.concatenate([zeros(MLA_NOPE), _rot_partner(krope, MLA_ROPE // 2), zeros(LANE - MLA_NOPE - MLA_ROPE)], 1)
    ckv = w_in[:, I_CKV:I_CKV + MLA_KV_LORA]
    cq = _pad_cols(w_in[:, I_CQ:I_CQ + MLA_Q_LORA], 2 * LANE)

    def heads(w, nh, partner):
        cols = []
        for h in range(nh):
            wh = w[:, h * SWA_HD:(h + 1) * SWA_HD]
            if partner:
                wh = _rot_partner(wh, SWA_HD // 2)
            cols.append(_pad_cols(wh, LANE))
        return jnp.concatenate(cols, 1)

    swq = w_in[:, I_SWQ:I_SWQ + SWA_HEADS * SWA_HD]
    swk = w_in[:, I_SWK:I_SWK + SWA_KV_HEADS * SWA_HD]
    swv = w_in[:, I_SWV:I_SWV + SWA_KV_HEADS * SWA_HD]
    w_big = jnp.concatenate([kr, krs, ckv, cq,
                             heads(swq, SWA_HEADS, False), heads(swq, SWA_HEADS, True),
                             heads(swk, SWA_KV_HEADS, False), heads(swk, SWA_KV_HEADS, True),
                             heads(swv, SWA_KV_HEADS, False),
                             w_in[:, I_S5:I_S5 + S5_W], w_in[:, I_HY:I_HY + (HY_ORDER + 1) * HY_W]], axis=1)
    assert w_big.shape[1] == N_BIG
    kvw = w_ukv.reshape(MLA_KV_LORA, MLA_HEADS, MLA_NOPE + MLA_V)
    kslots = [_pad_cols(kvw[:, h, :MLA_NOPE], LANE) for h in range(MLA_HEADS)]
    vslots = [_pad_cols(kvw[:, h, MLA_NOPE:], LANE) for h in range(MLA_HEADS)]
    w_ukv_p = jnp.concatenate(kslots + vslots, axis=1)
    qw = w_uq.reshape(MLA_Q_LORA, MLA_HEADS, MLA_NOPE + MLA_ROPE)
    qslots = [_pad_cols(qw[:, h], LANE) for h in range(MLA_HEADS)]
    pslots = [_pad_cols(jnp.concatenate([jnp.zeros((MLA_Q_LORA, MLA_NOPE), F32),
                                         _rot_partner(qw[:, h, MLA_NOPE:], MLA_ROPE // 2)], 1), LANE)
              for h in range(MLA_HEADS)]
    w_uq_p = _pad_rows(jnp.concatenate(qslots + pslots, axis=1), 2 * LANE)
    return dict(w_big=w_big.astype(CDT), w_ukv=w_ukv_p.astype(CDT), w_uq=w_uq_p.astype(CDT),
                g_kv=g_kv.reshape(1, -1), g_q=_pad_cols(g_q.reshape(1, -1), 2 * LANE))


def _pad_head_rows(w, nh, hd):
    w = w.reshape(nh, hd, w.shape[-1])
    return jnp.pad(w, ((0, 0), (0, LANE - hd), (0, 0))).reshape(nh * LANE, -1)


def _rope_tables(n_tokens, use_rope):
    ones = jnp.ones((n_tokens, LANE), F32)
    zeros = jnp.zeros((n_tokens, LANE), F32)
    if not use_rope:
        return ones, zeros, ones, zeros

    def axial(rot_dim):
        rows = n_tokens // GRID_W
        r = jnp.repeat(jnp.arange(rows, dtype=F32), GRID_W)
        col = jnp.tile(jnp.arange(GRID_W, dtype=F32), rows)
        n_freq = rot_dim // 4
        freqs = ROPE_THETA ** (-jnp.arange(n_freq, dtype=F32) / n_freq)
        ang = jnp.concatenate([r[:, None] * freqs, col[:, None] * freqs], axis=-1)
        return jnp.cos(ang), jnp.sin(ang)

    c, s = axial(MLA_ROPE)
    pad = LANE - MLA_NOPE - MLA_ROPE
    cm = jnp.concatenate([ones[:, :MLA_NOPE], c, c, ones[:, :pad]], 1)
    sm = jnp.concatenate([zeros[:, :MLA_NOPE], s, s, zeros[:, :pad]], 1)
    c, s = axial(SWA_HD)
    cw = jnp.concatenate([c, c, ones[:, :LANE - SWA_HD]], 1)
    sw = jnp.concatenate([s, s, zeros[:, :LANE - SWA_HD]], 1)
    return cm, sm, cw, sw


def _hyena_params(p, l):
    kin = max(l // LANE, 16)
    na = 2 * kin
    k_full = _hyena_filter(l, na * LANE, p)
    ka, kb = _hyena_filter_spectrum(k_full, na)
    return dict(hy_conv_w=p['hy_conv_w'], hy_conv_b=p['hy_conv_b'], hy_bias=p['hy_bias'], ka=ka, kb=kb)


def kernel(x, c, ctx, c_ctx, w_ada, b_ada, norm_pre, norm_post, ffn1_up, ffn1_down, ffn2_up, ffn2_down,
           w_in, mla_q_norm, mla_kv_norm, mla_w_uq, mla_w_ukv, hy_conv_w, hy_conv_b, hy_f_w1, hy_f_b1,
           hy_f_freq, hy_f_w2, hy_f_b2, hy_f_w3, hy_bias, swa_sink, s5_a_re, s5_a_im, s5_log_dt,
           s5_b_re, s5_b_im, s5_c_re, s5_c_im, s5_d, s5_glu_w, s5_glu_b, w_gate, b_gate,
           w_br_mla, w_br_hy, w_br_swa, w_br_s5, w_out):
    stacked = dict(w_ada=w_ada, b_ada=b_ada, norm_pre=norm_pre, norm_post=norm_post,
                   ffn1_up=ffn1_up, ffn1_down=ffn1_down, ffn2_up=ffn2_up, ffn2_down=ffn2_down,
                   w_in=w_in, mla_q_norm=mla_q_norm, mla_kv_norm=mla_kv_norm, mla_w_uq=mla_w_uq,
                   mla_w_ukv=mla_w_ukv, hy_conv_w=hy_conv_w, hy_conv_b=hy_conv_b, hy_f_w1=hy_f_w1,
                   hy_f_b1=hy_f_b1, hy_f_freq=hy_f_freq, hy_f_w2=hy_f_w2, hy_f_b2=hy_f_b2, hy_f_w3=hy_f_w3,
                   hy_bias=hy_bias, swa_sink=swa_sink, s5_a_re=s5_a_re, s5_a_im=s5_a_im,
                   s5_log_dt=s5_log_dt, s5_b_re=s5_b_re, s5_b_im=s5_b_im, s5_c_re=s5_c_re,
                   s5_c_im=s5_c_im, s5_d=s5_d, s5_glu_w=s5_glu_w, s5_glu_b=s5_glu_b,
                   w_gate=w_gate, b_gate=b_gate, w_br_mla=w_br_mla, w_br_hy=w_br_hy,
                   w_br_swa=w_br_swa, w_br_s5=w_br_s5, w_out=w_out)
    depth = w_ada.shape[0]
    nb, seq, d = x.shape
    nctx = ctx.shape[1]
    assert nb % 2 == 0 and seq % 256 == 0 and nctx % 256 == 0

    rows = -(-(nb + 1) // 8) * 8
    cvec = jnp.zeros((rows, d), F32).at[:nb].set(c).at[nb].set(c_ctx)
    mods = _modulation(cvec, w_ada, b_ada)
    lat_row = lambda i: i
    ctx_row = lambda i: nb

    rope_l = _rope_tables(seq, True)
    rope_c = _rope_tables(nctx, False)
    h0_zero = jnp.zeros((S5_GROUPS // 2, 4, nb, LANE), F32)

    xl, xc = x, ctx
    for l in range(depth):
        p = {name: arr[l] for name, arr in stacked.items()}
        mod = mods[l]
        ctx_out = l < depth - 1
        gpre = [p['norm_pre'][i].reshape(1, d) for i in range(N_SUB)]
        gpost = [p['norm_post'][i].reshape(1, d) for i in range(N_SUB)]
        f1u, f1d = p['ffn1_up'].astype(CDT), p['ffn1_down'].astype(CDT)
        f2u, f2d = p['ffn2_up'].astype(CDT), p['ffn2_down'].astype(CDT)
        wp_in = _prep_inproj(p['w_in'], p['mla_w_ukv'], p['mla_w_uq'], p['mla_kv_norm'], p['mla_q_norm'])
        wp_mg = dict(
            w_gate=jnp.concatenate([p['w_gate'][i] for i in range(4)], axis=1).astype(CDT),
            b_gate=p['b_gate'].reshape(1, -1),
            w_br_mla=_pad_head_rows(p['w_br_mla'], MLA_HEADS, MLA_V).astype(CDT),
            w_br_hy=p['w_br_hy'].astype(CDT),
            w_br_swa=_pad_head_rows(p['w_br_swa'], SWA_HEADS, SWA_HD).astype(CDT),
            w_br_s5=p['w_br_s5'].astype(CDT),
            glu_w=p['s5_glu_w'].astype(CDT), glu_b=p['s5_glu_b'].reshape(1, -1),
            w_out=p['w_out'].astype(CDT))
        s5_ops = _s5_operators(p)

        xl = _ffn(xl, mod, lat_row, gpre[0], gpost[0], f1u, f1d, 0)
        xc = _ffn(xc, mod, ctx_row, gpre[0], gpost[0], f1u, f1d, 0)

        qm_c, km_c, vm_c, qw_c, kw_c, vw_c, s5u_c, hy_c = _inproj(xc, mod, ctx_row, gpre[1], wp_in, rope_c)
        qm_l, km_l, vm_l, qw_l, kw_l, vw_l, s5u_l, hy_l = _inproj(xl, mod, lat_row, gpre[1], wp_in, rope_l)

        ys5_c, h_ctx = _s5(s5u_c, s5_ops, h0_zero)
        ys5_l, _ = _s5(s5u_l, s5_ops, h_ctx)
        o_mla = _mla(qm_l, [(km_l, vm_l), (km_c, vm_c)])
        o_swa = _swa(p['swa_sink'], qw_l, kw_l, vw_l, kw_c, vw_c, True)
        o_hy = _hyena(hy_l, _hyena_params(p, seq), seq)
        xl = _merge(xl, mod, lat_row, gpre[1], gpost[1], o_mla, o_hy, o_swa, ys5_l, wp_mg)
        xl = _ffn(xl, mod, lat_row, gpre[2], gpost[2], f2u, f2d, 2)
        if ctx_out:
            o_mla_c = _mla(qm_c, [(km_c, vm_c)])
            o_swa_c = _swa(p['swa_sink'], qw_c, kw_c, vw_c, kw_c, vw_c, False)
            o_hy_c = _hyena(hy_c, _hyena_params(p, nctx), nctx)
            xc = _merge(xc, mod, ctx_row, gpre[1], gpost[1], o_mla_c, o_hy_c, o_swa_c, ys5_c, wp_mg)
            xc = _ffn(xc, mod, ctx_row, gpre[2], gpost[2], f2u, f2d, 2)
    return xl
```

```python
import functools
import math

import numpy as np
import jax
import jax.numpy as jnp
from jax import lax
from jax.experimental import pallas as pl
from jax.experimental.pallas import tpu as pltpu

F32 = jnp.float32
CDT = jnp.bfloat16

D_MODEL = 1024
D_FF = 2816
N_SUB = 3
N_MOD = 3 * N_SUB
MACARON_W = 0.5
ROPE_THETA = 10000.0
GRID_W = 64
EPS = 1e-6
NEG_INF = -1e30

MLA_HEADS = 4
MLA_NOPE = 64
MLA_ROPE = 32
MLA_V = 64
MLA_Q_LORA = 192
MLA_KV_LORA = 128
MLA_SCALE = (MLA_NOPE + MLA_ROPE) ** -0.5
LOG2E = math.log2(math.e)

HY_W = 256
HY_ORDER = 2
HY_EMB = 33
HY_DECAY_TARGET = 1e-2
HY_FAST_PCT = 0.3
HY_SLOW_PCT = 1.5
SHORT_K = 3

SWA_HEADS = 4
SWA_KV_HEADS = 2
SWA_HD = 64
WINDOW = 128
SWA_SCALE = SWA_HD ** -0.5

S5_W = 256
S5_GC = 16
S5_GROUPS = S5_W // S5_GC
S5_P = 64
S5_ROW_BLOCK = 512
S5_T = 16

LANE = 128
MXU = 256
VMEM_LIMIT = 56 * 1024 * 1024

_IN_SIZES = (MLA_KV_LORA, MLA_ROPE, SWA_KV_HEADS * SWA_HD, SWA_KV_HEADS * SWA_HD, S5_W, MLA_Q_LORA,
             SWA_HEADS * SWA_HD, (HY_ORDER + 1) * HY_W)
_IN_OFF = np.concatenate([[0], np.cumsum(_IN_SIZES)])
(I_CKV, I_KROPE, I_SWK, I_SWV, I_S5, I_CQ, I_SWQ, I_HY) = [int(v) for v in _IN_OFF[:-1]]

O_KR, O_KRS, O_CKV, O_CQ = 0, 128, 256, 384
O_SQ, O_SQS = 640, 1152
O_SK, O_SKS, O_SV = 1664, 1920, 2176
O_S5, O_HY = 2432, 2688
N_BIG = O_HY + (HY_ORDER + 1) * HY_W


def _cparams(*sem):
    return pltpu.CompilerParams(dimension_semantics=sem, vmem_limit_bytes=VMEM_LIMIT)


def _resident(shape):
    nd = len(shape)
    return pl.BlockSpec(shape, lambda *_: (0,) * nd, pipeline_mode=pl.Buffered(1))


def _dot(a, b):
    return jnp.dot(a, b, preferred_element_type=F32)


def _dot_nt(a, b):
    return lax.dot_general(a, b, (((1,), (1,)), ((), ())), preferred_element_type=F32)


def _bdot(a, b):
    return lax.dot_general(a, b, (((2,), (1,)), ((0,), (0,))), preferred_element_type=F32)


def _rms(x, g):
    return x * lax.rsqrt(jnp.mean(x * x, axis=-1, keepdims=True) + EPS) * g


def _sigmoid(x):
    return 1.0 / (1.0 + jnp.exp(-x))


def _pre_mod(x, gpre, mod, sub):
    return _rms(x, gpre) * (1.0 + mod[3 * sub + 1:3 * sub + 2, :]) + mod[3 * sub:3 * sub + 1, :]


def _mod_kernel(c_ref, w_ref, b_ref, o_ref):
    c = c_ref[...]
    s = c * _sigmoid(c)
    w = w_ref[0]
    s_hi = s.astype(CDT)
    s_lo = (s - s_hi.astype(F32)).astype(CDT)
    w_hi = w.astype(CDT)
    w_lo = (w - w_hi.astype(F32)).astype(CDT)
    o_ref[0] = _dot(s_hi, w_hi) + _dot(s_hi, w_lo) + _dot(s_lo, w_hi) + b_ref[0]


def _modulation(cvec, w_ada, b_ada):
    depth, d, n = w_ada.shape
    rows = cvec.shape[0]
    tn = n // 8
    out = pl.pallas_call(
        _mod_kernel,
        out_shape=jax.ShapeDtypeStruct((depth, rows, n), F32),
        grid=(depth, n // tn),
        in_specs=[pl.BlockSpec((rows, d), lambda l, j: (0, 0)),
                  pl.BlockSpec((1, d, tn), lambda l, j: (l, 0, j)),
                  pl.BlockSpec((1, 1, tn), lambda l, j: (l, 0, j))],
        out_specs=pl.BlockSpec((1, rows, tn), lambda l, j: (l, 0, j)),
        compiler_params=_cparams("arbitrary", "arbitrary"),
        name="modulation",
    )(cvec, w_ada, b_ada.reshape(depth, 1, n))
    return out.reshape(depth, rows, N_MOD, D_MODEL)


FFN_CHUNK = 256


def _ffn_kernel(x_ref, mod_ref, gpre_ref, gpost_ref, wup_ref, wdn_ref, o_ref, *, sub):
    x = x_ref[0]
    mod = mod_ref[0]
    u = _pre_mod(x, gpre_ref[...], mod, sub).astype(CDT)
    acc = jnp.zeros(x.shape, F32)
    for c in range(D_FF // FFN_CHUNK):
        lo = c * FFN_CHUNK
        a = _dot(u, wup_ref[:, lo:lo + FFN_CHUNK])
        b = _dot(u, wup_ref[:, D_FF + lo:D_FF + lo + FFN_CHUNK])
        h = (a * _sigmoid(a) * b).astype(CDT)
        acc = acc + _dot(h, wdn_ref[lo:lo + FFN_CHUNK, :])
    gate = mod[3 * sub + 2:3 * sub + 3, :]
    o_ref[0] = x + MACARON_W * gate * _rms(acc, gpost_ref[...])


def _ffn(x, mod, mod_row, gpre, gpost, wup, wdn, sub):
    b, l, d = x.shape
    tm = min(512, l)
    return pl.pallas_call(
        functools.partial(_ffn_kernel, sub=sub),
        out_shape=jax.ShapeDtypeStruct(x.shape, F32),
        grid=(b, l // tm),
        in_specs=[pl.BlockSpec((1, tm, d), lambda i, j: (i, j, 0)),
                  pl.BlockSpec((1, N_MOD, d), lambda i, j: (mod_row(i), 0, 0)),
                  _resident((1, d)), _resident((1, d)),
                  _resident(wup.shape), _resident(wdn.shape)],
        out_specs=pl.BlockSpec((1, tm, d), lambda i, j: (i, j, 0)),
        compiler_params=_cparams("parallel", "parallel"),
        name="ffn_sublayer",
    )(x, mod, gpre, gpost, wup, wdn)


def _inproj_kernel(x_ref, mod_ref, gpre_ref, wbig_ref, gkv_ref, gq_ref, wukv_ref, wuq_ref,
                   cm_ref, sm_ref, cw_ref, sw_ref,
                   qm_ref, km_ref, vm_ref, qw_ref, kw_ref, vw_ref, s5_ref, hy_ref):
    u = _pre_mod(x_ref[0], gpre_ref[...], mod_ref[0], 1).astype(CDT)

    def seg(off, n):
        return _dot(u, wbig_ref[:, off:off + n])

    cm, sm = cm_ref[...], sm_ref[...]
    cw, sw = cw_ref[...], sw_ref[...]
    kr = seg(O_KR, LANE) * cm + seg(O_KRS, LANE) * sm
    ckv = seg(O_CKV, LANE)
    kvn = _rms(ckv, gkv_ref[...]).astype(CDT)
    kv = _dot(kvn, wukv_ref[...])
    ones_lane = (lax.broadcasted_iota(jnp.int32, (1, LANE), 1) == MLA_V).astype(F32)
    for h in range(MLA_HEADS):
        km_ref[0, h] = (kv[:, h * LANE:(h + 1) * LANE] + kr).astype(CDT)
        vm_ref[0, h] = (kv[:, (MLA_HEADS + h) * LANE:(MLA_HEADS + h + 1) * LANE] + ones_lane).astype(CDT)
    cq = seg(O_CQ, 2 * LANE)
    cqn = (cq * lax.rsqrt(jnp.sum(cq * cq, axis=-1, keepdims=True) * (1.0 / MLA_Q_LORA) + EPS)
           * gq_ref[...]).astype(CDT)
    qq = _dot(cqn, wuq_ref[...])
    for h in range(MLA_HEADS):
        q = qq[:, h * LANE:(h + 1) * LANE] * cm + qq[:, (MLA_HEADS + h) * LANE:(MLA_HEADS + h + 1) * LANE] * sm
        qm_ref[0, h] = (q * (MLA_SCALE * LOG2E)).astype(CDT)
    sq, sqs = seg(O_SQ, SWA_HEADS * LANE), seg(O_SQS, SWA_HEADS * LANE)
    for h in range(SWA_HEADS):
        q = sq[:, h * LANE:(h + 1) * LANE] * cw + sqs[:, h * LANE:(h + 1) * LANE] * sw
        qw_ref[0, h] = (q * SWA_SCALE).astype(CDT)
    sk, sks = seg(O_SK, SWA_KV_HEADS * LANE), seg(O_SKS, SWA_KV_HEADS * LANE)
    sv = seg(O_SV, SWA_KV_HEADS * LANE)
    for h in range(SWA_KV_HEADS):
        kw_ref[0, h] = (sk[:, h * LANE:(h + 1) * LANE] * cw + sks[:, h * LANE:(h + 1) * LANE] * sw).astype(CDT)
        vw_ref[0, h] = sv[:, h * LANE:(h + 1) * LANE].astype(CDT)
    s5_ref[0] = seg(O_S5, S5_W).astype(CDT)
    hy_ref[0] = seg(O_HY, (HY_ORDER + 1) * HY_W).astype(CDT)


def _inproj(x, mod, mod_row, gpre, wp, rope):
    b, l, d = x.shape
    tm = min(512, l)
    cm, sm, cw, sw = rope
    head = lambda n: jax.ShapeDtypeStruct((b, n, l, LANE), CDT)
    head_spec = lambda n: pl.BlockSpec((1, n, tm, LANE), lambda i, j: (i, 0, j, 0))
    tab_spec = pl.BlockSpec((tm, LANE), lambda i, j: (j, 0))
    row_spec = lambda n: pl.BlockSpec((1, tm, n), lambda i, j: (i, j, 0))
    return pl.pallas_call(
        _inproj_kernel,
        out_shape=(head(MLA_HEADS), head(MLA_HEADS), head(MLA_HEADS), head(SWA_HEADS), head(SWA_KV_HEADS),
                   head(SWA_KV_HEADS), jax.ShapeDtypeStruct((b, l, S5_W), CDT),
                   jax.ShapeDtypeStruct((b, l, (HY_ORDER + 1) * HY_W), CDT)),
        grid=(b, l // tm),
        in_specs=[row_spec(d),
                  pl.BlockSpec((1, N_MOD, d), lambda i, j: (mod_row(i), 0, 0)),
                  _resident((1, d)), _resident(wp['w_big'].shape),
                  _resident((1, LANE)), _resident((1, 2 * LANE)),
                  _resident(wp['w_ukv'].shape), _resident(wp['w_uq'].shape),
                  tab_spec, tab_spec, tab_spec, tab_spec],
        out_specs=(head_spec(MLA_HEADS), head_spec(MLA_HEADS), head_spec(MLA_HEADS), head_spec(SWA_HEADS),
                   head_spec(SWA_KV_HEADS), head_spec(SWA_KV_HEADS), row_spec(S5_W),
                   row_spec((HY_ORDER + 1) * HY_W)),
        compiler_params=_cparams("parallel", "parallel"),
        name="premod_inproj",
    )(x, mod, gpre, wp['w_big'], wp['g_kv'], wp['g_q'], wp['w_ukv'], wp['w_uq'], cm, sm, cw, sw)


MLA_KEY_CHUNK = 512
MLA_VROWS = 80


def _mla_kernel(*refs, n_src):
    q_ref, o_ref = refs[0], refs[1 + 2 * n_src]
    k_refs = [refs[1 + 2 * i] for i in range(n_src)]
    v_refs = [refs[2 + 2 * i] for i in range(n_src)]
    vt_scr = refs[2 + 2 * n_src:-1]
    s_scr = refs[-1]

    @pl.when(pl.program_id(2) == 0)
    def _():
        for v_ref, vt in zip(v_refs, vt_scr):
            vt[...] = v_ref[0, 0].astype(F32).T.astype(CDT)

    qt = q_ref[0, 0].astype(F32).T.astype(CDT)
    m, off = None, 0
    for k_ref in k_refs:
        n = k_ref.shape[2]
        kc = min(MLA_KEY_CHUNK, n)
        for c in range(n // kc):
            s = _dot(k_ref[0, 0, c * kc:(c + 1) * kc, :], qt)
            s_scr[off + c * kc:off + (c + 1) * kc, :] = s
            cmax = s.max(axis=0, keepdims=True)
            m = cmax if m is None else jnp.maximum(m, cmax)
        off += n
    acc, off = None, 0
    for k_ref, vt in zip(k_refs, vt_scr):
        n = k_ref.shape[2]
        kc = min(MLA_KEY_CHUNK, n)
        for c in range(n // kc):
            p = jnp.exp2(s_scr[off + c * kc:off + (c + 1) * kc, :] - m).astype(CDT)
            pv = _dot(vt[0:MLA_VROWS, c * kc:(c + 1) * kc], p)
            acc = pv if acc is None else acc + pv
        off += n
    o = acc / acc[MLA_V:MLA_V + 1, :]
    o = jnp.concatenate([o, jnp.zeros((LANE - MLA_VROWS, o.shape[1]), F32)], axis=0)
    o_ref[0] = o.T.astype(CDT)


def _mla(q, kvs):
    b, h, l, _ = q.shape
    tq = min(512, l)
    in_specs = [pl.BlockSpec((1, 1, tq, LANE), lambda i, j, t: (i, j, t, 0))]
    args = [q]
    scratch = []
    for k, v in kvs:
        n = k.shape[2]
        in_specs += [pl.BlockSpec((1, 1, n, LANE), lambda i, j, t: (i, j, 0, 0))] * 2
        args += [k, v]
        scratch.append(pltpu.VMEM((LANE, n), CDT))
    return pl.pallas_call(
        functools.partial(_mla_kernel, n_src=len(kvs)),
        out_shape=jax.ShapeDtypeStruct((b, l, h * LANE), CDT),
        grid=(b, h, l // tq),
        in_specs=in_specs,
        out_specs=pl.BlockSpec((1, tq, LANE), lambda i, j, t: (i, t, j)),
        scratch_shapes=scratch + [pltpu.VMEM((sum(k.shape[2] for k, _ in kvs), tq), F32)],
        compiler_params=_cparams("parallel", "parallel", "arbitrary"),
        name="mla_attention",
    )(*args)


def _swa_kernel(*refs, band, tq, nblk):
    sink_ref, q_ref = refs[0], refs[1]
    o_ref = refs[-1]
    kv = pl.program_id(1)
    t = pl.program_id(2)
    g = SWA_HEADS // SWA_KV_HEADS
    q = q_ref[0].reshape(g * tq, LANE)
    row = lax.broadcasted_iota(jnp.int32, (g * tq, 1), 0)
    snk = jnp.where(row < tq, sink_ref[kv * g], sink_ref[kv * g + 1])
    ss, vs = [], []
    if band:
        kp, kc, kn, vp, vc, vn, kx, vx = [r[0, 0] for r in refs[2:10]]
        qi = jnp.where(row < tq, row, row - tq)
        jp = lax.broadcasted_iota(jnp.int32, (1, WINDOW), 1)
        jc = lax.broadcasted_iota(jnp.int32, (1, tq), 1)
        s_p = jnp.where((jp >= qi) & (t > 0), _dot_nt(q, kp), NEG_INF)
        s_c = jnp.where(jnp.abs(qi - jc) <= WINDOW, _dot_nt(q, kc), NEG_INF)
        s_n = jnp.where((jp <= qi - (tq - WINDOW)) & (t < nblk - 1), _dot_nt(q, kn), NEG_INF)
        ss += [s_p, s_c, s_n]
        vs += [vp, vc, vn]
    else:
        kx, vx = [r[0, 0] for r in refs[2:4]]
    ss.append(_dot_nt(q, kx))
    vs.append(vx)
    m = snk
    for s in ss:
        m = jnp.maximum(m, s.max(axis=-1, keepdims=True))
    den = jnp.exp(snk - m)
    acc = jnp.zeros((g * tq, LANE), F32)
    for s, v in zip(ss, vs):
        p = jnp.exp(s - m)
        den = den + p.sum(axis=-1, keepdims=True)
        acc = acc + _dot(p.astype(CDT), v)
    o = (acc / den).astype(CDT)
    for i in range(g):
        o_ref[0, :, i * LANE:(i + 1) * LANE] = o[i * tq:(i + 1) * tq]


def _swa(sink, q, k, v, kx, vx, band):
    b, _, l, _ = q.shape
    g = SWA_HEADS // SWA_KV_HEADS
    tq = min(256, l)
    nblk = l // tq
    r = tq // WINDOW
    nw = l // WINDOW
    in_specs = [pl.BlockSpec(memory_space=pltpu.SMEM),
                pl.BlockSpec((1, g, tq, LANE), lambda i, j, t: (i, j, t, 0))]
    args = [sink, q]
    if band:
        prev = pl.BlockSpec((1, 1, WINDOW, LANE), lambda i, j, t: (i, j, jnp.maximum(t * r - 1, 0), 0))
        cur = pl.BlockSpec((1, 1, tq, LANE), lambda i, j, t: (i, j, t, 0))
        nxt = pl.BlockSpec((1, 1, WINDOW, LANE), lambda i, j, t: (i, j, jnp.minimum((t + 1) * r, nw - 1), 0))
        in_specs += [prev, cur, nxt, prev, cur, nxt]
        args += [k, k, k, v, v, v]
    nx = kx.shape[2]
    in_specs += [pl.BlockSpec((1, 1, nx, LANE), lambda i, j, t: (i, j, 0, 0))] * 2
    args += [kx, vx]
    return pl.pallas_call(
        functools.partial(_swa_kernel, band=band, tq=tq, nblk=nblk),
        out_shape=jax.ShapeDtypeStruct((b, l, SWA_HEADS * LANE), CDT),
        grid=(b, SWA_KV_HEADS, nblk),
        in_specs=in_specs,
        out_specs=pl.BlockSpec((1, tq, g * LANE), lambda i, j, t: (i, t, j)),
        compiler_params=_cparams("parallel", "parallel", "parallel"),
        name="swa_attention",
    )(*args)


def _shortconv_kernel(x_ref, w_ref, b_ref, o_ref):
    x = x_ref[0].astype(F32)
    l = x.shape[0]
    t = lax.broadcasted_iota(jnp.int32, (l, 1), 0)
    prev = jnp.where(t == 0, 0.0, pltpu.roll(x, 1, 0))
    nxt = jnp.where(t == l - 1, 0.0, pltpu.roll(x, l - 1, 0))
    y = b_ref[...] + prev * w_ref[0:1, :] + x * w_ref[1:2, :] + nxt * w_ref[2:3, :]
    o_ref[0] = y.T.astype(CDT)


def _shortconv_t(x, w, bias):
    b, l, c = x.shape
    tc = 256
    return pl.pallas_call(
        _shortconv_kernel,
        out_shape=jax.ShapeDtypeStruct((b, c, l), CDT),
        grid=(b, c // tc),
        in_specs=[pl.BlockSpec((1, l, tc), lambda i, j: (i, 0, j)),
                  pl.BlockSpec((SHORT_K, tc), lambda i, j: (0, j)),
                  pl.BlockSpec((1, tc), lambda i, j: (0, j))],
        out_specs=pl.BlockSpec((1, tc, l), lambda i, j: (i, j, 0)),
        compiler_params=_cparams("parallel", "parallel"),
        name="hyena_shortconv",
    )(x, w, bias)


def _swap(x):
    return jnp.concatenate([x[..., LANE:], x[..., :LANE]], axis=-1)


def _comb(pq, n):
    p, q = pq[:, :n], pq[:, n:]
    return jnp.concatenate([p[..., :LANE] - q[..., LANE:], p[..., LANE:] + q[..., :LANE]], axis=-1)


def _hyena_kernel(v_ref, g1_ref, g2_ref, f1_ref, c1_ref, tw_ref, f2_ref, f2c_ref, ka_ref, kb_ref, bias_ref,
                  o_ref, *, ct, na, kin):
    def load(ref):
        return jnp.concatenate([ref[0, 0], ref[0, 1]], axis=-1).astype(F32)

    z = load(v_ref)
    gates = (g1_ref, g2_ref)
    twa, twb = tw_ref[0], tw_ref[1]
    for o in range(HY_ORDER):
        a = _comb(_bdot(f1_ref[...], z.astype(CDT)), na)
        a = a * twa + _swap(a) * twb
        x = _dot(a.reshape(ct * na, 2 * LANE).astype(CDT), f2_ref[...])
        y = x * ka_ref[o].reshape(ct * na, 2 * LANE) + _swap(x) * kb_ref[o].reshape(ct * na, 2 * LANE)
        bq = _dot(y.astype(CDT), f2c_ref[...]).reshape(ct, na, 2 * LANE)
        bq = bq * twa - _swap(bq) * twb
        yt = _comb(_bdot(c1_ref[...], bq.astype(CDT)), kin)
        z = load(gates[o]) * (yt + bias_ref[o] * z)
    o_ref[0, 0] = z[..., :LANE].astype(CDT)
    o_ref[0, 1] = z[..., LANE:].astype(CDT)


def _hyena_tables(na, kin, ct):
    f1, tw, f2, f2c, c1 = _hyena_dft_np(na, kin)
    bc = lambda m: jnp.broadcast_to(jnp.asarray(m, F32).astype(CDT)[None], (ct,) + m.shape)
    return (bc(f1), bc(c1), jnp.asarray(tw, F32), jnp.asarray(f2, F32).astype(CDT),
            jnp.asarray(f2c, F32).astype(CDT))


def _hyena_dft_np(na, kin):
    n = na * LANE
    ka = np.arange(na)[:, None]
    a = np.arange(kin)[None, :]
    ang1 = -2.0 * np.pi * ((ka * a) % na) / na
    f1 = np.concatenate([np.cos(ang1), np.sin(ang1)], axis=0)
    c1 = np.concatenate([np.cos(ang1).T, -np.sin(ang1).T], axis=0)
    bb = np.arange(LANE)[None, :]
    angt = -2.0 * np.pi * ((ka * bb) % n) / n
    tr, ti = np.cos(angt), np.sin(angt)
    tw = np.stack([np.concatenate([tr, tr], 1), np.concatenate([-ti, ti], 1)])
    b2 = np.arange(LANE)
    ang2 = -2.0 * np.pi * ((b2[:, None] * b2[None, :]) % LANE) / LANE
    fr, fi = np.cos(ang2), np.sin(ang2)
    f2 = np.block([[fr, fi], [-fi, fr]])
    f2c = np.block([[fr, -fi], [fi, fr]])
    return f1, tw, f2, f2c, c1


def _hyena_filter_taps(l, hp):
    t = jnp.linspace(0.0, 1.0, l, dtype=F32)[None, :]
    bands = (HY_EMB - 1) // 2
    w = 2.0 * math.pi * jnp.arange(l, dtype=F32) / l
    fr = jnp.linspace(1e-4, bands - 1, bands, dtype=F32)
    ang = fr[:, None] * w[None, :]
    z = jnp.concatenate([t, jnp.cos(ang), -jnp.sin(ang)], axis=0)
    freq = hp['hy_f_freq']
    hi = lax.Precision.HIGHEST
    h = jnp.sin(freq[0][:, None] * (jnp.dot(hp['hy_f_w1'].T, z, precision=hi) + hp['hy_f_b1'][:, None]))
    h = jnp.sin(freq[1][:, None] * (jnp.dot(hp['hy_f_w2'].T, h, precision=hi) + hp['hy_f_b2'][:, None]))
    h = jnp.dot(hp['hy_f_w3'].T, h, precision=hi).reshape(HY_ORDER, 2, HY_W, l)
    deltas = jnp.abs(jnp.linspace(math.log(HY_DECAY_TARGET) / HY_SLOW_PCT,
                                  math.log(HY_DECAY_TARGET) / HY_FAST_PCT, HY_W, dtype=F32))
    h = h * jnp.exp(-t * deltas[:, None])
    lag0 = (jnp.arange(l) > 0).astype(F32)
    return h[:, 0].reshape(HY_ORDER * HY_W, l), (h[:, 1] * lag0).reshape(HY_ORDER * HY_W, l)


def _split(x):
    hi = x.astype(CDT)
    return hi, (x - hi.astype(F32)).astype(CDT)


def _hyfilt_kernel(hf_ref, hb_ref, f1h_ref, f1l_ref, tw_ref, f2h_ref, f2l_ref, ka_ref, kb_ref, *, rt, na, inv_n):
    xh, xl = _split(jnp.concatenate([hf_ref[...], hb_ref[...]], axis=-1))
    f1h, f1l = f1h_ref[...], f1l_ref[...]
    pq = _bdot(f1h, xh) + _bdot(f1l, xh) + _bdot(f1h, xl)
    p, q = pq[:, :na], pq[:, na:]
    a = jnp.concatenate([jnp.concatenate([p[..., :LANE], q[..., :LANE]], -1),
                         jnp.concatenate([p[..., LANE:], q[..., LANE:]], -1)], axis=0)
    a = a * tw_ref[0] + _swap(a) * tw_ref[1]
    ah, al = _split(a.reshape(2 * rt * na, 2 * LANE))
    x = (_dot(ah, f2h_ref[...]) + _dot(al, f2h_ref[...]) + _dot(ah, f2l_ref[...])).reshape(2 * rt, na, 2 * LANE)
    xf, xb = x[:rt], x[rt:]
    kr = (xf[..., :LANE] + xb[..., :LANE]) * inv_n
    ki = (xf[..., LANE:] - xb[..., LANE:]) * inv_n
    ka_ref[...] = jnp.concatenate([kr, kr], -1)
    kb_ref[...] = jnp.concatenate([-ki, ki], -1)


def _hyena_filter_spectrum(hf, hb, na, kin):
    rows, l = hf.shape
    lp = kin * LANE
    if lp != l:
        hf, hb = [jnp.pad(h, ((0, 0), (0, lp - l))) for h in (hf, hb)]
    hf, hb = hf.reshape(rows, kin, LANE), hb.reshape(rows, kin, LANE)
    rt = 16
    f1, tw, f2 = _hyena_dft_np(na, kin)[:3]
    f1 = jnp.broadcast_to(jnp.asarray(f1, F32)[None], (rt,) + f1.shape)
    f1h, f1l = _split(f1)
    f2h, f2l = _split(jnp.asarray(f2, F32))
    tw = jnp.asarray(tw, F32)
    ka, kb = pl.pallas_call(
        functools.partial(_hyfilt_kernel, rt=rt, na=na, inv_n=1.0 / (na * LANE)),
        out_shape=(jax.ShapeDtypeStruct((rows, na, 2 * LANE), F32),) * 2,
        grid=(rows // rt,),
        in_specs=[pl.BlockSpec((rt, kin, LANE), lambda i: (i, 0, 0)),
                  pl.BlockSpec((rt, kin, LANE), lambda i: (i, 0, 0)),
                  _resident(f1h.shape), _resident(f1l.shape), _resident(tw.shape),
                  _resident(f2h.shape), _resident(f2l.shape)],
        out_specs=(pl.BlockSpec((rt, na, 2 * LANE), lambda i: (i, 0, 0)),) * 2,
        compiler_params=_cparams("parallel"),
        name="hyena_filter_spectrum",
    )(hf, hb, f1h, f1l, tw, f2h, f2l)
    shape = (HY_ORDER, HY_W, na, 2 * LANE)
    return ka.reshape(shape), kb.reshape(shape)


def _hyena(hy, hp, l_true):
    b, l, _ = hy.shape
    ut = _shortconv_t(hy, hp['hy_conv_w'], hp['hy_conv_b'].reshape(1, -1))
    kin = max(l // LANE, 16)
    na = 2 * kin
    lp = kin * LANE
    if lp != l:
        ut = jnp.pad(ut, ((0, 0), (0, 0), (0, lp - l)))
    ut = ut.reshape(b // 2, 2, (HY_ORDER + 1) * HY_W, kin, LANE)
    ct = 16
    nc = HY_W // ct
    f1, c1, tw, f2, f2c = _hyena_tables(na, kin, ct)
    ka, kb = hp['ka'], hp['kb']
    bias = jnp.broadcast_to(hp['hy_bias'].reshape(HY_ORDER, HY_W, 1, 1), (HY_ORDER, HY_W, 1, 2 * LANE))
    blk = lambda off: pl.BlockSpec((1, 2, ct, kin, LANE), lambda c, p: (p, 0, c + off * nc, 0, 0))
    out = pl.pallas_call(
        functools.partial(_hyena_kernel, ct=ct, na=na, kin=kin),
        out_shape=jax.ShapeDtypeStruct((b // 2, 2, HY_W, kin, LANE), CDT),
        grid=(nc, b // 2),
        in_specs=[blk(0), blk(1), blk(2),
                  _resident(f1.shape), _resident(c1.shape), _resident(tw.shape),
                  _resident(f2.shape), _resident(f2c.shape),
                  pl.BlockSpec((HY_ORDER, ct, na, 2 * LANE), lambda c, p: (0, c, 0, 0)),
                  pl.BlockSpec((HY_ORDER, ct, na, 2 * LANE), lambda c, p: (0, c, 0, 0)),
                  pl.BlockSpec((HY_ORDER, ct, 1, 2 * LANE), lambda c, p: (0, c, 0, 0))],
        out_specs=pl.BlockSpec((1, 2, ct, kin, LANE), lambda c, p: (p, 0, c, 0, 0)),
        compiler_params=_cparams("parallel", "arbitrary"),
        name="hyena_longconv",
    )(ut, ut, ut, f1, c1, tw, f2, f2c, ka, kb, bias)
    out = out.reshape(b, HY_W, lp)[:, :, :l]
    return jnp.swapaxes(out, 1, 2)


def _s5_kernel(u_ref, psel_ref, pselt_ref, tloc_ref, wx_ref, wout_ref, d_ref, h0_ref, y_ref, hfin_ref,
               x_scr, h_scr, up_scr, *, nchunk, nb):
    w = S5_T * S5_GC
    rows = u_ref.shape[1]
    rblk = min(S5_ROW_BLOCK, rows)
    for i in range(rows // rblk):
        rs = slice(i * rblk, (i + 1) * rblk)
        up = _dot(u_ref[0, rs, :], psel_ref[0]).astype(CDT)
        up_scr[rs, :] = up
        x_scr[rs, :] = _dot(up, wx_ref[0])
    d = d_ref[0]
    dfr, dfi, dbr, dbi = d[0:1], d[1:2], d[2:3], d[3:4]

    def body(j, carry):
        hr, hi, gr, gi = carry
        rf = pl.multiple_of(j * nb, nb)
        rb = pl.multiple_of((nchunk - 1 - j) * nb, nb)
        h_scr[pl.ds(rf, nb), 0:LANE] = hr
        h_scr[pl.ds(rf, nb), LANE:2 * LANE] = hi
        h_scr[pl.ds(rb, nb), 2 * LANE:3 * LANE] = gr
        h_scr[pl.ds(rb, nb), 3 * LANE:4 * LANE] = gi
        xr = x_scr[pl.ds(rf, nb), 0:LANE]
        xi = x_scr[pl.ds(rf, nb), LANE:2 * LANE]
        yr = x_scr[pl.ds(rb, nb), 2 * LANE:3 * LANE]
        yi = x_scr[pl.ds(rb, nb), 3 * LANE:4 * LANE]
        return (dfr * hr - dfi * hi + xr, dfr * hi + dfi * hr + xi,
                dbr * gr - dbi * gi + yr, dbr * gi + dbi * gr + yi)

    fin = lax.fori_loop(0, nchunk, body, tuple(h0_ref[0, k] for k in range(4)))
    for k in range(4):
        hfin_ref[0, k] = fin[k]
    r = pl.program_id(1)
    for i in range(rows // rblk):
        rs = slice(i * rblk, (i + 1) * rblk)
        hs = h_scr[rs, :].astype(CDT)
        y0 = _dot(up_scr[rs, :w], tloc_ref[0]) + _dot(hs, wout_ref[0, :, :w])
        y1 = _dot(up_scr[rs, w:], tloc_ref[1]) + _dot(hs, wout_ref[0, :, w:])
        contrib = _dot(jnp.concatenate([y0, y1], axis=-1).astype(CDT), pselt_ref[0]).astype(CDT)

        @pl.when(r == 0)
        def _():
            y_ref[0, rs, :] = contrib

        @pl.when(r > 0)
        def _():
            y_ref[0, rs, :] = y_ref[0, rs, :] + contrib


S5_HALF_GROUPS = LANE // S5_GC
S5_HALF_PAIRS = S5_HALF_GROUPS // 2


def _s5_select():
    ri = jnp.arange(S5_T * LANE)
    r_sig, r_grp, r_ch = ri // LANE, (ri % LANE) // S5_GC, ri % S5_GC
    ci = jnp.arange(2 * S5_T * S5_GC)
    c_grp, c_sig, c_ch = ci // (S5_T * S5_GC), (ci % (S5_T * S5_GC)) // S5_GC, ci % S5_GC
    same = (r_sig[:, None] == c_sig[None, :]) & (r_ch[:, None] == c_ch[None, :])
    sel = jnp.stack([(same & (r_grp[:, None] == 2 * q + c_grp[None, :])) for q in range(S5_HALF_PAIRS)])
    sel = sel.astype(CDT)
    return sel, jnp.swapaxes(sel, 1, 2)


def _s5(u, ops, h0):
    b, l, _ = u.shape
    nchunk = l // S5_T
    w = S5_T * S5_GC
    rows = nchunk * b
    nhalf = S5_W // LANE
    wide = S5_T * LANE
    uh = u.reshape(b, nchunk, S5_T, nhalf, LANE).transpose(3, 1, 0, 2, 4).reshape(nhalf, rows, wide)
    psel, pselt = _s5_select()
    pair = lambda h, r: h * S5_HALF_PAIRS + r
    y, hfin = pl.pallas_call(
        functools.partial(_s5_kernel, nchunk=nchunk, nb=b),
        out_shape=(jax.ShapeDtypeStruct((nhalf, rows, wide), CDT),
                   jax.ShapeDtypeStruct((S5_GROUPS // 2, 4, b, LANE), F32)),
        grid=(nhalf, S5_HALF_PAIRS),
        in_specs=[pl.BlockSpec((1, rows, wide), lambda h, r: (h, 0, 0), pipeline_mode=pl.Buffered(1)),
                  pl.BlockSpec((1, wide, 2 * w), lambda h, r: (r, 0, 0)),
                  pl.BlockSpec((1, 2 * w, wide), lambda h, r: (r, 0, 0)),
                  pl.BlockSpec((2, w, w), lambda h, r: (pair(h, r), 0, 0)),
                  pl.BlockSpec((1, 2 * w, 2 * w), lambda h, r: (pair(h, r), 0, 0)),
                  pl.BlockSpec((1, 2 * w, 2 * w), lambda h, r: (pair(h, r), 0, 0)),
                  pl.BlockSpec((1, 4, LANE), lambda h, r: (pair(h, r), 0, 0)),
                  pl.BlockSpec((1, 4, b, LANE), lambda h, r: (pair(h, r), 0, 0, 0))],
        out_specs=(pl.BlockSpec((1, rows, wide), lambda h, r: (h, 0, 0)),
                   pl.BlockSpec((1, 4, b, LANE), lambda h, r: (pair(h, r), 0, 0, 0))),
        scratch_shapes=[pltpu.VMEM((rows, 2 * w), F32), pltpu.VMEM((rows, 2 * w), F32),
                        pltpu.VMEM((rows, 2 * w), CDT)],
        compiler_params=_cparams("parallel", "arbitrary"),
        name="s5_chunked",
    )(uh, psel, pselt, ops['tloc'], ops['wx'], ops['wout'], ops['d16'], h0)
    y = y.reshape(nhalf, nchunk, b, S5_T, LANE).transpose(2, 1, 3, 0, 4).reshape(b, l, S5_W)
    return y, hfin


def _s5_operators(p):
    g, pp, gc, t = S5_GROUPS, S5_P, S5_GC, S5_T
    npair = g // 2
    n = jnp.arange(t + 1, dtype=F32)[:, None, None]
    tops, wxs, wouts, d16 = [], [], [], []
    sig = jnp.arange(t)
    for d in range(2):
        a_re, a_im = p['s5_a_re'][d], p['s5_a_im'][d]
        dt = jnp.exp(p['s5_log_dt'][d])[:, None]
        mag1 = jnp.exp(dt * a_re)
        ab_re, ab_im = mag1 * jnp.cos(dt * a_im), mag1 * jnp.sin(dt * a_im)
        den = a_re * a_re + a_im * a_im
        f_re = ((ab_re - 1.0) * a_re + ab_im * a_im) / den
        f_im = (ab_im * a_re - (ab_re - 1.0) * a_im) / den
        mag = jnp.exp(n * (dt * a_re)[None])
        pr, pi = mag * jnp.cos(n * (dt * a_im)[None]), mag * jnp.sin(n * (dt * a_im)[None])
        b_re, b_im = p['s5_b_re'][d], p['s5_b_im'][d]
        bt_re = f_re[..., None] * b_re - f_im[..., None] * b_im
        bt_im = f_re[..., None] * b_im + f_im[..., None] * b_re
        c_re, c_im = p['s5_c_re'][d], p['s5_c_im'][d]
        ca_re = c_re[None] * pr[:, :, None, :] - c_im[None] * pi[:, :, None, :]
        ca_im = c_re[None] * pi[:, :, None, :] + c_im[None] * pr[:, :, None, :]
        hi = lax.Precision.HIGHEST
        kk = (jnp.einsum('ngcp,gpk->ngck', ca_re[:t], bt_re, precision=hi)
              - jnp.einsum('ngcp,gpk->ngck', ca_im[:t], bt_im, precision=hi))
        lag = (sig[None, :] - sig[:, None]) if d == 0 else (sig[:, None] - sig[None, :])
        top = jnp.where((lag >= 0)[:, :, None, None, None], kk[jnp.clip(lag, 0, t - 1)], 0.0)
        tops.append(top.transpose(2, 0, 4, 1, 3).reshape(g, t * gc, t * gc))
        pw = (t - 1 - sig) if d == 0 else sig
        xr = pr[pw][..., None] * bt_re[None] - pi[pw][..., None] * bt_im[None]
        xi = pr[pw][..., None] * bt_im[None] + pi[pw][..., None] * bt_re[None]
        wxs.append((xr.transpose(1, 0, 3, 2).reshape(g, t * gc, pp), xi.transpose(1, 0, 3, 2).reshape(g, t * gc, pp)))
        po = (sig + 1) if d == 0 else (t - sig)
        wouts.append((ca_re[po].transpose(1, 3, 0, 2).reshape(g, pp, t * gc),
                      -ca_im[po].transpose(1, 3, 0, 2).reshape(g, pp, t * gc)))
        d16.append((pr[t], pi[t]))
    eye = jnp.eye(t * gc, dtype=F32).reshape(t, gc, t, gc)
    skip = (eye[None] * p['s5_d'].reshape(g, 1, 1, 1, gc)).reshape(g, t * gc, t * gc)
    tloc = tops[0] + tops[1] + skip

    def pair_cols(m):
        m = m.reshape(npair, 2, m.shape[1], m.shape[2])
        z = jnp.zeros_like(m[:, 0])
        return jnp.concatenate([jnp.concatenate([m[:, 0], z], -1), jnp.concatenate([z, m[:, 1]], -1)], -2)

    wx = jnp.concatenate([pair_cols(wxs[0][0]), pair_cols(wxs[0][1]), pair_cols(wxs[1][0]), pair_cols(wxs[1][1])],
                         axis=-1)
    wout = jnp.concatenate([pair_cols(wouts[0][0]), pair_cols(wouts[0][1]), pair_cols(wouts[1][0]),
                            pair_cols(wouts[1][1])], axis=-2)
    dd = jnp.stack([d16[0][0], d16[0][1], d16[1][0], d16[1][1]], axis=0)
    dd = dd.reshape(4, npair, 2 * pp).transpose(1, 0, 2)
    return dict(tloc=tloc.astype(CDT), wx=wx.astype(CDT), wout=wout.astype(CDT), d16=dd)


def _merge_kernel(x_ref, mod_ref, gpre_ref, gpost_ref, omla_ref, ohy_ref, oswa_ref, ys5_ref,
                  wgate_ref, bgate_ref, wmla_ref, why_ref, wswa_ref, ws5_ref, gluw_ref, glub_ref, wout_ref, o_ref):
    x = x_ref[0]
    mod = mod_ref[0]
    u = _pre_mod(x, gpre_ref[...], mod, 1).astype(CDT)
    y = ys5_ref[0].astype(F32)
    g = 0.5 * y * (1.0 + jnp.tanh(math.sqrt(2.0 / math.pi) * (y + 0.044715 * (y * y * y))))
    o_s5 = (g * _sigmoid(_dot(g.astype(CDT), gluw_ref[...]) + glub_ref[...])).astype(CDT)
    outs = (omla_ref[0], ohy_ref[0], oswa_ref[0], o_s5)
    wbr = (wmla_ref, why_ref, wswa_ref, ws5_ref)
    m = jnp.zeros(x.shape, F32)
    for i in range(4):
        gate = _sigmoid(_dot(u, wgate_ref[:, i * D_MODEL:(i + 1) * D_MODEL]) + bgate_ref[:, i * D_MODEL:(i + 1) * D_MODEL])
        m = m + gate * _dot(outs[i], wbr[i][...])
    f = _dot(m.astype(CDT), wout_ref[...])
    o_ref[0] = x + mod[5:6, :] * _rms(f, gpost_ref[...])


def _merge(x, mod, mod_row, gpre, gpost, o_mla, o_hy, o_swa, y_s5, wp):
    b, l, d = x.shape
    tm = min(512, l)
    row_spec = lambda n: pl.BlockSpec((1, tm, n), lambda i, j: (i, j, 0))
    names = ('w_gate', 'b_gate', 'w_br_mla', 'w_br_hy', 'w_br_swa', 'w_br_s5', 'glu_w', 'glu_b', 'w_out')
    return pl.pallas_call(
        _merge_kernel,
        out_shape=jax.ShapeDtypeStruct(x.shape, F32),
        grid=(b, l // tm),
        in_specs=[row_spec(d), pl.BlockSpec((1, N_MOD, d), lambda i, j: (mod_row(i), 0, 0)),
                  _resident((1, d)), _resident((1, d)),
                  row_spec(o_mla.shape[-1]), row_spec(o_hy.shape[-1]), row_spec(o_swa.shape[-1]),
                  row_spec(y_s5.shape[-1])] + [_resident(wp[k].shape) for k in names],
        out_specs=row_spec(d),
        compiler_params=_cparams("parallel", "parallel"),
        name="merge_out",
    )(x, mod, gpre, gpost, o_mla, o_hy, o_swa, y_s5, *[wp[k] for k in names])


def _rot_partner(w, half):
    return jnp.concatenate([-w[:, half:], w[:, :half]], axis=1)


def _pad_cols(w, n):
    return jnp.pad(w, ((0, 0), (0, n - w.shape[1])))


def _pad_rows(w, n):
    return jnp.pad(w, ((0, n - w.shape[0]), (0, 0)))


def _prep_inproj(w_in, w_ukv, w_uq, g_kv, g_q):
    d = w_in.shape[0]
    zeros = lambda n: jnp.zeros((d, n), F32)
    krope = w_in[:, I_KROPE:I_KROPE + MLA_ROPE]
    kr = jnp.concatenate([zeros(MLA_NOPE), krope, zeros(LANE - MLA_NOPE - MLA_ROPE)], 1)
    krs = jnp.concatenate([zeros(MLA_NOPE), _rot_partner(krope, MLA_ROPE // 2), zeros(LANE - MLA_NOPE - MLA_ROPE)], 1)
    ckv = w_in[:, I_CKV:I_CKV + MLA_KV_LORA]
    cq = _pad_cols(w_in[:, I_CQ:I_CQ + MLA_Q_LORA], 2 * LANE)

    def heads(w, nh, partner):
        cols = []
        for h in range(nh):
            wh = w[:, h * SWA_HD:(h + 1) * SWA_HD]
            if partner:
                wh = _rot_partner(wh, SWA_HD // 2)
            cols.append(_pad_cols(wh, LANE))
        return jnp.concatenate(cols, 1)

    swq = w_in[:, I_SWQ:I_SWQ + SWA_HEADS * SWA_HD]
    swk = w_in[:, I_SWK:I_SWK + SWA_KV_HEADS * SWA_HD]
    swv = w_in[:, I_SWV:I_SWV + SWA_KV_HEADS * SWA_HD]
    w_big = jnp.concatenate([kr, krs, ckv, cq,
                             heads(swq, SWA_HEADS, False), heads(swq, SWA_HEADS, True),
                             heads(swk, SWA_KV_HEADS, False), heads(swk, SWA_KV_HEADS, True),
                             heads(swv, SWA_KV_HEADS, False),
                             w_in[:, I_S5:I_S5 + S5_W], w_in[:, I_HY:I_HY + (HY_ORDER + 1) * HY_W]], axis=1)
    assert w_big.shape[1] == N_BIG
    kvw = w_ukv.reshape(MLA_KV_LORA, MLA_HEADS, MLA_NOPE + MLA_V)
    kslots = [_pad_cols(kvw[:, h, :MLA_NOPE], LANE) for h in range(MLA_HEADS)]
    vslots = [_pad_cols(kvw[:, h, MLA_NOPE:], LANE) for h in range(MLA_HEADS)]
    w_ukv_p = jnp.concatenate(kslots + vslots, axis=1)
    qw = w_uq.reshape(MLA_Q_LORA, MLA_HEADS, MLA_NOPE + MLA_ROPE)
    qslots = [_pad_cols(qw[:, h], LANE) for h in range(MLA_HEADS)]
    pslots = [_pad_cols(jnp.concatenate([jnp.zeros((MLA_Q_LORA, MLA_NOPE), F32),
                                         _rot_partner(qw[:, h, MLA_NOPE:], MLA_ROPE // 2)], 1), LANE)
              for h in range(MLA_HEADS)]
    w_uq_p = _pad_rows(jnp.concatenate(qslots + pslots, axis=1), 2 * LANE)
    return dict(w_big=w_big.astype(CDT), w_ukv=w_ukv_p.astype(CDT), w_uq=w_uq_p.astype(CDT),
                g_kv=g_kv.reshape(1, -1), g_q=_pad_cols(g_q.reshape(1, -1), 2 * LANE))


def _pad_head_rows(w, nh, hd):
    w = w.reshape(nh, hd, w.shape[-1])
    return jnp.pad(w, ((0, 0), (0, LANE - hd), (0, 0))).reshape(nh * LANE, -1)


def _rope_tables(n_tokens, use_rope):
    ones = jnp.ones((n_tokens, LANE), F32)
    zeros = jnp.zeros((n_tokens, LANE), F32)
    if not use_rope:
        return ones, zeros, ones, zeros

    def axial(rot_dim):
        rows = n_tokens // GRID_W
        r = jnp.repeat(jnp.arange(rows, dtype=F32), GRID_W)
        col = jnp.tile(jnp.arange(GRID_W, dtype=F32), rows)
        n_freq = rot_dim // 4
        freqs = ROPE_THETA ** (-jnp.arange(n_freq, dtype=F32) / n_freq)
        ang = jnp.concatenate([r[:, None] * freqs, col[:, None] * freqs], axis=-1)
        return jnp.cos(ang), jnp.sin(ang)

    c, s = axial(MLA_ROPE)
    pad = LANE - MLA_NOPE - MLA_ROPE
    cm = jnp.concatenate([ones[:, :MLA_NOPE], c, c, ones[:, :pad]], 1)
    sm = jnp.concatenate([zeros[:, :MLA_NOPE], s, s, zeros[:, :pad]], 1)
    c, s = axial(SWA_HD)
    cw = jnp.concatenate([c, c, ones[:, :LANE - SWA_HD]], 1)
    sw = jnp.concatenate([s, s, zeros[:, :LANE - SWA_HD]], 1)
    return cm, sm, cw, sw


def _hyena_params(p, l):
    kin = max(l // LANE, 16)
    na = 2 * kin
    hf, hb = _hyena_filter_taps(l, p)
    ka, kb = _hyena_filter_spectrum(hf, hb, na, kin)
    return dict(hy_conv_w=p['hy_conv_w'], hy_conv_b=p['hy_conv_b'], hy_bias=p['hy_bias'], ka=ka, kb=kb)


def kernel(x, c, ctx, c_ctx, w_ada, b_ada, norm_pre, norm_post, ffn1_up, ffn1_down, ffn2_up, ffn2_down,
           w_in, mla_q_norm, mla_kv_norm, mla_w_uq, mla_w_ukv, hy_conv_w, hy_conv_b, hy_f_w1, hy_f_b1,
           hy_f_freq, hy_f_w2, hy_f_b2, hy_f_w3, hy_bias, swa_sink, s5_a_re, s5_a_im, s5_log_dt,
           s5_b_re, s5_b_im, s5_c_re, s5_c_im, s5_d, s5_glu_w, s5_glu_b, w_gate, b_gate,
           w_br_mla, w_br_hy, w_br_swa, w_br_s5, w_out):
    stacked = dict(w_ada=w_ada, b_ada=b_ada, norm_pre=norm_pre, norm_post=norm_post,
                   ffn1_up=ffn1_up, ffn1_down=ffn1_down, ffn2_up=ffn2_up, ffn2_down=ffn2_down,
                   w_in=w_in, mla_q_norm=mla_q_norm, mla_kv_norm=mla_kv_norm, mla_w_uq=mla_w_uq,
                   mla_w_ukv=mla_w_ukv, hy_conv_w=hy_conv_w, hy_conv_b=hy_conv_b, hy_f_w1=hy_f_w1,
                   hy_f_b1=hy_f_b1, hy_f_freq=hy_f_freq, hy_f_w2=hy_f_w2, hy_f_b2=hy_f_b2, hy_f_w3=hy_f_w3,
                   hy_bias=hy_bias, swa_sink=swa_sink, s5_a_re=s5_a_re, s5_a_im=s5_a_im,
                   s5_log_dt=s5_log_dt, s5_b_re=s5_b_re, s5_b_im=s5_b_im, s5_c_re=s5_c_re,
                   s5_c_im=s5_c_im, s5_d=s5_d, s5_glu_w=s5_glu_w, s5_glu_b=s5_glu_b,
                   w_gate=w_gate, b_gate=b_gate, w_br_mla=w_br_mla, w_br_hy=w_br_hy,
                   w_br_swa=w_br_swa, w_br_s5=w_br_s5, w_out=w_out)
    depth = w_ada.shape[0]
    nb, seq, d = x.shape
    nctx = ctx.shape[1]
    assert nb % 2 == 0 and seq % 256 == 0 and nctx % 256 == 0

    rows = -(-(nb + 1) // 8) * 8
    cvec = jnp.zeros((rows, d), F32).at[:nb].set(c).at[nb].set(c_ctx)
    mods = _modulation(cvec, w_ada, b_ada)
    lat_row = lambda i: i
    ctx_row = lambda i: nb

    rope_l = _rope_tables(seq, True)
    rope_c = _rope_tables(nctx, False)
    h0_zero = jnp.zeros((S5_GROUPS // 2, 4, nb, LANE), F32)

    xl, xc = x, ctx
    for l in range(depth):
        p = {name: arr[l] for name, arr in stacked.items()}
        mod = mods[l]
        ctx_out = l < depth - 1
        gpre = [p['norm_pre'][i].reshape(1, d) for i in range(N_SUB)]
        gpost = [p['norm_post'][i].reshape(1, d) for i in range(N_SUB)]
        f1u, f1d = p['ffn1_up'].astype(CDT), p['ffn1_down'].astype(CDT)
        f2u, f2d = p['ffn2_up'].astype(CDT), p['ffn2_down'].astype(CDT)
        wp_in = _prep_inproj(p['w_in'], p['mla_w_ukv'], p['mla_w_uq'], p['mla_kv_norm'], p['mla_q_norm'])
        wp_mg = dict(
            w_gate=jnp.concatenate([p['w_gate'][i] for i in range(4)], axis=1).astype(CDT),
            b_gate=p['b_gate'].reshape(1, -1),
            w_br_mla=_pad_head_rows(p['w_br_mla'], MLA_HEADS, MLA_V).astype(CDT),
            w_br_hy=p['w_br_hy'].astype(CDT),
            w_br_swa=_pad_head_rows(p['w_br_swa'], SWA_HEADS, SWA_HD).astype(CDT),
            w_br_s5=p['w_br_s5'].astype(CDT),
            glu_w=p['s5_glu_w'].astype(CDT), glu_b=p['s5_glu_b'].reshape(1, -1),
            w_out=p['w_out'].astype(CDT))
        s5_ops = _s5_operators(p)

        xl = _ffn(xl, mod, lat_row, gpre[0], gpost[0], f1u, f1d, 0)
        xc = _ffn(xc, mod, ctx_row, gpre[0], gpost[0], f1u, f1d, 0)

        qm_c, km_c, vm_c, qw_c, kw_c, vw_c, s5u_c, hy_c = _inproj(xc, mod, ctx_row, gpre[1], wp_in, rope_c)
        qm_l, km_l, vm_l, qw_l, kw_l, vw_l, s5u_l, hy_l = _inproj(xl, mod, lat_row, gpre[1], wp_in, rope_l)

        ys5_c, h_ctx = _s5(s5u_c, s5_ops, h0_zero)
        ys5_l, _ = _s5(s5u_l, s5_ops, h_ctx)
        o_mla = _mla(qm_l, [(km_l, vm_l), (km_c, vm_c)])
        o_swa = _swa(p['swa_sink'], qw_l, kw_l, vw_l, kw_c, vw_c, True)
        o_hy = _hyena(hy_l, _hyena_params(p, seq), seq)
        xl = _merge(xl, mod, lat_row, gpre[1], gpost[1], o_mla, o_hy, o_swa, ys5_l, wp_mg)
        xl = _ffn(xl, mod, lat_row, gpre[2], gpost[2], f2u, f2d, 2)
        if ctx_out:
            o_mla_c = _mla(qm_c, [(km_c, vm_c)])
            o_swa_c = _swa(p['swa_sink'], qw_c, kw_c, vw_c, kw_c, vw_c, False)
            o_hy_c = _hyena(hy_c, _hyena_params(p, nctx), nctx)
            xc = _merge(xc, mod, ctx_row, gpre[1], gpost[1], o_mla_c, o_hy_c, o_swa_c, ys5_c, wp_mg)
            xc = _ffn(xc, mod, ctx_row, gpre[2], gpost[2], f2u, f2d, 2)
    return xl
```

```python
import functools
import math

import numpy as np
import jax
import jax.numpy as jnp
from jax import lax
from jax.experimental import pallas as pl
from jax.experimental.pallas import tpu as pltpu

F32 = jnp.float32
CDT = jnp.bfloat16

D_MODEL = 1024
D_FF = 2816
N_SUB = 3
N_MOD = 3 * N_SUB
MACARON_W = 0.5
ROPE_THETA = 10000.0
GRID_W = 64
EPS = 1e-6
NEG_INF = -1e30

MLA_HEADS = 4
MLA_NOPE = 64
MLA_ROPE = 32
MLA_V = 64
MLA_Q_LORA = 192
MLA_KV_LORA = 128
MLA_SCALE = (MLA_NOPE + MLA_ROPE) ** -0.5
LOG2E = math.log2(math.e)

HY_W = 256
HY_ORDER = 2
HY_EMB = 33
HY_DECAY_TARGET = 1e-2
HY_FAST_PCT = 0.3
HY_SLOW_PCT = 1.5
SHORT_K = 3

SWA_HEADS = 4
SWA_KV_HEADS = 2
SWA_HD = 64
WINDOW = 128
SWA_SCALE = SWA_HD ** -0.5

S5_W = 256
S5_GC = 16
S5_GROUPS = S5_W // S5_GC
S5_P = 64
S5_ROW_BLOCK = 512
S5_T = 16

LANE = 128
MXU = 256
VMEM_LIMIT = 56 * 1024 * 1024

_IN_SIZES = (MLA_KV_LORA, MLA_ROPE, SWA_KV_HEADS * SWA_HD, SWA_KV_HEADS * SWA_HD, S5_W, MLA_Q_LORA,
             SWA_HEADS * SWA_HD, (HY_ORDER + 1) * HY_W)
_IN_OFF = np.concatenate([[0], np.cumsum(_IN_SIZES)])
(I_CKV, I_KROPE, I_SWK, I_SWV, I_S5, I_CQ, I_SWQ, I_HY) = [int(v) for v in _IN_OFF[:-1]]

O_KR, O_KRS, O_CKV, O_CQ = 0, 128, 256, 384
O_SQ, O_SQS = 640, 1152
O_SK, O_SKS, O_SV = 1664, 1920, 2176
O_S5, O_HY = 2432, 2688
N_BIG = O_HY + (HY_ORDER + 1) * HY_W


def _cparams(*sem):
    return pltpu.CompilerParams(dimension_semantics=sem, vmem_limit_bytes=VMEM_LIMIT)


def _resident(shape):
    nd = len(shape)
    return pl.BlockSpec(shape, lambda *_: (0,) * nd, pipeline_mode=pl.Buffered(1))


def _dot(a, b):
    return jnp.dot(a, b, preferred_element_type=F32)


def _dot_nt(a, b):
    return lax.dot_general(a, b, (((1,), (1,)), ((), ())), preferred_element_type=F32)


def _bdot(a, b):
    return lax.dot_general(a, b, (((2,), (1,)), ((0,), (0,))), preferred_element_type=F32)


def _rms(x, g):
    return x * lax.rsqrt(jnp.mean(x * x, axis=-1, keepdims=True) + EPS) * g


def _sigmoid(x):
    return 1.0 / (1.0 + jnp.exp(-x))


def _pre_mod(x, gpre, mod, sub):
    return _rms(x, gpre) * (1.0 + mod[3 * sub + 1:3 * sub + 2, :]) + mod[3 * sub:3 * sub + 1, :]


def _mod_kernel(c_ref, w_ref, b_ref, o_ref):
    c = c_ref[...]
    s = c * _sigmoid(c)
    w = w_ref[0]
    s_hi = s.astype(CDT)
    s_lo = (s - s_hi.astype(F32)).astype(CDT)
    w_hi = w.astype(CDT)
    w_lo = (w - w_hi.astype(F32)).astype(CDT)
    o_ref[0] = _dot(s_hi, w_hi) + _dot(s_hi, w_lo) + _dot(s_lo, w_hi) + b_ref[0]


def _modulation(cvec, w_ada, b_ada):
    depth, d, n = w_ada.shape
    rows = cvec.shape[0]
    tn = n // 8
    out = pl.pallas_call(
        _mod_kernel,
        out_shape=jax.ShapeDtypeStruct((depth, rows, n), F32),
        grid=(depth, n // tn),
        in_specs=[pl.BlockSpec((rows, d), lambda l, j: (0, 0)),
                  pl.BlockSpec((1, d, tn), lambda l, j: (l, 0, j)),
                  pl.BlockSpec((1, 1, tn), lambda l, j: (l, 0, j))],
        out_specs=pl.BlockSpec((1, rows, tn), lambda l, j: (l, 0, j)),
        compiler_params=_cparams("arbitrary", "arbitrary"),
        name="modulation",
    )(cvec, w_ada, b_ada.reshape(depth, 1, n))
    return out.reshape(depth, rows, N_MOD, D_MODEL)


FFN_CHUNK = 256


def _ffn_kernel(x_ref, mod_ref, gpre_ref, gpost_ref, wup_ref, wdn_ref, o_ref, *, sub):
    x = x_ref[0]
    mod = mod_ref[0]
    u = _pre_mod(x, gpre_ref[...], mod, sub).astype(CDT)
    acc = jnp.zeros(x.shape, F32)
    for c in range(D_FF // FFN_CHUNK):
        lo = c * FFN_CHUNK
        a = _dot(u, wup_ref[:, lo:lo + FFN_CHUNK])
        b = _dot(u, wup_ref[:, D_FF + lo:D_FF + lo + FFN_CHUNK])
        h = (a * _sigmoid(a) * b).astype(CDT)
        acc = acc + _dot(h, wdn_ref[lo:lo + FFN_CHUNK, :])
    gate = mod[3 * sub + 2:3 * sub + 3, :]
    o_ref[0] = x + MACARON_W * gate * _rms(acc, gpost_ref[...])


def _ffn(x, mod, mod_row, gpre, gpost, wup, wdn, sub):
    b, l, d = x.shape
    tm = min(512, l)
    return pl.pallas_call(
        functools.partial(_ffn_kernel, sub=sub),
        out_shape=jax.ShapeDtypeStruct(x.shape, F32),
        grid=(b, l // tm),
        in_specs=[pl.BlockSpec((1, tm, d), lambda i, j: (i, j, 0)),
                  pl.BlockSpec((1, N_MOD, d), lambda i, j: (mod_row(i), 0, 0)),
                  _resident((1, d)), _resident((1, d)),
                  _resident(wup.shape), _resident(wdn.shape)],
        out_specs=pl.BlockSpec((1, tm, d), lambda i, j: (i, j, 0)),
        compiler_params=_cparams("parallel", "parallel"),
        name="ffn_sublayer",
    )(x, mod, gpre, gpost, wup, wdn)


def _inproj_kernel(x_ref, mod_ref, gpre_ref, wbig_ref, gkv_ref, gq_ref, wukv_ref, wuq_ref,
                   cm_ref, sm_ref, cw_ref, sw_ref,
                   qm_ref, km_ref, vm_ref, qw_ref, kw_ref, vw_ref, s5_ref, hy_ref):
    u = _pre_mod(x_ref[0], gpre_ref[...], mod_ref[0], 1).astype(CDT)

    def seg(off, n):
        return _dot(u, wbig_ref[:, off:off + n])

    cm, sm = cm_ref[...], sm_ref[...]
    cw, sw = cw_ref[...], sw_ref[...]
    kr = seg(O_KR, LANE) * cm + seg(O_KRS, LANE) * sm
    ckv = seg(O_CKV, LANE)
    kvn = _rms(ckv, gkv_ref[...]).astype(CDT)
    kv = _dot(kvn, wukv_ref[...])
    ones_lane = (lax.broadcasted_iota(jnp.int32, (1, LANE), 1) == MLA_V).astype(F32)
    for h in range(MLA_HEADS):
        km_ref[0, h] = (kv[:, h * LANE:(h + 1) * LANE] + kr).astype(CDT)
        vm_ref[0, h] = (kv[:, (MLA_HEADS + h) * LANE:(MLA_HEADS + h + 1) * LANE] + ones_lane).T.astype(CDT)
    cq = seg(O_CQ, 2 * LANE)
    cqn = (cq * lax.rsqrt(jnp.sum(cq * cq, axis=-1, keepdims=True) * (1.0 / MLA_Q_LORA) + EPS)
           * gq_ref[...]).astype(CDT)
    qq = _dot(cqn, wuq_ref[...])
    for h in range(MLA_HEADS):
        q = qq[:, h * LANE:(h + 1) * LANE] * cm + qq[:, (MLA_HEADS + h) * LANE:(MLA_HEADS + h + 1) * LANE] * sm
        qm_ref[0, h] = (q * (MLA_SCALE * LOG2E)).T.astype(CDT)
    sq, sqs = seg(O_SQ, SWA_HEADS * LANE), seg(O_SQS, SWA_HEADS * LANE)
    for h in range(SWA_HEADS):
        q = sq[:, h * LANE:(h + 1) * LANE] * cw + sqs[:, h * LANE:(h + 1) * LANE] * sw
        qw_ref[0, h] = (q * (SWA_SCALE * LOG2E)).T.astype(CDT)
    sk, sks = seg(O_SK, SWA_KV_HEADS * LANE), seg(O_SKS, SWA_KV_HEADS * LANE)
    sv = seg(O_SV, SWA_KV_HEADS * LANE)
    for h in range(SWA_KV_HEADS):
        kw_ref[0, h] = (sk[:, h * LANE:(h + 1) * LANE] * cw + sks[:, h * LANE:(h + 1) * LANE] * sw).astype(CDT)
        vw_ref[0, h] = (sv[:, h * LANE:(h + 1) * LANE] + ones_lane).T.astype(CDT)
    s5_ref[0] = seg(O_S5, S5_W).astype(CDT)
    hy_ref[0] = seg(O_HY, (HY_ORDER + 1) * HY_W).astype(CDT)


def _inproj(x, mod, mod_row, gpre, wp, rope):
    b, l, d = x.shape
    tm = min(512, l)
    cm, sm, cw, sw = rope
    head = lambda n: jax.ShapeDtypeStruct((b, n, l, LANE), CDT)
    head_spec = lambda n: pl.BlockSpec((1, n, tm, LANE), lambda i, j: (i, 0, j, 0))
    tab_spec = pl.BlockSpec((tm, LANE), lambda i, j: (j, 0))
    row_spec = lambda n: pl.BlockSpec((1, tm, n), lambda i, j: (i, j, 0))
    head_t = lambda n: jax.ShapeDtypeStruct((b, n, LANE, l), CDT)
    head_t_spec = lambda n: pl.BlockSpec((1, n, LANE, tm), lambda i, j: (i, 0, 0, j))
    return pl.pallas_call(
        _inproj_kernel,
        out_shape=(head_t(MLA_HEADS), head(MLA_HEADS), head_t(MLA_HEADS), head_t(SWA_HEADS), head(SWA_KV_HEADS),
                   head_t(SWA_KV_HEADS), jax.ShapeDtypeStruct((b, l, S5_W), CDT),
                   jax.ShapeDtypeStruct((b, l, (HY_ORDER + 1) * HY_W), CDT)),
        grid=(b, l // tm),
        in_specs=[row_spec(d),
                  pl.BlockSpec((1, N_MOD, d), lambda i, j: (mod_row(i), 0, 0)),
                  _resident((1, d)), _resident(wp['w_big'].shape),
                  _resident((1, LANE)), _resident((1, 2 * LANE)),
                  _resident(wp['w_ukv'].shape), _resident(wp['w_uq'].shape),
                  tab_spec, tab_spec, tab_spec, tab_spec],
        out_specs=(head_t_spec(MLA_HEADS), head_spec(MLA_HEADS), head_t_spec(MLA_HEADS), head_t_spec(SWA_HEADS),
                   head_spec(SWA_KV_HEADS), head_t_spec(SWA_KV_HEADS), row_spec(S5_W),
                   row_spec((HY_ORDER + 1) * HY_W)),
        compiler_params=_cparams("parallel", "parallel"),
        name="premod_inproj",
    )(x, mod, gpre, wp['w_big'], wp['g_kv'], wp['g_q'], wp['w_ukv'], wp['w_uq'], cm, sm, cw, sw)


MLA_KEY_CHUNK = 512
MLA_VROWS = 80


def _mla_scores(qt_ref, k_refs, s_buf, m_buf):
    qt = qt_ref[0, 0]
    m, off = None, 0
    for k_ref in k_refs:
        n = k_ref.shape[2]
        kc = min(MLA_KEY_CHUNK, n)
        for c in range(n // kc):
            s = _dot(k_ref[0, 0, c * kc:(c + 1) * kc, :], qt)
            s_buf[off + c * kc:off + (c + 1) * kc, :] = s
            cmax = s.max(axis=0, keepdims=True)
            m = cmax if m is None else jnp.maximum(m, cmax)
        off += n
    m_buf[...] = m


def _mla_values(vt_refs, s_buf, m_buf, o_ref):
    m = m_buf[...]
    acc, off = None, 0
    for vt_ref in vt_refs:
        n = vt_ref.shape[3]
        kc = min(MLA_KEY_CHUNK, n)
        for c in range(n // kc):
            p = jnp.exp2(s_buf[off + c * kc:off + (c + 1) * kc, :] - m).astype(CDT)
            pv = _dot(vt_ref[0, 0, 0:MLA_VROWS, c * kc:(c + 1) * kc], p)
            acc = pv if acc is None else acc + pv
        off += n
    o = acc / acc[MLA_V:MLA_V + 1, :]
    o = jnp.concatenate([o, jnp.zeros((LANE - MLA_VROWS, o.shape[1]), F32)], axis=0)
    o_ref[0] = o.T.astype(CDT)


def _mla_kernel(*refs, n_src, ntile):
    qt_ref, o_ref = refs[0], refs[1 + 2 * n_src]
    k_refs = [refs[1 + 2 * i] for i in range(n_src)]
    vt_refs = [refs[2 + 2 * i] for i in range(n_src)]
    s_bufs = refs[2 + 2 * n_src:4 + 2 * n_src]
    m_bufs = refs[4 + 2 * n_src:6 + 2 * n_src]
    t = pl.program_id(2)
    odd = t % 2 == 1

    @pl.when(t == 0)
    def _():
        _mla_scores(qt_ref, k_refs, s_bufs[0], m_bufs[0])

    @pl.when((t > 0) & (t < ntile) & odd)
    def _():
        _mla_scores(qt_ref, k_refs, s_bufs[1], m_bufs[1])
        _mla_values(vt_refs, s_bufs[0], m_bufs[0], o_ref)

    @pl.when((t > 0) & (t < ntile) & jnp.logical_not(odd))
    def _():
        _mla_scores(qt_ref, k_refs, s_bufs[0], m_bufs[0])
        _mla_values(vt_refs, s_bufs[1], m_bufs[1], o_ref)

    @pl.when(t == ntile)
    def _():
        _mla_values(vt_refs, s_bufs[(ntile - 1) % 2], m_bufs[(ntile - 1) % 2], o_ref)


def _mla(qt, kvs):
    b, h, _, l = qt.shape
    tq = min(512, l)
    ntile = l // tq
    in_specs = [pl.BlockSpec((1, 1, LANE, tq), lambda i, j, t: (i, j, 0, jnp.minimum(t, ntile - 1)))]
    args = [qt]
    for k, vt in kvs:
        n = k.shape[2]
        in_specs += [pl.BlockSpec((1, 1, n, LANE), lambda i, j, t: (i, j, 0, 0)),
                     pl.BlockSpec((1, 1, LANE, n), lambda i, j, t: (i, j, 0, 0))]
        args += [k, vt]
    nk = sum(k.shape[2] for k, _ in kvs)
    return pl.pallas_call(
        functools.partial(_mla_kernel, n_src=len(kvs), ntile=ntile),
        out_shape=jax.ShapeDtypeStruct((b, l, h * LANE), CDT),
        grid=(b, h, ntile + 1),
        in_specs=in_specs,
        out_specs=pl.BlockSpec((1, tq, LANE), lambda i, j, t: (i, jnp.maximum(t - 1, 0), j)),
        scratch_shapes=[pltpu.VMEM((nk, tq), F32), pltpu.VMEM((nk, tq), F32),
                        pltpu.VMEM((1, tq), F32), pltpu.VMEM((1, tq), F32)],
        compiler_params=_cparams("parallel", "parallel", "arbitrary"),
        name="mla_attention",
    )(*args)


def _swa_kernel(*refs, band, tq, nblk):
    sink_ref, qt_ref = refs[0], refs[1]
    o_ref = refs[-1]
    t = pl.program_id(1)
    g = SWA_HEADS // SWA_KV_HEADS
    lane = lax.broadcasted_iota(jnp.int32, (1, g * tq), 1)
    qi = jnp.where(lane < tq, lane, lane - tq)
    jp = lax.broadcasted_iota(jnp.int32, (WINDOW, 1), 0)
    jc = lax.broadcasted_iota(jnp.int32, (tq, 1), 0)
    for kv in range(SWA_KV_HEADS):
        qt = jnp.concatenate([qt_ref[0, kv * g + i] for i in range(g)], axis=-1)
        snk = jnp.where(lane < tq, sink_ref[kv * g], sink_ref[kv * g + 1]) * LOG2E
        ss, vts = [], []
        if band:
            kp, kc, kn, vtp, vtc, vtn, kx, vtx = [r[0, kv] for r in refs[2:10]]
            s_p = jnp.where((jp >= qi) & (t > 0), _dot(kp, qt), NEG_INF)
            s_c = jnp.where(jnp.abs(qi - jc) <= WINDOW, _dot(kc, qt), NEG_INF)
            s_n = jnp.where((jp <= qi - (tq - WINDOW)) & (t < nblk - 1), _dot(kn, qt), NEG_INF)
            ss += [s_p, s_c, s_n]
            vts += [vtp, vtc, vtn]
        else:
            kx, vtx = [r[0, kv] for r in refs[2:4]]
        ss.append(_dot(kx, qt))
        vts.append(vtx)
        m = snk
        for s in ss:
            m = jnp.maximum(m, s.max(axis=0, keepdims=True))
        acc = None
        for s, vt in zip(ss, vts):
            pv = _dot(vt[0:MLA_VROWS, :], jnp.exp2(s - m).astype(CDT))
            acc = pv if acc is None else acc + pv
        o = acc / (acc[SWA_HD:SWA_HD + 1, :] + jnp.exp2(snk - m))
        o = jnp.concatenate([o, jnp.zeros((LANE - MLA_VROWS, g * tq), F32)], axis=0)
        for i in range(g):
            o_ref[0, :, (kv * g + i) * LANE:(kv * g + i + 1) * LANE] = o[:, i * tq:(i + 1) * tq].T.astype(CDT)


def _swa(sink, qt, k, vt, kx, vtx, band):
    b, _, _, l = qt.shape
    hk = SWA_KV_HEADS
    tq = min(256, l)
    nblk = l // tq
    r = tq // WINDOW
    nw = l // WINDOW
    in_specs = [pl.BlockSpec(memory_space=pltpu.SMEM),
                pl.BlockSpec((1, SWA_HEADS, LANE, tq), lambda i, t: (i, 0, 0, t))]
    args = [sink, qt]
    if band:
        prev_i = lambda t: jnp.maximum(t * r - 1, 0)
        next_i = lambda t: jnp.minimum((t + 1) * r, nw - 1)
        in_specs += [pl.BlockSpec((1, hk, WINDOW, LANE), lambda i, t: (i, 0, prev_i(t), 0)),
                     pl.BlockSpec((1, hk, tq, LANE), lambda i, t: (i, 0, t, 0)),
                     pl.BlockSpec((1, hk, WINDOW, LANE), lambda i, t: (i, 0, next_i(t), 0)),
                     pl.BlockSpec((1, hk, LANE, WINDOW), lambda i, t: (i, 0, 0, prev_i(t))),
                     pl.BlockSpec((1, hk, LANE, tq), lambda i, t: (i, 0, 0, t)),
                     pl.BlockSpec((1, hk, LANE, WINDOW), lambda i, t: (i, 0, 0, next_i(t)))]
        args += [k, k, k, vt, vt, vt]
    nx = kx.shape[2]
    in_specs += [pl.BlockSpec((1, hk, nx, LANE), lambda i, t: (i, 0, 0, 0)),
                 pl.BlockSpec((1, hk, LANE, nx), lambda i, t: (i, 0, 0, 0))]
    args += [kx, vtx]
    return pl.pallas_call(
        functools.partial(_swa_kernel, band=band, tq=tq, nblk=nblk),
        out_shape=jax.ShapeDtypeStruct((b, l, SWA_HEADS * LANE), CDT),
        grid=(b, nblk),
        in_specs=in_specs,
        out_specs=pl.BlockSpec((1, tq, SWA_HEADS * LANE), lambda i, t: (i, t, 0)),
        compiler_params=_cparams("parallel", "parallel"),
        name="swa_attention",
    )(*args)


def _shortconv_kernel(x_ref, w_ref, b_ref, o_ref):
    x = x_ref[0].astype(F32)
    l = x.shape[0]
    t = lax.broadcasted_iota(jnp.int32, (l, 1), 0)
    prev = jnp.where(t == 0, 0.0, pltpu.roll(x, 1, 0))
    nxt = jnp.where(t == l - 1, 0.0, pltpu.roll(x, l - 1, 0))
    y = b_ref[...] + prev * w_ref[0:1, :] + x * w_ref[1:2, :] + nxt * w_ref[2:3, :]
    o_ref[0] = y.T.astype(CDT)


def _shortconv_t(x, w, bias):
    b, l, c = x.shape
    tc = 256
    return pl.pallas_call(
        _shortconv_kernel,
        out_shape=jax.ShapeDtypeStruct((b, c, l), CDT),
        grid=(b, c // tc),
        in_specs=[pl.BlockSpec((1, l, tc), lambda i, j: (i, 0, j)),
                  pl.BlockSpec((SHORT_K, tc), lambda i, j: (0, j)),
                  pl.BlockSpec((1, tc), lambda i, j: (0, j))],
        out_specs=pl.BlockSpec((1, tc, l), lambda i, j: (i, j, 0)),
        compiler_params=_cparams("parallel", "parallel"),
        name="hyena_shortconv",
    )(x, w, bias)


def _swap(x):
    return jnp.concatenate([x[..., LANE:], x[..., :LANE]], axis=-1)


def _comb(pq, n):
    p, q = pq[:, :n], pq[:, n:]
    return jnp.concatenate([p[..., :LANE] - q[..., LANE:], p[..., LANE:] + q[..., :LANE]], axis=-1)


def _hyena_kernel(v_ref, g1_ref, g2_ref, f1_ref, c1_ref, tw_ref, f2_ref, f2c_ref, ka_ref, kb_ref, bias_ref,
                  o_ref, *, ct, na, kin):
    def load(ref):
        return jnp.concatenate([ref[0, 0], ref[0, 1]], axis=-1).astype(F32)

    z = load(v_ref)
    gates = (g1_ref, g2_ref)
    twa, twb = tw_ref[0], tw_ref[1]
    for o in range(HY_ORDER):
        a = _comb(_bdot(f1_ref[...], z.astype(CDT)), na)
        a = a * twa + _swap(a) * twb
        x = _dot(a.reshape(ct * na, 2 * LANE).astype(CDT), f2_ref[...])
        y = x * ka_ref[o].reshape(ct * na, 2 * LANE) + _swap(x) * kb_ref[o].reshape(ct * na, 2 * LANE)
        bq = _dot(y.astype(CDT), f2c_ref[...]).reshape(ct, na, 2 * LANE)
        bq = bq * twa - _swap(bq) * twb
        yt = _comb(_bdot(c1_ref[...], bq.astype(CDT)), kin)
        z = load(gates[o]) * (yt + bias_ref[o] * z)
    o_ref[0, 0] = z[..., :LANE].astype(CDT)
    o_ref[0, 1] = z[..., LANE:].astype(CDT)


def _hyena_tables(na, kin, ct):
    f1, tw, f2, f2c, c1 = _hyena_dft_np(na, kin)
    bc = lambda m: jnp.broadcast_to(jnp.asarray(m, F32).astype(CDT)[None], (ct,) + m.shape)
    return (bc(f1), bc(c1), jnp.asarray(tw, F32), jnp.asarray(f2, F32).astype(CDT),
            jnp.asarray(f2c, F32).astype(CDT))


def _hyena_dft_np(na, kin):
    n = na * LANE
    ka = np.arange(na)[:, None]
    a = np.arange(kin)[None, :]
    ang1 = -2.0 * np.pi * ((ka * a) % na) / na
    f1 = np.concatenate([np.cos(ang1), np.sin(ang1)], axis=0)
    c1 = np.concatenate([np.cos(ang1).T, -np.sin(ang1).T], axis=0)
    bb = np.arange(LANE)[None, :]
    angt = -2.0 * np.pi * ((ka * bb) % n) / n
    tr, ti = np.cos(angt), np.sin(angt)
    tw = np.stack([np.concatenate([tr, tr], 1), np.concatenate([-ti, ti], 1)])
    b2 = np.arange(LANE)
    ang2 = -2.0 * np.pi * ((b2[:, None] * b2[None, :]) % LANE) / LANE
    fr, fi = np.cos(ang2), np.sin(ang2)
    f2 = np.block([[fr, fi], [-fi, fr]])
    f2c = np.block([[fr, -fi], [fi, fr]])
    return f1, tw, f2, f2c, c1


def _hyena_filter_taps(l, hp):
    t = jnp.linspace(0.0, 1.0, l, dtype=F32)[None, :]
    bands = (HY_EMB - 1) // 2
    w = 2.0 * math.pi * jnp.arange(l, dtype=F32) / l
    fr = jnp.linspace(1e-4, bands - 1, bands, dtype=F32)
    ang = fr[:, None] * w[None, :]
    z = jnp.concatenate([t, jnp.cos(ang), -jnp.sin(ang)], axis=0)
    freq = hp['hy_f_freq']
    hi = lax.Precision.HIGHEST
    h = jnp.sin(freq[0][:, None] * (jnp.dot(hp['hy_f_w1'].T, z, precision=hi) + hp['hy_f_b1'][:, None]))
    h = jnp.sin(freq[1][:, None] * (jnp.dot(hp['hy_f_w2'].T, h, precision=hi) + hp['hy_f_b2'][:, None]))
    h = jnp.dot(hp['hy_f_w3'].T, h, precision=hi).reshape(HY_ORDER, 2, HY_W, l)
    deltas = jnp.abs(jnp.linspace(math.log(HY_DECAY_TARGET) / HY_SLOW_PCT,
                                  math.log(HY_DECAY_TARGET) / HY_FAST_PCT, HY_W, dtype=F32))
    h = h * jnp.exp(-t * deltas[:, None])
    lag0 = (jnp.arange(l) > 0).astype(F32)
    return h[:, 0].reshape(HY_ORDER * HY_W, l), (h[:, 1] * lag0).reshape(HY_ORDER * HY_W, l)


def _split(x):
    hi = x.astype(CDT)
    return hi, (x - hi.astype(F32)).astype(CDT)


def _hyfilt_kernel(hf_ref, hb_ref, f1h_ref, f1l_ref, tw_ref, f2h_ref, f2l_ref, ka_ref, kb_ref, *, rt, na, inv_n):
    xh, xl = _split(jnp.concatenate([hf_ref[...], hb_ref[...]], axis=-1))
    f1h, f1l = f1h_ref[...], f1l_ref[...]
    pq = _bdot(f1h, xh) + _bdot(f1l, xh) + _bdot(f1h, xl)
    p, q = pq[:, :na], pq[:, na:]
    a = jnp.concatenate([jnp.concatenate([p[..., :LANE], q[..., :LANE]], -1),
                         jnp.concatenate([p[..., LANE:], q[..., LANE:]], -1)], axis=0)
    a = a * tw_ref[0] + _swap(a) * tw_ref[1]
    ah, al = _split(a.reshape(2 * rt * na, 2 * LANE))
    x = (_dot(ah, f2h_ref[...]) + _dot(al, f2h_ref[...]) + _dot(ah, f2l_ref[...])).reshape(2 * rt, na, 2 * LANE)
    xf, xb = x[:rt], x[rt:]
    kr = (xf[..., :LANE] + xb[..., :LANE]) * inv_n
    ki = (xf[..., LANE:] - xb[..., LANE:]) * inv_n
    ka_ref[...] = jnp.concatenate([kr, kr], -1)
    kb_ref[...] = jnp.concatenate([-ki, ki], -1)


def _hyena_filter_spectrum(hf, hb, na, kin):
    rows, l = hf.shape
    lp = kin * LANE
    if lp != l:
        hf, hb = [jnp.pad(h, ((0, 0), (0, lp - l))) for h in (hf, hb)]
    hf, hb = hf.reshape(rows, kin, LANE), hb.reshape(rows, kin, LANE)
    rt = 16
    f1, tw, f2 = _hyena_dft_np(na, kin)[:3]
    f1 = jnp.broadcast_to(jnp.asarray(f1, F32)[None], (rt,) + f1.shape)
    f1h, f1l = _split(f1)
    f2h, f2l = _split(jnp.asarray(f2, F32))
    tw = jnp.asarray(tw, F32)
    ka, kb = pl.pallas_call(
        functools.partial(_hyfilt_kernel, rt=rt, na=na, inv_n=1.0 / (na * LANE)),
        out_shape=(jax.ShapeDtypeStruct((rows, na, 2 * LANE), F32),) * 2,
        grid=(rows // rt,),
        in_specs=[pl.BlockSpec((rt, kin, LANE), lambda i: (i, 0, 0)),
                  pl.BlockSpec((rt, kin, LANE), lambda i: (i, 0, 0)),
                  _resident(f1h.shape), _resident(f1l.shape), _resident(tw.shape),
                  _resident(f2h.shape), _resident(f2l.shape)],
        out_specs=(pl.BlockSpec((rt, na, 2 * LANE), lambda i: (i, 0, 0)),) * 2,
        compiler_params=_cparams("parallel"),
        name="hyena_filter_spectrum",
    )(hf, hb, f1h, f1l, tw, f2h, f2l)
    shape = (HY_ORDER, HY_W, na, 2 * LANE)
    return ka.reshape(shape), kb.reshape(shape)


def _hyena(hy, hp, l_true):
    b, l, _ = hy.shape
    ut = _shortconv_t(hy, hp['hy_conv_w'], hp['hy_conv_b'].reshape(1, -1))
    kin = max(l // LANE, 16)
    na = 2 * kin
    lp = kin * LANE
    if lp != l:
        ut = jnp.pad(ut, ((0, 0), (0, 0), (0, lp - l)))
    ut = ut.reshape(b // 2, 2, (HY_ORDER + 1) * HY_W, kin, LANE)
    ct = 16
    nc = HY_W // ct
    f1, c1, tw, f2, f2c = _hyena_tables(na, kin, ct)
    ka, kb = hp['ka'], hp['kb']
    bias = jnp.broadcast_to(hp['hy_bias'].reshape(HY_ORDER, HY_W, 1, 1), (HY_ORDER, HY_W, 1, 2 * LANE))
    blk = lambda off: pl.BlockSpec((1, 2, ct, kin, LANE), lambda c, p: (p, 0, c + off * nc, 0, 0))
    out = pl.pallas_call(
        functools.partial(_hyena_kernel, ct=ct, na=na, kin=kin),
        out_shape=jax.ShapeDtypeStruct((b // 2, 2, HY_W, kin, LANE), CDT),
        grid=(nc, b // 2),
        in_specs=[blk(0), blk(1), blk(2),
                  _resident(f1.shape), _resident(c1.shape), _resident(tw.shape),
                  _resident(f2.shape), _resident(f2c.shape),
                  pl.BlockSpec((HY_ORDER, ct, na, 2 * LANE), lambda c, p: (0, c, 0, 0)),
                  pl.BlockSpec((HY_ORDER, ct, na, 2 * LANE), lambda c, p: (0, c, 0, 0)),
                  pl.BlockSpec((HY_ORDER, ct, 1, 2 * LANE), lambda c, p: (0, c, 0, 0))],
        out_specs=pl.BlockSpec((1, 2, ct, kin, LANE), lambda c, p: (p, 0, c, 0, 0)),
        compiler_params=_cparams("parallel", "arbitrary"),
        name="hyena_longconv",
    )(ut, ut, ut, f1, c1, tw, f2, f2c, ka, kb, bias)
    out = out.reshape(b, HY_W, lp)[:, :, :l]
    return jnp.swapaxes(out, 1, 2)


def _s5_kernel(u_ref, psel_ref, pselt_ref, tloc_ref, wx_ref, wout_ref, d_ref, h0_ref, y_ref, hfin_ref,
               x_scr, h_scr, up_scr, *, nchunk, nb):
    w = S5_T * S5_GC
    rows = u_ref.shape[1]
    rblk = min(S5_ROW_BLOCK, rows)
    for i in range(rows // rblk):
        rs = slice(i * rblk, (i + 1) * rblk)
        up = _dot(u_ref[0, rs, :], psel_ref[0]).astype(CDT)
        up_scr[rs, :] = up
        x_scr[rs, :] = _dot(up, wx_ref[0])
    d = d_ref[0]
    dfr, dfi, dbr, dbi = d[0:1], d[1:2], d[2:3], d[3:4]

    def body(j, carry):
        hr, hi, gr, gi = carry
        rf = pl.multiple_of(j * nb, nb)
        rb = pl.multiple_of((nchunk - 1 - j) * nb, nb)
        h_scr[pl.ds(rf, nb), 0:LANE] = hr
        h_scr[pl.ds(rf, nb), LANE:2 * LANE] = hi
        h_scr[pl.ds(rb, nb), 2 * LANE:3 * LANE] = gr
        h_scr[pl.ds(rb, nb), 3 * LANE:4 * LANE] = gi
        xr = x_scr[pl.ds(rf, nb), 0:LANE]
        xi = x_scr[pl.ds(rf, nb), LANE:2 * LANE]
        yr = x_scr[pl.ds(rb, nb), 2 * LANE:3 * LANE]
        yi = x_scr[pl.ds(rb, nb), 3 * LANE:4 * LANE]
        return (dfr * hr - dfi * hi + xr, dfr * hi + dfi * hr + xi,
                dbr * gr - dbi * gi + yr, dbr * gi + dbi * gr + yi)

    fin = lax.fori_loop(0, nchunk, body, tuple(h0_ref[0, k] for k in range(4)))
    for k in range(4):
        hfin_ref[0, k] = fin[k]
    r = pl.program_id(1)
    for i in range(rows // rblk):
        rs = slice(i * rblk, (i + 1) * rblk)
        hs = h_scr[rs, :].astype(CDT)
        y0 = _dot(up_scr[rs, :w], tloc_ref[0]) + _dot(hs, wout_ref[0, :, :w])
        y1 = _dot(up_scr[rs, w:], tloc_ref[1]) + _dot(hs, wout_ref[0, :, w:])
        contrib = _dot(jnp.concatenate([y0, y1], axis=-1).astype(CDT), pselt_ref[0]).astype(CDT)

        @pl.when(r == 0)
        def _():
            y_ref[0, rs, :] = contrib

        @pl.when(r > 0)
        def _():
            y_ref[0, rs, :] = y_ref[0, rs, :] + contrib


S5_HALF_GROUPS = LANE // S5_GC
S5_HALF_PAIRS = S5_HALF_GROUPS // 2


def _s5_select():
    ri = jnp.arange(S5_T * LANE)
    r_sig, r_grp, r_ch = ri // LANE, (ri % LANE) // S5_GC, ri % S5_GC
    ci = jnp.arange(2 * S5_T * S5_GC)
    c_grp, c_sig, c_ch = ci // (S5_T * S5_GC), (ci % (S5_T * S5_GC)) // S5_GC, ci % S5_GC
    same = (r_sig[:, None] == c_sig[None, :]) & (r_ch[:, None] == c_ch[None, :])
    sel = jnp.stack([(same & (r_grp[:, None] == 2 * q + c_grp[None, :])) for q in range(S5_HALF_PAIRS)])
    sel = sel.astype(CDT)
    return sel, jnp.swapaxes(sel, 1, 2)


def _s5(u, ops, h0):
    b, l, _ = u.shape
    nchunk = l // S5_T
    w = S5_T * S5_GC
    rows = nchunk * b
    nhalf = S5_W // LANE
    wide = S5_T * LANE
    uh = u.reshape(b, nchunk, S5_T, nhalf, LANE).transpose(3, 1, 0, 2, 4).reshape(nhalf, rows, wide)
    psel, pselt = _s5_select()
    pair = lambda h, r: h * S5_HALF_PAIRS + r
    y, hfin = pl.pallas_call(
        functools.partial(_s5_kernel, nchunk=nchunk, nb=b),
        out_shape=(jax.ShapeDtypeStruct((nhalf, rows, wide), CDT),
                   jax.ShapeDtypeStruct((S5_GROUPS // 2, 4, b, LANE), F32)),
        grid=(nhalf, S5_HALF_PAIRS),
        in_specs=[pl.BlockSpec((1, rows, wide), lambda h, r: (h, 0, 0), pipeline_mode=pl.Buffered(1)),
                  pl.BlockSpec((1, wide, 2 * w), lambda h, r: (r, 0, 0)),
                  pl.BlockSpec((1, 2 * w, wide), lambda h, r: (r, 0, 0)),
                  pl.BlockSpec((2, w, w), lambda h, r: (pair(h, r), 0, 0)),
                  pl.BlockSpec((1, 2 * w, 2 * w), lambda h, r: (pair(h, r), 0, 0)),
                  pl.BlockSpec((1, 2 * w, 2 * w), lambda h, r: (pair(h, r), 0, 0)),
                  pl.BlockSpec((1, 4, LANE), lambda h, r: (pair(h, r), 0, 0)),
                  pl.BlockSpec((1, 4, b, LANE), lambda h, r: (pair(h, r), 0, 0, 0))],
        out_specs=(pl.BlockSpec((1, rows, wide), lambda h, r: (h, 0, 0)),
                   pl.BlockSpec((1, 4, b, LANE), lambda h, r: (pair(h, r), 0, 0, 0))),
        scratch_shapes=[pltpu.VMEM((rows, 2 * w), F32), pltpu.VMEM((rows, 2 * w), F32),
                        pltpu.VMEM((rows, 2 * w), CDT)],
        compiler_params=_cparams("parallel", "arbitrary"),
        name="s5_chunked",
    )(uh, psel, pselt, ops['tloc'], ops['wx'], ops['wout'], ops['d16'], h0)
    y = y.reshape(nhalf, nchunk, b, S5_T, LANE).transpose(2, 1, 3, 0, 4).reshape(b, l, S5_W)
    return y, hfin


def _s5_operators(p):
    g, pp, gc, t = S5_GROUPS, S5_P, S5_GC, S5_T
    npair = g // 2
    n = jnp.arange(t + 1, dtype=F32)[:, None, None]
    tops, wxs, wouts, d16 = [], [], [], []
    sig = jnp.arange(t)
    for d in range(2):
        a_re, a_im = p['s5_a_re'][d], p['s5_a_im'][d]
        dt = jnp.exp(p['s5_log_dt'][d])[:, None]
        mag1 = jnp.exp(dt * a_re)
        ab_re, ab_im = mag1 * jnp.cos(dt * a_im), mag1 * jnp.sin(dt * a_im)
        den = a_re * a_re + a_im * a_im
        f_re = ((ab_re - 1.0) * a_re + ab_im * a_im) / den
        f_im = (ab_im * a_re - (ab_re - 1.0) * a_im) / den
        mag = jnp.exp(n * (dt * a_re)[None])
        pr, pi = mag * jnp.cos(n * (dt * a_im)[None]), mag * jnp.sin(n * (dt * a_im)[None])
        b_re, b_im = p['s5_b_re'][d], p['s5_b_im'][d]
        bt_re = f_re[..., None] * b_re - f_im[..., None] * b_im
        bt_im = f_re[..., None] * b_im + f_im[..., None] * b_re
        c_re, c_im = p['s5_c_re'][d], p['s5_c_im'][d]
        ca_re = c_re[None] * pr[:, :, None, :] - c_im[None] * pi[:, :, None, :]
        ca_im = c_re[None] * pi[:, :, None, :] + c_im[None] * pr[:, :, None, :]
        hi = lax.Precision.HIGHEST
        kk = (jnp.einsum('ngcp,gpk->ngck', ca_re[:t], bt_re, precision=hi)
              - jnp.einsum('ngcp,gpk->ngck', ca_im[:t], bt_im, precision=hi))
        lag = (sig[None, :] - sig[:, None]) if d == 0 else (sig[:, None] - sig[None, :])
        top = jnp.where((lag >= 0)[:, :, None, None, None], kk[jnp.clip(lag, 0, t - 1)], 0.0)
        tops.append(top.transpose(2, 0, 4, 1, 3).reshape(g, t * gc, t * gc))
        pw = (t - 1 - sig) if d == 0 else sig
        xr = pr[pw][..., None] * bt_re[None] - pi[pw][..., None] * bt_im[None]
        xi = pr[pw][..., None] * bt_im[None] + pi[pw][..., None] * bt_re[None]
        wxs.append((xr.transpose(1, 0, 3, 2).reshape(g, t * gc, pp), xi.transpose(1, 0, 3, 2).reshape(g, t * gc, pp)))
        po = (sig + 1) if d == 0 else (t - sig)
        wouts.append((ca_re[po].transpose(1, 3, 0, 2).reshape(g, pp, t * gc),
                      -ca_im[po].transpose(1, 3, 0, 2).reshape(g, pp, t * gc)))
        d16.append((pr[t], pi[t]))
    eye = jnp.eye(t * gc, dtype=F32).reshape(t, gc, t, gc)
    skip = (eye[None] * p['s5_d'].reshape(g, 1, 1, 1, gc)).reshape(g, t * gc, t * gc)
    tloc = tops[0] + tops[1] + skip

    def pair_cols(m):
        m = m.reshape(npair, 2, m.shape[1], m.shape[2])
        z = jnp.zeros_like(m[:, 0])
        return jnp.concatenate([jnp.concatenate([m[:, 0], z], -1), jnp.concatenate([z, m[:, 1]], -1)], -2)

    wx = jnp.concatenate([pair_cols(wxs[0][0]), pair_cols(wxs[0][1]), pair_cols(wxs[1][0]), pair_cols(wxs[1][1])],
                         axis=-1)
    wout = jnp.concatenate([pair_cols(wouts[0][0]), pair_cols(wouts[0][1]), pair_cols(wouts[1][0]),
                            pair_cols(wouts[1][1])], axis=-2)
    dd = jnp.stack([d16[0][0], d16[0][1], d16[1][0], d16[1][1]], axis=0)
    dd = dd.reshape(4, npair, 2 * pp).transpose(1, 0, 2)
    return dict(tloc=tloc.astype(CDT), wx=wx.astype(CDT), wout=wout.astype(CDT), d16=dd)


def _merge_kernel(x_ref, mod_ref, gpre_ref, gpost_ref, omla_ref, ohy_ref, oswa_ref, ys5_ref,
                  wgate_ref, bgate_ref, wmla_ref, why_ref, wswa_ref, ws5_ref, gluw_ref, glub_ref, wout_ref, o_ref):
    x = x_ref[0]
    mod = mod_ref[0]
    u = _pre_mod(x, gpre_ref[...], mod, 1).astype(CDT)
    y = ys5_ref[0].astype(F32)
    g = 0.5 * y * (1.0 + jnp.tanh(math.sqrt(2.0 / math.pi) * (y + 0.044715 * (y * y * y))))
    o_s5 = (g * _sigmoid(_dot(g.astype(CDT), gluw_ref[...]) + glub_ref[...])).astype(CDT)
    outs = (omla_ref[0], ohy_ref[0], oswa_ref[0], o_s5)
    wbr = (wmla_ref, why_ref, wswa_ref, ws5_ref)
    m = jnp.zeros(x.shape, F32)
    for i in range(4):
        gate = _sigmoid(_dot(u, wgate_ref[:, i * D_MODEL:(i + 1) * D_MODEL]) + bgate_ref[:, i * D_MODEL:(i + 1) * D_MODEL])
        m = m + gate * _dot(outs[i], wbr[i][...])
    f = _dot(m.astype(CDT), wout_ref[...])
    o_ref[0] = x + mod[5:6, :] * _rms(f, gpost_ref[...])


def _merge(x, mod, mod_row, gpre, gpost, o_mla, o_hy, o_swa, y_s5, wp):
    b, l, d = x.shape
    tm = min(512, l)
    row_spec = lambda n: pl.BlockSpec((1, tm, n), lambda i, j: (i, j, 0))
    names = ('w_gate', 'b_gate', 'w_br_mla', 'w_br_hy', 'w_br_swa', 'w_br_s5', 'glu_w', 'glu_b', 'w_out')
    return pl.pallas_call(
        _merge_kernel,
        out_shape=jax.ShapeDtypeStruct(x.shape, F32),
        grid=(b, l // tm),
        in_specs=[row_spec(d), pl.BlockSpec((1, N_MOD, d), lambda i, j: (mod_row(i), 0, 0)),
                  _resident((1, d)), _resident((1, d)),
                  row_spec(o_mla.shape[-1]), row_spec(o_hy.shape[-1]), row_spec(o_swa.shape[-1]),
                  row_spec(y_s5.shape[-1])] + [_resident(wp[k].shape) for k in names],
        out_specs=row_spec(d),
        compiler_params=_cparams("parallel", "parallel"),
        name="merge_out",
    )(x, mod, gpre, gpost, o_mla, o_hy, o_swa, y_s5, *[wp[k] for k in names])


def _rot_partner(w, half):
    return jnp.concatenate([-w[:, half:], w[:, :half]], axis=1)


def _pad_cols(w, n):
    return jnp.pad(w, ((0, 0), (0, n - w.shape[1])))


def _pad_rows(w, n):
    return jnp.pad(w, ((0, n - w.shape[0]), (0, 0)))


def _prep_inproj(w_in, w_ukv, w_uq, g_kv, g_q):
    d = w_in.shape[0]
    zeros = lambda n: jnp.zeros((d, n), F32)
    krope = w_in[:, I_KROPE:I_KROPE + MLA_ROPE]
    kr = jnp.concatenate([zeros(MLA_NOPE), krope, zeros(LANE - MLA_NOPE - MLA_ROPE)], 1)
    krs = jnp.concatenate([zeros(MLA_NOPE), _rot_partner(krope, MLA_ROPE // 2), zeros(LANE - MLA_NOPE - MLA_ROPE)], 1)
    ckv = w_in[:, I_CKV:I_CKV + MLA_KV_LORA]
    cq = _pad_cols(w_in[:, I_CQ:I_CQ + MLA_Q_LORA], 2 * LANE)

    def heads(w, nh, partner):
        cols = []
        for h in range(nh):
            wh = w[:, h * SWA_HD:(h + 1) * SWA_HD]
            if partner:
                wh = _rot_partner(wh, SWA_HD // 2)
            cols.append(_pad_cols(wh, LANE))
        return jnp.concatenate(cols, 1)

    swq = w_in[:, I_SWQ:I_SWQ + SWA_HEADS * SWA_HD]
    swk = w_in[:, I_SWK:I_SWK + SWA_KV_HEADS * SWA_HD]
    swv = w_in[:, I_SWV:I_SWV + SWA_KV_HEADS * SWA_HD]
    w_big = jnp.concatenate([kr, krs, ckv, cq,
                             heads(swq, SWA_HEADS, False), heads(swq, SWA_HEADS, True),
                             heads(swk, SWA_KV_HEADS, False), heads(swk, SWA_KV_HEADS, True),
                             heads(swv, SWA_KV_HEADS, False),
                             w_in[:, I_S5:I_S5 + S5_W], w_in[:, I_HY:I_HY + (HY_ORDER + 1) * HY_W]], axis=1)
    assert w_big.shape[1] == N_BIG
    kvw = w_ukv.reshape(MLA_KV_LORA, MLA_HEADS, MLA_NOPE + MLA_V)
    kslots = [_pad_cols(kvw[:, h, :MLA_NOPE], LANE) for h in range(MLA_HEADS)]
    vslots = [_pad_cols(kvw[:, h, MLA_NOPE:], LANE) for h in range(MLA_HEADS)]
    w_ukv_p = jnp.concatenate(kslots + vslots, axis=1)
    qw = w_uq.reshape(MLA_Q_LORA, MLA_HEADS, MLA_NOPE + MLA_ROPE)
    qslots = [_pad_cols(qw[:, h], LANE) for h in range(MLA_HEADS)]
    pslots = [_pad_cols(jnp.concatenate([jnp.zeros((MLA_Q_LORA, MLA_NOPE), F32),
                                         _rot_partner(qw[:, h, MLA_NOPE:], MLA_ROPE // 2)], 1), LANE)
              for h in range(MLA_HEADS)]
    w_uq_p = _pad_rows(jnp.concatenate(qslots + pslots, axis=1), 2 * LANE)
    return dict(w_big=w_big.astype(CDT), w_ukv=w_ukv_p.astype(CDT), w_uq=w_uq_p.astype(CDT),
                g_kv=g_kv.reshape(1, -1), g_q=_pad_cols(g_q.reshape(1, -1), 2 * LANE))


def _pad_head_rows(w, nh, hd):
    w = w.reshape(nh, hd, w.shape[-1])
    return jnp.pad(w, ((0, 0), (0, LANE - hd), (0, 0))).reshape(nh * LANE, -1)


def _rope_tables(n_tokens, use_rope):
    ones = jnp.ones((n_tokens, LANE), F32)
    zeros = jnp.zeros((n_tokens, LANE), F32)
    if not use_rope:
        return ones, zeros, ones, zeros

    def axial(rot_dim):
        rows = n_tokens // GRID_W
        r = jnp.repeat(jnp.arange(rows, dtype=F32), GRID_W)
        col = jnp.tile(jnp.arange(GRID_W, dtype=F32), rows)
        n_freq = rot_dim // 4
        freqs = ROPE_THETA ** (-jnp.arange(n_freq, dtype=F32) / n_freq)
        ang = jnp.concatenate([r[:, None] * freqs, col[:, None] * freqs], axis=-1)
        return jnp.cos(ang), jnp.sin(ang)

    c, s = axial(MLA_ROPE)
    pad = LANE - MLA_NOPE - MLA_ROPE
    cm = jnp.concatenate([ones[:, :MLA_NOPE], c, c, ones[:, :pad]], 1)
    sm = jnp.concatenate([zeros[:, :MLA_NOPE], s, s, zeros[:, :pad]], 1)
    c, s = axial(SWA_HD)
    cw = jnp.concatenate([c, c, ones[:, :LANE - SWA_HD]], 1)
    sw = jnp.concatenate([s, s, zeros[:, :LANE - SWA_HD]], 1)
    return cm, sm, cw, sw


def _hyena_params(p, l):
    kin = max(l // LANE, 16)
    na = 2 * kin
    hf, hb = _hyena_filter_taps(l, p)
    ka, kb = _hyena_filter_spectrum(hf, hb, na, kin)
    return dict(hy_conv_w=p['hy_conv_w'], hy_conv_b=p['hy_conv_b'], hy_bias=p['hy_bias'], ka=ka, kb=kb)


def kernel(x, c, ctx, c_ctx, w_ada, b_ada, norm_pre, norm_post, ffn1_up, ffn1_down, ffn2_up, ffn2_down,
           w_in, mla_q_norm, mla_kv_norm, mla_w_uq, mla_w_ukv, hy_conv_w, hy_conv_b, hy_f_w1, hy_f_b1,
           hy_f_freq, hy_f_w2, hy_f_b2, hy_f_w3, hy_bias, swa_sink, s5_a_re, s5_a_im, s5_log_dt,
           s5_b_re, s5_b_im, s5_c_re, s5_c_im, s5_d, s5_glu_w, s5_glu_b, w_gate, b_gate,
           w_br_mla, w_br_hy, w_br_swa, w_br_s5, w_out):
    stacked = dict(w_ada=w_ada, b_ada=b_ada, norm_pre=norm_pre, norm_post=norm_post,
                   ffn1_up=ffn1_up, ffn1_down=ffn1_down, ffn2_up=ffn2_up, ffn2_down=ffn2_down,
                   w_in=w_in, mla_q_norm=mla_q_norm, mla_kv_norm=mla_kv_norm, mla_w_uq=mla_w_uq,
                   mla_w_ukv=mla_w_ukv, hy_conv_w=hy_conv_w, hy_conv_b=hy_conv_b, hy_f_w1=hy_f_w1,
                   hy_f_b1=hy_f_b1, hy_f_freq=hy_f_freq, hy_f_w2=hy_f_w2, hy_f_b2=hy_f_b2, hy_f_w3=hy_f_w3,
                   hy_bias=hy_bias, swa_sink=swa_sink, s5_a_re=s5_a_re, s5_a_im=s5_a_im,
                   s5_log_dt=s5_log_dt, s5_b_re=s5_b_re, s5_b_im=s5_b_im, s5_c_re=s5_c_re,
                   s5_c_im=s5_c_im, s5_d=s5_d, s5_glu_w=s5_glu_w, s5_glu_b=s5_glu_b,
                   w_gate=w_gate, b_gate=b_gate, w_br_mla=w_br_mla, w_br_hy=w_br_hy,
                   w_br_swa=w_br_swa, w_br_s5=w_br_s5, w_out=w_out)
    depth = w_ada.shape[0]
    nb, seq, d = x.shape
    nctx = ctx.shape[1]
    assert nb % 2 == 0 and seq % 256 == 0 and nctx % 256 == 0

    rows = -(-(nb + 1) // 8) * 8
    cvec = jnp.zeros((rows, d), F32).at[:nb].set(c).at[nb].set(c_ctx)
    mods = _modulation(cvec, w_ada, b_ada)
    lat_row = lambda i: i
    ctx_row = lambda i: nb

    rope_l = _rope_tables(seq, True)
    rope_c = _rope_tables(nctx, False)
    h0_zero = jnp.zeros((S5_GROUPS // 2, 4, nb, LANE), F32)

    xl, xc = x, ctx
    for l in range(depth):
        p = {name: arr[l] for name, arr in stacked.items()}
        mod = mods[l]
        ctx_out = l < depth - 1
        gpre = [p['norm_pre'][i].reshape(1, d) for i in range(N_SUB)]
        gpost = [p['norm_post'][i].reshape(1, d) for i in range(N_SUB)]
        f1u, f1d = p['ffn1_up'].astype(CDT), p['ffn1_down'].astype(CDT)
        f2u, f2d = p['ffn2_up'].astype(CDT), p['ffn2_down'].astype(CDT)
        wp_in = _prep_inproj(p['w_in'], p['mla_w_ukv'], p['mla_w_uq'], p['mla_kv_norm'], p['mla_q_norm'])
        wp_mg = dict(
            w_gate=jnp.concatenate([p['w_gate'][i] for i in range(4)], axis=1).astype(CDT),
            b_gate=p['b_gate'].reshape(1, -1),
            w_br_mla=_pad_head_rows(p['w_br_mla'], MLA_HEADS, MLA_V).astype(CDT),
            w_br_hy=p['w_br_hy'].astype(CDT),
            w_br_swa=_pad_head_rows(p['w_br_swa'], SWA_HEADS, SWA_HD).astype(CDT),
            w_br_s5=p['w_br_s5'].astype(CDT),
            glu_w=p['s5_glu_w'].astype(CDT), glu_b=p['s5_glu_b'].reshape(1, -1),
            w_out=p['w_out'].astype(CDT))
        s5_ops = _s5_operators(p)

        xl = _ffn(xl, mod, lat_row, gpre[0], gpost[0], f1u, f1d, 0)
        xc = _ffn(xc, mod, ctx_row, gpre[0], gpost[0], f1u, f1d, 0)

        qm_c, km_c, vm_c, qw_c, kw_c, vw_c, s5u_c, hy_c = _inproj(xc, mod, ctx_row, gpre[1], wp_in, rope_c)
        qm_l, km_l, vm_l, qw_l, kw_l, vw_l, s5u_l, hy_l = _inproj(xl, mod, lat_row, gpre[1], wp_in, rope_l)

        ys5_c, h_ctx = _s5(s5u_c, s5_ops, h0_zero)
        ys5_l, _ = _s5(s5u_l, s5_ops, h_ctx)
        o_mla = _mla(qm_l, [(km_l, vm_l), (km_c, vm_c)])
        o_swa = _swa(p['swa_sink'], qw_l, kw_l, vw_l, kw_c, vw_c, True)
        o_hy = _hyena(hy_l, _hyena_params(p, seq), seq)
        xl = _merge(xl, mod, lat_row, gpre[1], gpost[1], o_mla, o_hy, o_swa, ys5_l, wp_mg)
        xl = _ffn(xl, mod, lat_row, gpre[2], gpost[2], f2u, f2d, 2)
        if ctx_out:
            o_mla_c = _mla(qm_c, [(km_c, vm_c)])
            o_swa_c = _swa(p['swa_sink'], qw_c, kw_c, vw_c, kw_c, vw_c, False)
            o_hy_c = _hyena(hy_c, _hyena_params(p, nctx), nctx)
            xc = _merge(xc, mod, ctx_row, gpre[1], gpost[1], o_mla_c, o_hy_c, o_swa_c, ys5_c, wp_mg)
            xc = _ffn(xc, mod, ctx_row, gpre[2], gpost[2], f2u, f2d, 2)
    return xl
```

```python
import functools
import math

import numpy as np
import jax
import jax.numpy as jnp
from jax import lax
from jax.experimental import pallas as pl
from jax.experimental.pallas import tpu as pltpu

F32 = jnp.float32
CDT = jnp.bfloat16

D_MODEL = 1024
D_FF = 2816
N_SUB = 3
N_MOD = 3 * N_SUB
MACARON_W = 0.5
ROPE_THETA = 10000.0
GRID_W = 64
EPS = 1e-6
NEG_INF = -1e30

MLA_HEADS = 4
MLA_NOPE = 64
MLA_ROPE = 32
MLA_V = 64
MLA_Q_LORA = 192
MLA_KV_LORA = 128
MLA_SCALE = (MLA_NOPE + MLA_ROPE) ** -0.5
LOG2E = math.log2(math.e)

HY_W = 256
HY_ORDER = 2
HY_EMB = 33
HY_DECAY_TARGET = 1e-2
HY_FAST_PCT = 0.3
HY_SLOW_PCT = 1.5
SHORT_K = 3

SWA_HEADS = 4
SWA_KV_HEADS = 2
SWA_HD = 64
WINDOW = 128
SWA_SCALE = SWA_HD ** -0.5

S5_W = 256
S5_GC = 16
S5_GROUPS = S5_W // S5_GC
S5_P = 64
S5_BLOCK_ROWS = 512
S5_T = 16

LANE = 128
MXU = 256
VMEM_LIMIT = 56 * 1024 * 1024

_IN_SIZES = (MLA_KV_LORA, MLA_ROPE, SWA_KV_HEADS * SWA_HD, SWA_KV_HEADS * SWA_HD, S5_W, MLA_Q_LORA,
             SWA_HEADS * SWA_HD, (HY_ORDER + 1) * HY_W)
_IN_OFF = np.concatenate([[0], np.cumsum(_IN_SIZES)])
(I_CKV, I_KROPE, I_SWK, I_SWV, I_S5, I_CQ, I_SWQ, I_HY) = [int(v) for v in _IN_OFF[:-1]]

O_KR, O_KRS, O_CKV, O_CQ = 0, 128, 256, 384
O_SQ, O_SQS = 640, 1152
O_SK, O_SKS, O_SV = 1664, 1920, 2176
O_S5, O_HY = 2432, 2688
N_BIG = O_HY + (HY_ORDER + 1) * HY_W


def _cparams(*sem):
    return pltpu.CompilerParams(dimension_semantics=sem, vmem_limit_bytes=VMEM_LIMIT)


def _resident(shape):
    nd = len(shape)
    return pl.BlockSpec(shape, lambda *_: (0,) * nd, pipeline_mode=pl.Buffered(1))


def _layer_resident(shape, layer):
    nd = len(shape) - 1
    return pl.BlockSpec((1,) + tuple(shape[1:]), lambda *_: (layer,) + (0,) * nd, pipeline_mode=pl.Buffered(1))


def _dot(a, b):
    return jnp.dot(a, b, preferred_element_type=F32)


def _dot_nt(a, b):
    return lax.dot_general(a, b, (((1,), (1,)), ((), ())), preferred_element_type=F32)


def _bdot(a, b):
    return lax.dot_general(a, b, (((2,), (1,)), ((0,), (0,))), preferred_element_type=F32)


def _rms(x, g):
    return x * lax.rsqrt(jnp.mean(x * x, axis=-1, keepdims=True) + EPS) * g


def _sigmoid(x):
    return 1.0 / (1.0 + jnp.exp(-x))


def _pre_mod(x, gpre, mod, sub):
    return _rms(x, gpre) * (1.0 + mod[3 * sub + 1:3 * sub + 2, :]) + mod[3 * sub:3 * sub + 1, :]


def _mod_kernel(c_ref, w_ref, b_ref, o_ref):
    c = c_ref[...]
    s = c * _sigmoid(c)
    w = w_ref[0]
    s_hi = s.astype(CDT)
    s_lo = (s - s_hi.astype(F32)).astype(CDT)
    w_hi = w.astype(CDT)
    w_lo = (w - w_hi.astype(F32)).astype(CDT)
    o_ref[0] = _dot(s_hi, w_hi) + _dot(s_hi, w_lo) + _dot(s_lo, w_hi) + b_ref[0]


def _modulation(cvec, w_ada, b_ada):
    depth, d, n = w_ada.shape
    rows = cvec.shape[0]
    tn = n // 8
    out = pl.pallas_call(
        _mod_kernel,
        out_shape=jax.ShapeDtypeStruct((depth, rows, n), F32),
        grid=(depth, n // tn),
        in_specs=[pl.BlockSpec((rows, d), lambda l, j: (0, 0)),
                  pl.BlockSpec((1, d, tn), lambda l, j: (l, 0, j)),
                  pl.BlockSpec((1, 1, tn), lambda l, j: (l, 0, j))],
        out_specs=pl.BlockSpec((1, rows, tn), lambda l, j: (l, 0, j)),
        compiler_params=_cparams("arbitrary", "arbitrary"),
        name="modulation",
    )(cvec, w_ada, b_ada.reshape(depth, 1, n))
    return out.reshape(depth, rows, N_MOD, D_MODEL)


FFN_CHUNK = 256


def _ffn_kernel(x_ref, mod_ref, gpre_ref, gpost_ref, wup_ref, wdn_ref, o_ref, *, sub):
    x = x_ref[0]
    mod = mod_ref[0]
    u = _pre_mod(x, gpre_ref[...], mod, sub).astype(CDT)
    acc = jnp.zeros(x.shape, F32)
    for c in range(D_FF // FFN_CHUNK):
        lo = c * FFN_CHUNK
        a = _dot(u, wup_ref[0, :, lo:lo + FFN_CHUNK])
        b = _dot(u, wup_ref[0, :, D_FF + lo:D_FF + lo + FFN_CHUNK])
        h = (a * _sigmoid(a) * b).astype(CDT)
        acc = acc + _dot(h, wdn_ref[0, lo:lo + FFN_CHUNK, :])
    gate = mod[3 * sub + 2:3 * sub + 3, :]
    o_ref[0] = x + MACARON_W * gate * _rms(acc, gpost_ref[...])


def _ffn(x, mod, mod_row, gpre, gpost, wup, wdn, layer, sub):
    b, l, d = x.shape
    tm = min(512, l)
    return pl.pallas_call(
        functools.partial(_ffn_kernel, sub=sub),
        out_shape=jax.ShapeDtypeStruct(x.shape, F32),
        grid=(b, l // tm),
        in_specs=[pl.BlockSpec((1, tm, d), lambda i, j: (i, j, 0)),
                  pl.BlockSpec((1, N_MOD, d), lambda i, j: (mod_row(i), 0, 0)),
                  _resident((1, d)), _resident((1, d)),
                  _layer_resident(wup.shape, layer), _layer_resident(wdn.shape, layer)],
        out_specs=pl.BlockSpec((1, tm, d), lambda i, j: (i, j, 0)),
        compiler_params=_cparams("parallel", "parallel"),
        name="ffn_sublayer",
    )(x, mod, gpre, gpost, wup, wdn)


def _inproj_kernel(x_ref, mod_ref, gpre_ref, wbig_ref, gkv_ref, gq_ref, wukv_ref, wuq_ref,
                   cm_ref, sm_ref, cw_ref, sw_ref,
                   qm_ref, km_ref, vm_ref, qw_ref, kw_ref, vw_ref, s5_ref, hy_ref, s5_scr):
    u = _pre_mod(x_ref[0], gpre_ref[...], mod_ref[0], 1).astype(CDT)

    def seg(off, n):
        return _dot(u, wbig_ref[:, off:off + n])

    cm, sm = cm_ref[...], sm_ref[...]
    cw, sw = cw_ref[...], sw_ref[...]
    kr = seg(O_KR, LANE) * cm + seg(O_KRS, LANE) * sm
    ckv = seg(O_CKV, LANE)
    kvn = _rms(ckv, gkv_ref[...]).astype(CDT)
    kv = _dot(kvn, wukv_ref[...])
    ones_lane = (lax.broadcasted_iota(jnp.int32, (1, LANE), 1) == MLA_V).astype(F32)
    for h in range(MLA_HEADS):
        km_ref[0, h] = (kv[:, h * LANE:(h + 1) * LANE] + kr).astype(CDT)
        vm_ref[0, h] = (kv[:, (MLA_HEADS + h) * LANE:(MLA_HEADS + h + 1) * LANE] + ones_lane).T.astype(CDT)
    cq = seg(O_CQ, 2 * LANE)
    cqn = (cq * lax.rsqrt(jnp.sum(cq * cq, axis=-1, keepdims=True) * (1.0 / MLA_Q_LORA) + EPS)
           * gq_ref[...]).astype(CDT)
    qq = _dot(cqn, wuq_ref[...])
    for h in range(MLA_HEADS):
        q = qq[:, h * LANE:(h + 1) * LANE] * cm + qq[:, (MLA_HEADS + h) * LANE:(MLA_HEADS + h + 1) * LANE] * sm
        qm_ref[0, h] = (q * (MLA_SCALE * LOG2E)).T.astype(CDT)
    sq, sqs = seg(O_SQ, SWA_HEADS * LANE), seg(O_SQS, SWA_HEADS * LANE)
    for h in range(SWA_HEADS):
        q = sq[:, h * LANE:(h + 1) * LANE] * cw + sqs[:, h * LANE:(h + 1) * LANE] * sw
        qw_ref[0, h] = (q * (SWA_SCALE * LOG2E)).T.astype(CDT)
    sk, sks = seg(O_SK, SWA_KV_HEADS * LANE), seg(O_SKS, SWA_KV_HEADS * LANE)
    sv = seg(O_SV, SWA_KV_HEADS * LANE)
    for h in range(SWA_KV_HEADS):
        kw_ref[0, h] = (sk[:, h * LANE:(h + 1) * LANE] * cw + sks[:, h * LANE:(h + 1) * LANE] * sw).astype(CDT)
        vw_ref[0, h] = (sv[:, h * LANE:(h + 1) * LANE] + ones_lane).T.astype(CDT)
    s5 = seg(O_S5, S5_W)
    nchunk = s5_scr.shape[1] // S5_T
    for half in range(S5_W // LANE):
        s5_scr[half] = s5[:, half * LANE:(half + 1) * LANE]
        for sig in range(S5_T):
            s5_ref[half, 0, :, sig * LANE:(sig + 1) * LANE] = (
                s5_scr[half, pl.ds(sig, nchunk, stride=S5_T), :].astype(CDT))
    hy_ref[0] = seg(O_HY, (HY_ORDER + 1) * HY_W).astype(CDT)


def _inproj(x, mod, mod_row, gpre, wp, rope):
    b, l, d = x.shape
    tm = min(512, l)
    cm, sm, cw, sw = rope
    head = lambda n: jax.ShapeDtypeStruct((b, n, l, LANE), CDT)
    head_spec = lambda n: pl.BlockSpec((1, n, tm, LANE), lambda i, j: (i, 0, j, 0))
    tab_spec = pl.BlockSpec((tm, LANE), lambda i, j: (j, 0))
    row_spec = lambda n: pl.BlockSpec((1, tm, n), lambda i, j: (i, j, 0))
    head_t = lambda n: jax.ShapeDtypeStruct((b, n, LANE, l), CDT)
    head_t_spec = lambda n: pl.BlockSpec((1, n, LANE, tm), lambda i, j: (i, 0, 0, j))
    return pl.pallas_call(
        _inproj_kernel,
        out_shape=(head_t(MLA_HEADS), head(MLA_HEADS), head_t(MLA_HEADS), head_t(SWA_HEADS), head(SWA_KV_HEADS),
                   head_t(SWA_KV_HEADS), jax.ShapeDtypeStruct((S5_W // LANE, b, l // S5_T, S5_T * LANE), CDT),
                   jax.ShapeDtypeStruct((b, l, (HY_ORDER + 1) * HY_W), CDT)),
        grid=(b, l // tm),
        in_specs=[row_spec(d),
                  pl.BlockSpec((1, N_MOD, d), lambda i, j: (mod_row(i), 0, 0)),
                  _resident((1, d)), _resident(wp['w_big'].shape),
                  _resident((1, LANE)), _resident((1, 2 * LANE)),
                  _resident(wp['w_ukv'].shape), _resident(wp['w_uq'].shape),
                  tab_spec, tab_spec, tab_spec, tab_spec],
        out_specs=(head_t_spec(MLA_HEADS), head_spec(MLA_HEADS), head_t_spec(MLA_HEADS), head_t_spec(SWA_HEADS),
                   head_spec(SWA_KV_HEADS), head_t_spec(SWA_KV_HEADS),
                   pl.BlockSpec((S5_W // LANE, 1, tm // S5_T, S5_T * LANE), lambda i, j: (0, i, j, 0)),
                   row_spec((HY_ORDER + 1) * HY_W)),
        scratch_shapes=[pltpu.VMEM((S5_W // LANE, tm, LANE), F32)],
        compiler_params=_cparams("parallel", "parallel"),
        name="premod_inproj",
    )(x, mod, gpre, wp['w_big'], wp['g_kv'], wp['g_q'], wp['w_ukv'], wp['w_uq'], cm, sm, cw, sw)


MLA_KEY_CHUNK = 512
MLA_VROWS = 80


def _mla_scores(qt_ref, k_refs, s_buf, m_buf):
    qt = qt_ref[0, 0]
    m, off = None, 0
    for k_ref in k_refs:
        n = k_ref.shape[2]
        kc = min(MLA_KEY_CHUNK, n)
        for c in range(n // kc):
            s = _dot(k_ref[0, 0, c * kc:(c + 1) * kc, :], qt)
            s_buf[off + c * kc:off + (c + 1) * kc, :] = s
            cmax = s.max(axis=0, keepdims=True)
            m = cmax if m is None else jnp.maximum(m, cmax)
        off += n
    m_buf[...] = m


def _mla_values(vt_refs, s_buf, m_buf, o_ref):
    m = m_buf[...]
    acc, off = None, 0
    for vt_ref in vt_refs:
        n = vt_ref.shape[3]
        kc = min(MLA_KEY_CHUNK, n)
        for c in range(n // kc):
            p = jnp.exp2(s_buf[off + c * kc:off + (c + 1) * kc, :] - m).astype(CDT)
            pv = _dot(vt_ref[0, 0, 0:MLA_VROWS, c * kc:(c + 1) * kc], p)
            acc = pv if acc is None else acc + pv
        off += n
    o = acc / acc[MLA_V:MLA_V + 1, :]
    o = jnp.concatenate([o, jnp.zeros((LANE - MLA_VROWS, o.shape[1]), F32)], axis=0)
    o_ref[0] = o.T.astype(CDT)


def _mla_kernel(*refs, n_src, ntile):
    qt_ref, o_ref = refs[0], refs[1 + 2 * n_src]
    k_refs = [refs[1 + 2 * i] for i in range(n_src)]
    vt_refs = [refs[2 + 2 * i] for i in range(n_src)]
    s_bufs = refs[2 + 2 * n_src:4 + 2 * n_src]
    m_bufs = refs[4 + 2 * n_src:6 + 2 * n_src]
    t = pl.program_id(2)
    odd = t % 2 == 1

    @pl.when(t == 0)
    def _():
        _mla_scores(qt_ref, k_refs, s_bufs[0], m_bufs[0])

    @pl.when((t > 0) & (t < ntile) & odd)
    def _():
        _mla_scores(qt_ref, k_refs, s_bufs[1], m_bufs[1])
        _mla_values(vt_refs, s_bufs[0], m_bufs[0], o_ref)

    @pl.when((t > 0) & (t < ntile) & jnp.logical_not(odd))
    def _():
        _mla_scores(qt_ref, k_refs, s_bufs[0], m_bufs[0])
        _mla_values(vt_refs, s_bufs[1], m_bufs[1], o_ref)

    @pl.when(t == ntile)
    def _():
        _mla_values(vt_refs, s_bufs[(ntile - 1) % 2], m_bufs[(ntile - 1) % 2], o_ref)


def _mla(qt, kvs):
    b, h, _, l = qt.shape
    tq = min(512, l)
    ntile = l // tq
    in_specs = [pl.BlockSpec((1, 1, LANE, tq), lambda i, j, t: (i, j, 0, jnp.minimum(t, ntile - 1)))]
    args = [qt]
    for k, vt in kvs:
        n = k.shape[2]
        in_specs += [pl.BlockSpec((1, 1, n, LANE), lambda i, j, t: (i, j, 0, 0)),
                     pl.BlockSpec((1, 1, LANE, n), lambda i, j, t: (i, j, 0, 0))]
        args += [k, vt]
    nk = sum(k.shape[2] for k, _ in kvs)
    return pl.pallas_call(
        functools.partial(_mla_kernel, n_src=len(kvs), ntile=ntile),
        out_shape=jax.ShapeDtypeStruct((b, l, h * LANE), CDT),
        grid=(b, h, ntile + 1),
        in_specs=in_specs,
        out_specs=pl.BlockSpec((1, tq, LANE), lambda i, j, t: (i, jnp.maximum(t - 1, 0), j)),
        scratch_shapes=[pltpu.VMEM((nk, tq), F32), pltpu.VMEM((nk, tq), F32),
                        pltpu.VMEM((1, tq), F32), pltpu.VMEM((1, tq), F32)],
        compiler_params=_cparams("parallel", "parallel", "arbitrary"),
        name="mla_attention",
    )(*args)


def _swa_kernel(*refs, band, tq, nblk):
    sink_ref, qt_ref = refs[0], refs[1]
    o_ref = refs[-1]
    t = pl.program_id(1)
    g = SWA_HEADS // SWA_KV_HEADS
    lane = lax.broadcasted_iota(jnp.int32, (1, g * tq), 1)
    qi = jnp.where(lane < tq, lane, lane - tq)
    jp = lax.broadcasted_iota(jnp.int32, (WINDOW, 1), 0)
    jc = lax.broadcasted_iota(jnp.int32, (tq, 1), 0)
    for kv in range(SWA_KV_HEADS):
        qt = jnp.concatenate([qt_ref[0, kv * g + i] for i in range(g)], axis=-1)
        snk = jnp.where(lane < tq, sink_ref[kv * g], sink_ref[kv * g + 1]) * LOG2E
        ss, vts = [], []
        if band:
            kp, kc, kn, vtp, vtc, vtn, kx, vtx = [r[0, kv] for r in refs[2:10]]
            s_p = jnp.where((jp >= qi) & (t > 0), _dot(kp, qt), NEG_INF)
            s_c = jnp.where(jnp.abs(qi - jc) <= WINDOW, _dot(kc, qt), NEG_INF)
            s_n = jnp.where((jp <= qi - (tq - WINDOW)) & (t < nblk - 1), _dot(kn, qt), NEG_INF)
            ss += [s_p, s_c, s_n]
            vts += [vtp, vtc, vtn]
        else:
            kx, vtx = [r[0, kv] for r in refs[2:4]]
        ss.append(_dot(kx, qt))
        vts.append(vtx)
        m = snk
        for s in ss:
            m = jnp.maximum(m, s.max(axis=0, keepdims=True))
        acc = None
        for s, vt in zip(ss, vts):
            pv = _dot(vt[0:MLA_VROWS, :], jnp.exp2(s - m).astype(CDT))
            acc = pv if acc is None else acc + pv
        o = acc / (acc[SWA_HD:SWA_HD + 1, :] + jnp.exp2(snk - m))
        o = jnp.concatenate([o, jnp.zeros((LANE - MLA_VROWS, g * tq), F32)], axis=0)
        for i in range(g):
            o_ref[0, :, (kv * g + i) * LANE:(kv * g + i + 1) * LANE] = o[:, i * tq:(i + 1) * tq].T.astype(CDT)


def _swa(sink, qt, k, vt, kx, vtx, band):
    b, _, _, l = qt.shape
    hk = SWA_KV_HEADS
    tq = min(256, l)
    nblk = l // tq
    r = tq // WINDOW
    nw = l // WINDOW
    in_specs = [pl.BlockSpec(memory_space=pltpu.SMEM),
                pl.BlockSpec((1, SWA_HEADS, LANE, tq), lambda i, t: (i, 0, 0, t))]
    args = [sink, qt]
    if band:
        prev_i = lambda t: jnp.maximum(t * r - 1, 0)
        next_i = lambda t: jnp.minimum((t + 1) * r, nw - 1)
        in_specs += [pl.BlockSpec((1, hk, WINDOW, LANE), lambda i, t: (i, 0, prev_i(t), 0)),
                     pl.BlockSpec((1, hk, tq, LANE), lambda i, t: (i, 0, t, 0)),
                     pl.BlockSpec((1, hk, WINDOW, LANE), lambda i, t: (i, 0, next_i(t), 0)),
                     pl.BlockSpec((1, hk, LANE, WINDOW), lambda i, t: (i, 0, 0, prev_i(t))),
                     pl.BlockSpec((1, hk, LANE, tq), lambda i, t: (i, 0, 0, t)),
                     pl.BlockSpec((1, hk, LANE, WINDOW), lambda i, t: (i, 0, 0, next_i(t)))]
        args += [k, k, k, vt, vt, vt]
    nx = kx.shape[2]
    in_specs += [pl.BlockSpec((1, hk, nx, LANE), lambda i, t: (i, 0, 0, 0)),
                 pl.BlockSpec((1, hk, LANE, nx), lambda i, t: (i, 0, 0, 0))]
    args += [kx, vtx]
    return pl.pallas_call(
        functools.partial(_swa_kernel, band=band, tq=tq, nblk=nblk),
        out_shape=jax.ShapeDtypeStruct((b, l, SWA_HEADS * LANE), CDT),
        grid=(b, nblk),
        in_specs=in_specs,
        out_specs=pl.BlockSpec((1, tq, SWA_HEADS * LANE), lambda i, t: (i, t, 0)),
        compiler_params=_cparams("parallel", "parallel"),
        name="swa_attention",
    )(*args)


def _shortconv_kernel(x_ref, w_ref, b_ref, o_ref):
    x = x_ref[0].astype(F32)
    l = x.shape[0]
    t = lax.broadcasted_iota(jnp.int32, (l, 1), 0)
    prev = jnp.where(t == 0, 0.0, pltpu.roll(x, 1, 0))
    nxt = jnp.where(t == l - 1, 0.0, pltpu.roll(x, l - 1, 0))
    y = b_ref[...] + prev * w_ref[0:1, :] + x * w_ref[1:2, :] + nxt * w_ref[2:3, :]
    o_ref[0] = y.T.astype(CDT)


def _shortconv_t(x, w, bias):
    b, l, c = x.shape
    tc = 256
    return pl.pallas_call(
        _shortconv_kernel,
        out_shape=jax.ShapeDtypeStruct((b, c, l), CDT),
        grid=(b, c // tc),
        in_specs=[pl.BlockSpec((1, l, tc), lambda i, j: (i, 0, j)),
                  pl.BlockSpec((SHORT_K, tc), lambda i, j: (0, j)),
                  pl.BlockSpec((1, tc), lambda i, j: (0, j))],
        out_specs=pl.BlockSpec((1, tc, l), lambda i, j: (i, j, 0)),
        compiler_params=_cparams("parallel", "parallel"),
        name="hyena_shortconv",
    )(x, w, bias)


def _swap(x):
    return jnp.concatenate([x[..., LANE:], x[..., :LANE]], axis=-1)


def _comb(pq, n):
    p, q = pq[:, :n], pq[:, n:]
    return jnp.concatenate([p[..., :LANE] - q[..., LANE:], p[..., LANE:] + q[..., :LANE]], axis=-1)


def _hyena_kernel(v_ref, g1_ref, g2_ref, f1_ref, c1_ref, tw_ref, f2_ref, f2c_ref, ka_ref, kb_ref, bias_ref,
                  o_ref, *, ct, na, kin):
    def load(ref):
        return jnp.concatenate([ref[0, 0], ref[0, 1]], axis=-1).astype(F32)

    z = load(v_ref)
    gates = (g1_ref, g2_ref)
    twa, twb = tw_ref[0], tw_ref[1]
    for o in range(HY_ORDER):
        a = _comb(_bdot(f1_ref[...], z.astype(CDT)), na)
        a = a * twa + _swap(a) * twb
        x = _dot(a.reshape(ct * na, 2 * LANE).astype(CDT), f2_ref[...])
        y = x * ka_ref[o].reshape(ct * na, 2 * LANE) + _swap(x) * kb_ref[o].reshape(ct * na, 2 * LANE)
        bq = _dot(y.astype(CDT), f2c_ref[...]).reshape(ct, na, 2 * LANE)
        bq = bq * twa - _swap(bq) * twb
        yt = _comb(_bdot(c1_ref[...], bq.astype(CDT)), kin)
        z = load(gates[o]) * (yt + bias_ref[o] * z)
    o_ref[0, 0] = z[..., :LANE].astype(CDT)
    o_ref[0, 1] = z[..., LANE:].astype(CDT)


def _hyena_tables(na, kin, ct):
    f1, tw, f2, f2c, c1 = _hyena_dft_np(na, kin)
    bc = lambda m: jnp.broadcast_to(jnp.asarray(m, F32).astype(CDT)[None], (ct,) + m.shape)
    return (bc(f1), bc(c1), jnp.asarray(tw, F32), jnp.asarray(f2, F32).astype(CDT),
            jnp.asarray(f2c, F32).astype(CDT))


def _hyena_dft_np(na, kin):
    n = na * LANE
    ka = np.arange(na)[:, None]
    a = np.arange(kin)[None, :]
    ang1 = -2.0 * np.pi * ((ka * a) % na) / na
    f1 = np.concatenate([np.cos(ang1), np.sin(ang1)], axis=0)
    c1 = np.concatenate([np.cos(ang1).T, -np.sin(ang1).T], axis=0)
    bb = np.arange(LANE)[None, :]
    angt = -2.0 * np.pi * ((ka * bb) % n) / n
    tr, ti = np.cos(angt), np.sin(angt)
    tw = np.stack([np.concatenate([tr, tr], 1), np.concatenate([-ti, ti], 1)])
    b2 = np.arange(LANE)
    ang2 = -2.0 * np.pi * ((b2[:, None] * b2[None, :]) % LANE) / LANE
    fr, fi = np.cos(ang2), np.sin(ang2)
    f2 = np.block([[fr, fi], [-fi, fr]])
    f2c = np.block([[fr, -fi], [fi, fr]])
    return f1, tw, f2, f2c, c1


def _hyena_filter_taps(l, hp):
    t = jnp.linspace(0.0, 1.0, l, dtype=F32)[None, :]
    bands = (HY_EMB - 1) // 2
    w = 2.0 * math.pi * jnp.arange(l, dtype=F32) / l
    fr = jnp.linspace(1e-4, bands - 1, bands, dtype=F32)
    ang = fr[:, None] * w[None, :]
    z = jnp.concatenate([t, jnp.cos(ang), -jnp.sin(ang)], axis=0)
    freq = hp['hy_f_freq']
    hi = lax.Precision.HIGHEST
    h = jnp.sin(freq[0][:, None] * (jnp.dot(hp['hy_f_w1'].T, z, precision=hi) + hp['hy_f_b1'][:, None]))
    h = jnp.sin(freq[1][:, None] * (jnp.dot(hp['hy_f_w2'].T, h, precision=hi) + hp['hy_f_b2'][:, None]))
    h = jnp.dot(hp['hy_f_w3'].T, h, precision=hi).reshape(HY_ORDER, 2, HY_W, l)
    deltas = jnp.abs(jnp.linspace(math.log(HY_DECAY_TARGET) / HY_SLOW_PCT,
                                  math.log(HY_DECAY_TARGET) / HY_FAST_PCT, HY_W, dtype=F32))
    h = h * jnp.exp(-t * deltas[:, None])
    lag0 = (jnp.arange(l) > 0).astype(F32)
    return h[:, 0].reshape(HY_ORDER * HY_W, l), (h[:, 1] * lag0).reshape(HY_ORDER * HY_W, l)


def _split(x):
    hi = x.astype(CDT)
    return hi, (x - hi.astype(F32)).astype(CDT)


def _hyfilt_kernel(hf_ref, hb_ref, f1h_ref, f1l_ref, tw_ref, f2h_ref, f2l_ref, ka_ref, kb_ref, *, rt, na, inv_n):
    xh, xl = _split(jnp.concatenate([hf_ref[...], hb_ref[...]], axis=-1))
    f1h, f1l = f1h_ref[...], f1l_ref[...]
    pq = _bdot(f1h, xh) + _bdot(f1l, xh) + _bdot(f1h, xl)
    p, q = pq[:, :na], pq[:, na:]
    a = jnp.concatenate([jnp.concatenate([p[..., :LANE], q[..., :LANE]], -1),
                         jnp.concatenate([p[..., LANE:], q[..., LANE:]], -1)], axis=0)
    a = a * tw_ref[0] + _swap(a) * tw_ref[1]
    ah, al = _split(a.reshape(2 * rt * na, 2 * LANE))
    x = (_dot(ah, f2h_ref[...]) + _dot(al, f2h_ref[...]) + _dot(ah, f2l_ref[...])).reshape(2 * rt, na, 2 * LANE)
    xf, xb = x[:rt], x[rt:]
    kr = (xf[..., :LANE] + xb[..., :LANE]) * inv_n
    ki = (xf[..., LANE:] - xb[..., LANE:]) * inv_n
    ka_ref[...] = jnp.concatenate([kr, kr], -1)
    kb_ref[...] = jnp.concatenate([-ki, ki], -1)


def _hyena_filter_spectrum(hf, hb, na, kin):
    rows, l = hf.shape
    lp = kin * LANE
    if lp != l:
        hf, hb = [jnp.pad(h, ((0, 0), (0, lp - l))) for h in (hf, hb)]
    hf, hb = hf.reshape(rows, kin, LANE), hb.reshape(rows, kin, LANE)
    rt = 16
    f1, tw, f2 = _hyena_dft_np(na, kin)[:3]
    f1 = jnp.broadcast_to(jnp.asarray(f1, F32)[None], (rt,) + f1.shape)
    f1h, f1l = _split(f1)
    f2h, f2l = _split(jnp.asarray(f2, F32))
    tw = jnp.asarray(tw, F32)
    ka, kb = pl.pallas_call(
        functools.partial(_hyfilt_kernel, rt=rt, na=na, inv_n=1.0 / (na * LANE)),
        out_shape=(jax.ShapeDtypeStruct((rows, na, 2 * LANE), F32),) * 2,
        grid=(rows // rt,),
        in_specs=[pl.BlockSpec((rt, kin, LANE), lambda i: (i, 0, 0)),
                  pl.BlockSpec((rt, kin, LANE), lambda i: (i, 0, 0)),
                  _resident(f1h.shape), _resident(f1l.shape), _resident(tw.shape),
                  _resident(f2h.shape), _resident(f2l.shape)],
        out_specs=(pl.BlockSpec((rt, na, 2 * LANE), lambda i: (i, 0, 0)),) * 2,
        compiler_params=_cparams("parallel"),
        name="hyena_filter_spectrum",
    )(hf, hb, f1h, f1l, tw, f2h, f2l)
    shape = (HY_ORDER, HY_W, na, 2 * LANE)
    return ka.reshape(shape), kb.reshape(shape)


def _hyena(hy, hp, l_true):
    b, l, _ = hy.shape
    ut = _shortconv_t(hy, hp['hy_conv_w'], hp['hy_conv_b'].reshape(1, -1))
    kin = max(l // LANE, 16)
    na = 2 * kin
    lp = kin * LANE
    if lp != l:
        ut = jnp.pad(ut, ((0, 0), (0, 0), (0, lp - l)))
    ut = ut.reshape(b // 2, 2, (HY_ORDER + 1) * HY_W, kin, LANE)
    ct = 16
    nc = HY_W // ct
    f1, c1, tw, f2, f2c = _hyena_tables(na, kin, ct)
    ka, kb = hp['ka'], hp['kb']
    bias = jnp.broadcast_to(hp['hy_bias'].reshape(HY_ORDER, HY_W, 1, 1), (HY_ORDER, HY_W, 1, 2 * LANE))
    blk = lambda off: pl.BlockSpec((1, 2, ct, kin, LANE), lambda c, p: (p, 0, c + off * nc, 0, 0))
    out = pl.pallas_call(
        functools.partial(_hyena_kernel, ct=ct, na=na, kin=kin),
        out_shape=jax.ShapeDtypeStruct((b // 2, 2, HY_W, kin, LANE), CDT),
        grid=(nc, b // 2),
        in_specs=[blk(0), blk(1), blk(2),
                  _resident(f1.shape), _resident(c1.shape), _resident(tw.shape),
                  _resident(f2.shape), _resident(f2c.shape),
                  pl.BlockSpec((HY_ORDER, ct, na, 2 * LANE), lambda c, p: (0, c, 0, 0)),
                  pl.BlockSpec((HY_ORDER, ct, na, 2 * LANE), lambda c, p: (0, c, 0, 0)),
                  pl.BlockSpec((HY_ORDER, ct, 1, 2 * LANE), lambda c, p: (0, c, 0, 0))],
        out_specs=pl.BlockSpec((1, 2, ct, kin, LANE), lambda c, p: (p, 0, c, 0, 0)),
        compiler_params=_cparams("parallel", "arbitrary"),
        name="hyena_longconv",
    )(ut, ut, ut, f1, c1, tw, f2, f2c, ka, kb, bias)
    out = out.reshape(b, HY_W, lp)[:, :, :l]
    return jnp.swapaxes(out, 1, 2)


def _s5_kernel(u_ref, psel_ref, pselt_ref, tloc_ref, wx_ref, wout_ref, d_ref, h0_ref, y_ref, hfin_ref,
               x_scr, h_scr, up_scr, *, nchunk, nb):
    w = S5_T * S5_GC
    gb = max(1, min(nb, S5_BLOCK_ROWS // nchunk))
    for b0 in range(0, nb, gb):
        rs = slice(b0 * nchunk, (b0 + gb) * nchunk)
        up = _dot(u_ref[0, rs, :], psel_ref[0]).astype(CDT)
        up_scr[rs, :] = up
        xb = _dot(up, wx_ref[0])
        for i in range(gb):
            for k in range(4):
                x_scr[k, pl.ds(b0 + i, nchunk, stride=nb), :] = xb[i * nchunk:(i + 1) * nchunk,
                                                                   k * LANE:(k + 1) * LANE]
    d = d_ref[0]
    dfr, dfi, dbr, dbi = d[0:1], d[1:2], d[2:3], d[3:4]

    def body(j, carry):
        hr, hi, gr, gi = carry
        rf = pl.multiple_of(j * nb, nb)
        rb = pl.multiple_of((nchunk - 1 - j) * nb, nb)
        h_scr[0, pl.ds(rf, nb), :] = hr
        h_scr[1, pl.ds(rf, nb), :] = hi
        h_scr[2, pl.ds(rb, nb), :] = gr
        h_scr[3, pl.ds(rb, nb), :] = gi
        xr = x_scr[0, pl.ds(rf, nb), :]
        xi = x_scr[1, pl.ds(rf, nb), :]
        yr = x_scr[2, pl.ds(rb, nb), :]
        yi = x_scr[3, pl.ds(rb, nb), :]
        return (dfr * hr - dfi * hi + xr, dfr * hi + dfi * hr + xi,
                dbr * gr - dbi * gi + yr, dbr * gi + dbi * gr + yi)

    fin = lax.fori_loop(0, nchunk, body, tuple(h0_ref[0, k] for k in range(4)))
    for k in range(4):
        hfin_ref[0, k] = fin[k]
    r = pl.program_id(1)
    for b0 in range(0, nb, gb):
        rs = slice(b0 * nchunk, (b0 + gb) * nchunk)
        hs = jnp.concatenate(
            [jnp.concatenate([h_scr[k, pl.ds(b0 + i, nchunk, stride=nb), :] for k in range(4)], axis=-1)
             for i in range(gb)], axis=0).astype(CDT)
        y0 = _dot(up_scr[rs, :w], tloc_ref[0]) + _dot(hs, wout_ref[0, :, :w])
        y1 = _dot(up_scr[rs, w:], tloc_ref[1]) + _dot(hs, wout_ref[0, :, w:])
        contrib = _dot(jnp.concatenate([y0, y1], axis=-1).astype(CDT), pselt_ref[0]).astype(CDT)

        @pl.when(r == 0)
        def _():
            y_ref[0, rs, :] = contrib

        @pl.when(r > 0)
        def _():
            y_ref[0, rs, :] = y_ref[0, rs, :] + contrib


S5_HALF_GROUPS = LANE // S5_GC
S5_HALF_PAIRS = S5_HALF_GROUPS // 2


def _s5_select():
    ri = jnp.arange(S5_T * LANE)
    r_sig, r_grp, r_ch = ri // LANE, (ri % LANE) // S5_GC, ri % S5_GC
    ci = jnp.arange(2 * S5_T * S5_GC)
    c_grp, c_sig, c_ch = ci // (S5_T * S5_GC), (ci % (S5_T * S5_GC)) // S5_GC, ci % S5_GC
    same = (r_sig[:, None] == c_sig[None, :]) & (r_ch[:, None] == c_ch[None, :])
    sel = jnp.stack([(same & (r_grp[:, None] == 2 * q + c_grp[None, :])) for q in range(S5_HALF_PAIRS)])
    sel = sel.astype(CDT)
    return sel, jnp.swapaxes(sel, 1, 2)


def _s5(u, ops, h0):
    nhalf, b, nchunk, wide = u.shape
    w = S5_T * S5_GC
    rows = nchunk * b
    uh = u.reshape(nhalf, rows, wide)
    psel, pselt = _s5_select()
    pair = lambda h, r: h * S5_HALF_PAIRS + r
    y, hfin = pl.pallas_call(
        functools.partial(_s5_kernel, nchunk=nchunk, nb=b),
        out_shape=(jax.ShapeDtypeStruct((nhalf, rows, wide), CDT),
                   jax.ShapeDtypeStruct((S5_GROUPS // 2, 4, b, LANE), F32)),
        grid=(nhalf, S5_HALF_PAIRS),
        in_specs=[pl.BlockSpec((1, rows, wide), lambda h, r: (h, 0, 0), pipeline_mode=pl.Buffered(1)),
                  pl.BlockSpec((1, wide, 2 * w), lambda h, r: (r, 0, 0)),
                  pl.BlockSpec((1, 2 * w, wide), lambda h, r: (r, 0, 0)),
                  pl.BlockSpec((2, w, w), lambda h, r: (pair(h, r), 0, 0)),
                  pl.BlockSpec((1, 2 * w, 2 * w), lambda h, r: (pair(h, r), 0, 0)),
                  pl.BlockSpec((1, 2 * w, 2 * w), lambda h, r: (pair(h, r), 0, 0)),
                  pl.BlockSpec((1, 4, LANE), lambda h, r: (pair(h, r), 0, 0)),
                  pl.BlockSpec((1, 4, b, LANE), lambda h, r: (pair(h, r), 0, 0, 0))],
        out_specs=(pl.BlockSpec((1, rows, wide), lambda h, r: (h, 0, 0)),
                   pl.BlockSpec((1, 4, b, LANE), lambda h, r: (pair(h, r), 0, 0, 0))),
        scratch_shapes=[pltpu.VMEM((4, rows, LANE), F32), pltpu.VMEM((4, rows, LANE), F32),
                        pltpu.VMEM((rows, 2 * w), CDT)],
        compiler_params=_cparams("parallel", "arbitrary"),
        name="s5_chunked",
    )(uh, psel, pselt, ops['tloc'], ops['wx'], ops['wout'], ops['d16'], h0)
    return y.reshape(nhalf, b, nchunk, wide), hfin


def _s5_operators(p):
    g, pp, gc, t = S5_GROUPS, S5_P, S5_GC, S5_T
    npair = g // 2
    n = jnp.arange(t + 1, dtype=F32)[:, None, None]
    tops, wxs, wouts, d16 = [], [], [], []
    sig = jnp.arange(t)
    for d in range(2):
        a_re, a_im = p['s5_a_re'][d], p['s5_a_im'][d]
        dt = jnp.exp(p['s5_log_dt'][d])[:, None]
        mag1 = jnp.exp(dt * a_re)
        ab_re, ab_im = mag1 * jnp.cos(dt * a_im), mag1 * jnp.sin(dt * a_im)
        den = a_re * a_re + a_im * a_im
        f_re = ((ab_re - 1.0) * a_re + ab_im * a_im) / den
        f_im = (ab_im * a_re - (ab_re - 1.0) * a_im) / den
        mag = jnp.exp(n * (dt * a_re)[None])
        pr, pi = mag * jnp.cos(n * (dt * a_im)[None]), mag * jnp.sin(n * (dt * a_im)[None])
        b_re, b_im = p['s5_b_re'][d], p['s5_b_im'][d]
        bt_re = f_re[..., None] * b_re - f_im[..., None] * b_im
        bt_im = f_re[..., None] * b_im + f_im[..., None] * b_re
        c_re, c_im = p['s5_c_re'][d], p['s5_c_im'][d]
        ca_re = c_re[None] * pr[:, :, None, :] - c_im[None] * pi[:, :, None, :]
        ca_im = c_re[None] * pi[:, :, None, :] + c_im[None] * pr[:, :, None, :]
        hi = lax.Precision.HIGHEST
        lag_re = ca_re[:t] if d == 0 else ca_re[:t][::-1]
        lag_im = ca_im[:t] if d == 0 else ca_im[:t][::-1]
        tops.append(jnp.einsum('ngcp,gpk->gknc', lag_re, bt_re, precision=hi)
                    - jnp.einsum('ngcp,gpk->gknc', lag_im, bt_im, precision=hi))
        pwr, pwi = (pr[:t][::-1], pi[:t][::-1]) if d == 0 else (pr[:t], pi[:t])
        pwr, pwi = pwr.transpose(1, 0, 2)[:, :, None, :], pwi.transpose(1, 0, 2)[:, :, None, :]
        btr, bti = bt_re.transpose(0, 2, 1)[:, None], bt_im.transpose(0, 2, 1)[:, None]
        wxs.append(((pwr * btr - pwi * bti).reshape(g, t * gc, pp), (pwr * bti + pwi * btr).reshape(g, t * gc, pp)))
        out_re = ca_re[1:t + 1] if d == 0 else ca_re[1:t + 1][::-1]
        out_im = ca_im[1:t + 1] if d == 0 else ca_im[1:t + 1][::-1]
        wouts.append((out_re.transpose(1, 3, 0, 2).reshape(g, pp, t * gc),
                      -out_im.transpose(1, 3, 0, 2).reshape(g, pp, t * gc)))
        d16.append((pr[t], pi[t]))
    skip = jnp.eye(gc, dtype=F32)[None, :, None, :] * p['s5_d'].reshape(g, 1, 1, gc)
    kf, kb = tops
    kcomb = jnp.concatenate([kb[:, :, :t - 1], kf[:, :, 0:1] + kb[:, :, t - 1:t] + skip, kf[:, :, 1:]], axis=2)
    kcomb = kcomb.reshape(g, gc, (2 * t - 1) * gc)
    tloc = jnp.stack([kcomb[:, :, (t - 1 - s) * gc:(t - 1 - s) * gc + t * gc] for s in range(t)], axis=1)
    tloc = tloc.reshape(g, t * gc, t * gc)

    def pair_cols(m):
        m = m.reshape(npair, 2, m.shape[1], m.shape[2])
        z = jnp.zeros_like(m[:, 0])
        return jnp.concatenate([jnp.concatenate([m[:, 0], z], -1), jnp.concatenate([z, m[:, 1]], -1)], -2)

    wx = jnp.concatenate([pair_cols(wxs[0][0]), pair_cols(wxs[0][1]), pair_cols(wxs[1][0]), pair_cols(wxs[1][1])],
                         axis=-1)
    wout = jnp.concatenate([pair_cols(wouts[0][0]), pair_cols(wouts[0][1]), pair_cols(wouts[1][0]),
                            pair_cols(wouts[1][1])], axis=-2)
    dd = jnp.stack([d16[0][0], d16[0][1], d16[1][0], d16[1][1]], axis=0)
    dd = dd.reshape(4, npair, 2 * pp).transpose(1, 0, 2)
    return dict(tloc=tloc.astype(CDT), wx=wx.astype(CDT), wout=wout.astype(CDT), d16=dd)


def _merge_kernel(x_ref, mod_ref, gpre_ref, gpost_ref, omla_ref, ohy_ref, oswa_ref, ys5_ref,
                  wgate_ref, bgate_ref, wmla_ref, why_ref, wswa_ref, ws5_ref, gluw_ref, glub_ref, wout_ref, o_ref,
                  y_scr):
    x = x_ref[0]
    mod = mod_ref[0]
    u = _pre_mod(x, gpre_ref[...], mod, 1).astype(CDT)
    nchunk = y_scr.shape[1] // S5_T
    for sig in range(S5_T):
        for half in range(S5_W // LANE):
            y_scr[half, pl.ds(sig, nchunk, stride=S5_T), :] = (
                ys5_ref[half, 0, :, sig * LANE:(sig + 1) * LANE].astype(F32))
    y = jnp.concatenate([y_scr[half] for half in range(S5_W // LANE)], axis=-1)
    g = 0.5 * y * (1.0 + jnp.tanh(math.sqrt(2.0 / math.pi) * (y + 0.044715 * (y * y * y))))
    o_s5 = (g * _sigmoid(_dot(g.astype(CDT), gluw_ref[...]) + glub_ref[...])).astype(CDT)
    outs = (omla_ref[0], ohy_ref[0], oswa_ref[0], o_s5)
    wbr = (wmla_ref, why_ref, wswa_ref, ws5_ref)
    m = jnp.zeros(x.shape, F32)
    for i in range(4):
        gate = _sigmoid(_dot(u, wgate_ref[0, i]) + bgate_ref[:, i * D_MODEL:(i + 1) * D_MODEL])
        m = m + gate * _dot(outs[i], wbr[i][...])
    f = _dot(m.astype(CDT), wout_ref[0])
    o_ref[0] = x + mod[5:6, :] * _rms(f, gpost_ref[...])


def _merge(x, mod, mod_row, gpre, gpost, o_mla, o_hy, o_swa, y_s5, wp, layer):
    b, l, d = x.shape
    tm = min(512, l)
    row_spec = lambda n: pl.BlockSpec((1, tm, n), lambda i, j: (i, j, 0))
    names = ('w_gate', 'b_gate', 'w_br_mla', 'w_br_hy', 'w_br_swa', 'w_br_s5', 'glu_w', 'glu_b', 'w_out')
    stacked = ('w_gate', 'w_out')
    wspec = lambda k: _layer_resident(wp[k].shape, layer) if k in stacked else _resident(wp[k].shape)
    return pl.pallas_call(
        _merge_kernel,
        out_shape=jax.ShapeDtypeStruct(x.shape, F32),
        grid=(b, l // tm),
        in_specs=[row_spec(d), pl.BlockSpec((1, N_MOD, d), lambda i, j: (mod_row(i), 0, 0)),
                  _resident((1, d)), _resident((1, d)),
                  row_spec(o_mla.shape[-1]), row_spec(o_hy.shape[-1]), row_spec(o_swa.shape[-1]),
                  pl.BlockSpec((S5_W // LANE, 1, tm // S5_T, S5_T * LANE), lambda i, j: (0, i, j, 0))]
        + [wspec(k) for k in names],
        out_specs=row_spec(d),
        scratch_shapes=[pltpu.VMEM((S5_W // LANE, tm, LANE), F32)],
        compiler_params=_cparams("parallel", "parallel"),
        name="merge_out",
    )(x, mod, gpre, gpost, o_mla, o_hy, o_swa, y_s5, *[wp[k] for k in names])


def _rot_partner(w, half):
    return jnp.concatenate([-w[:, half:], w[:, :half]], axis=1)


def _pad_cols(w, n):
    return jnp.pad(w, ((0, 0), (0, n - w.shape[1])))


def _pad_rows(w, n):
    return jnp.pad(w, ((0, n - w.shape[0]), (0, 0)))


def _prep_inproj(w_in, w_ukv, w_uq, g_kv, g_q):
    d = w_in.shape[0]
    zeros = lambda n: jnp.zeros((d, n), F32)
    krope = w_in[:, I_KROPE:I_KROPE + MLA_ROPE]
    kr = jnp.concatenate([zeros(MLA_NOPE), krope, zeros(LANE - MLA_NOPE - MLA_ROPE)], 1)
    krs = jnp.concatenate([zeros(MLA_NOPE), _rot_partner(krope, MLA_ROPE // 2), zeros(LANE - MLA_NOPE - MLA_ROPE)], 1)
    ckv = w_in[:, I_CKV:I_CKV + MLA_KV_LORA]
    cq = _pad_cols(w_in[:, I_CQ:I_CQ + MLA_Q_LORA], 2 * LANE)

    def heads(w, nh, partner):
        cols = []
        for h in range(nh):
            wh = w[:, h * SWA_HD:(h + 1) * SWA_HD]
            if partner:
                wh = _rot_partner(wh, SWA_HD // 2)
            cols.append(_pad_cols(wh, LANE))
        return jnp.concatenate(cols, 1)

    swq = w_in[:, I_SWQ:I_SWQ + SWA_HEADS * SWA_HD]
    swk = w_in[:, I_SWK:I_SWK + SWA_KV_HEADS * SWA_HD]
    swv = w_in[:, I_SWV:I_SWV + SWA_KV_HEADS * SWA_HD]
    w_big = jnp.concatenate([kr, krs, ckv, cq,
                             heads(swq, SWA_HEADS, False), heads(swq, SWA_HEADS, True),
                             heads(swk, SWA_KV_HEADS, False), heads(swk, SWA_KV_HEADS, True),
                             heads(swv, SWA_KV_HEADS, False),
                             w_in[:, I_S5:I_S5 + S5_W], w_in[:, I_HY:I_HY + (HY_ORDER + 1) * HY_W]], axis=1)
    assert w_big.shape[1] == N_BIG
    kvw = w_ukv.reshape(MLA_KV_LORA, MLA_HEADS, MLA_NOPE + MLA_V)
    kslots = [_pad_cols(kvw[:, h, :MLA_NOPE], LANE) for h in range(MLA_HEADS)]
    vslots = [_pad_cols(kvw[:, h, MLA_NOPE:], LANE) for h in range(MLA_HEADS)]
    w_ukv_p = jnp.concatenate(kslots + vslots, axis=1)
    qw = w_uq.reshape(MLA_Q_LORA, MLA_HEADS, MLA_NOPE + MLA_ROPE)
    qslots = [_pad_cols(qw[:, h], LANE) for h in range(MLA_HEADS)]
    pslots = [_pad_cols(jnp.concatenate([jnp.zeros((MLA_Q_LORA, MLA_NOPE), F32),
                                         _rot_partner(qw[:, h, MLA_NOPE:], MLA_ROPE // 2)], 1), LANE)
              for h in range(MLA_HEADS)]
    w_uq_p = _pad_rows(jnp.concatenate(qslots + pslots, axis=1), 2 * LANE)
    return dict(w_big=w_big.astype(CDT), w_ukv=w_ukv_p.astype(CDT), w_uq=w_uq_p.astype(CDT),
                g_kv=g_kv.reshape(1, -1), g_q=_pad_cols(g_q.reshape(1, -1), 2 * LANE))


def _pad_head_rows(w, nh, hd):
    w = w.reshape(nh, hd, w.shape[-1])
    return jnp.pad(w, ((0, 0), (0, LANE - hd), (0, 0))).reshape(nh * LANE, -1)


def _rope_tables(n_tokens, use_rope):
    ones = jnp.ones((n_tokens, LANE), F32)
    zeros = jnp.zeros((n_tokens, LANE), F32)
    if not use_rope:
        return ones, zeros, ones, zeros

    def axial(rot_dim):
        rows = n_tokens // GRID_W
        r = jnp.repeat(jnp.arange(rows, dtype=F32), GRID_W)
        col = jnp.tile(jnp.arange(GRID_W, dtype=F32), rows)
        n_freq = rot_dim // 4
        freqs = ROPE_THETA ** (-jnp.arange(n_freq, dtype=F32) / n_freq)
        ang = jnp.concatenate([r[:, None] * freqs, col[:, None] * freqs], axis=-1)
        return jnp.cos(ang), jnp.sin(ang)

    c, s = axial(MLA_ROPE)
    pad = LANE - MLA_NOPE - MLA_ROPE
    cm = jnp.concatenate([ones[:, :MLA_NOPE], c, c, ones[:, :pad]], 1)
    sm = jnp.concatenate([zeros[:, :MLA_NOPE], s, s, zeros[:, :pad]], 1)
    c, s = axial(SWA_HD)
    cw = jnp.concatenate([c, c, ones[:, :LANE - SWA_HD]], 1)
    sw = jnp.concatenate([s, s, zeros[:, :LANE - SWA_HD]], 1)
    return cm, sm, cw, sw


def _hyena_params(p, l):
    kin = max(l // LANE, 16)
    na = 2 * kin
    hf, hb = _hyena_filter_taps(l, p)
    ka, kb = _hyena_filter_spectrum(hf, hb, na, kin)
    return dict(hy_conv_w=p['hy_conv_w'], hy_conv_b=p['hy_conv_b'], hy_bias=p['hy_bias'], ka=ka, kb=kb)


def kernel(x, c, ctx, c_ctx, w_ada, b_ada, norm_pre, norm_post, ffn1_up, ffn1_down, ffn2_up, ffn2_down,
           w_in, mla_q_norm, mla_kv_norm, mla_w_uq, mla_w_ukv, hy_conv_w, hy_conv_b, hy_f_w1, hy_f_b1,
           hy_f_freq, hy_f_w2, hy_f_b2, hy_f_w3, hy_bias, swa_sink, s5_a_re, s5_a_im, s5_log_dt,
           s5_b_re, s5_b_im, s5_c_re, s5_c_im, s5_d, s5_glu_w, s5_glu_b, w_gate, b_gate,
           w_br_mla, w_br_hy, w_br_swa, w_br_s5, w_out):
    stacked = dict(w_ada=w_ada, b_ada=b_ada, norm_pre=norm_pre, norm_post=norm_post,
                   ffn1_up=ffn1_up, ffn1_down=ffn1_down, ffn2_up=ffn2_up, ffn2_down=ffn2_down,
                   w_in=w_in, mla_q_norm=mla_q_norm, mla_kv_norm=mla_kv_norm, mla_w_uq=mla_w_uq,
                   mla_w_ukv=mla_w_ukv, hy_conv_w=hy_conv_w, hy_conv_b=hy_conv_b, hy_f_w1=hy_f_w1,
                   hy_f_b1=hy_f_b1, hy_f_freq=hy_f_freq, hy_f_w2=hy_f_w2, hy_f_b2=hy_f_b2, hy_f_w3=hy_f_w3,
                   hy_bias=hy_bias, swa_sink=swa_sink, s5_a_re=s5_a_re, s5_a_im=s5_a_im,
                   s5_log_dt=s5_log_dt, s5_b_re=s5_b_re, s5_b_im=s5_b_im, s5_c_re=s5_c_re,
                   s5_c_im=s5_c_im, s5_d=s5_d, s5_glu_w=s5_glu_w, s5_glu_b=s5_glu_b,
                   w_gate=w_gate, b_gate=b_gate, w_br_mla=w_br_mla, w_br_hy=w_br_hy,
                   w_br_swa=w_br_swa, w_br_s5=w_br_s5, w_out=w_out)
    depth = w_ada.shape[0]
    nb, seq, d = x.shape
    nctx = ctx.shape[1]
    assert nb % 2 == 0 and seq % 256 == 0 and nctx % 256 == 0

    rows = -(-(nb + 1) // 8) * 8
    cvec = jnp.zeros((rows, d), F32).at[:nb].set(c).at[nb].set(c_ctx)
    mods = _modulation(cvec, w_ada, b_ada)
    lat_row = lambda i: i
    ctx_row = lambda i: nb

    rope_l = _rope_tables(seq, True)
    rope_c = _rope_tables(nctx, False)
    h0_zero = jnp.zeros((S5_GROUPS // 2, 4, nb, LANE), F32)

    f1u, f1d, f2u, f2d = [w.astype(CDT) for w in (ffn1_up, ffn1_down, ffn2_up, ffn2_down)]
    w_gate_all, w_out_all = w_gate.astype(CDT), w_out.astype(CDT)

    xl, xc = x, ctx
    for l in range(depth):
        p = {name: arr[l] for name, arr in stacked.items()}
        mod = mods[l]
        ctx_out = l < depth - 1
        gpre = [p['norm_pre'][i].reshape(1, d) for i in range(N_SUB)]
        gpost = [p['norm_post'][i].reshape(1, d) for i in range(N_SUB)]
        wp_in = _prep_inproj(p['w_in'], p['mla_w_ukv'], p['mla_w_uq'], p['mla_kv_norm'], p['mla_q_norm'])
        wp_mg = dict(
            w_gate=w_gate_all,
            b_gate=p['b_gate'].reshape(1, -1),
            w_br_mla=_pad_head_rows(p['w_br_mla'], MLA_HEADS, MLA_V).astype(CDT),
            w_br_hy=p['w_br_hy'].astype(CDT),
            w_br_swa=_pad_head_rows(p['w_br_swa'], SWA_HEADS, SWA_HD).astype(CDT),
            w_br_s5=p['w_br_s5'].astype(CDT),
            glu_w=p['s5_glu_w'].astype(CDT), glu_b=p['s5_glu_b'].reshape(1, -1),
            w_out=w_out_all)
        s5_ops = _s5_operators(p)

        xl = _ffn(xl, mod, lat_row, gpre[0], gpost[0], f1u, f1d, l, 0)
        xc = _ffn(xc, mod, ctx_row, gpre[0], gpost[0], f1u, f1d, l, 0)

        qm_c, km_c, vm_c, qw_c, kw_c, vw_c, s5u_c, hy_c = _inproj(xc, mod, ctx_row, gpre[1], wp_in, rope_c)
        qm_l, km_l, vm_l, qw_l, kw_l, vw_l, s5u_l, hy_l = _inproj(xl, mod, lat_row, gpre[1], wp_in, rope_l)

        ys5_c, h_ctx = _s5(s5u_c, s5_ops, h0_zero)
        ys5_l, _ = _s5(s5u_l, s5_ops, h_ctx)
        o_mla = _mla(qm_l, [(km_l, vm_l), (km_c, vm_c)])
        o_swa = _swa(p['swa_sink'], qw_l, kw_l, vw_l, kw_c, vw_c, True)
        o_hy = _hyena(hy_l, _hyena_params(p, seq), seq)
        xl = _merge(xl, mod, lat_row, gpre[1], gpost[1], o_mla, o_hy, o_swa, ys5_l, wp_mg, l)
        xl = _ffn(xl, mod, lat_row, gpre[2], gpost[2], f2u, f2d, l, 2)
        if ctx_out:
            o_mla_c = _mla(qm_c, [(km_c, vm_c)])
            o_swa_c = _swa(p['swa_sink'], qw_c, kw_c, vw_c, kw_c, vw_c, False)
            o_hy_c = _hyena(hy_c, _hyena_params(p, nctx), nctx)
            xc = _merge(xc, mod, ctx_row, gpre[1], gpost[1], o_mla_c, o_hy_c, o_swa_c, ys5_c, wp_mg, l)
            xc = _ffn(xc, mod, ctx_row, gpre[2], gpost[2], f2u, f2d, l, 2)
    return xl
```

```python
import functools
import math

import numpy as np
import jax
import jax.numpy as jnp
from jax import lax
from jax.experimental import pallas as pl
from jax.experimental.pallas import tpu as pltpu

F32 = jnp.float32
CDT = jnp.bfloat16

D_MODEL = 1024
D_FF = 2816
N_SUB = 3
N_MOD = 3 * N_SUB
MACARON_W = 0.5
ROPE_THETA = 10000.0
GRID_W = 64
EPS = 1e-6
NEG_INF = -1e30

MLA_HEADS = 4
MLA_NOPE = 64
MLA_ROPE = 32
MLA_V = 64
MLA_Q_LORA = 192
MLA_KV_LORA = 128
MLA_SCALE = (MLA_NOPE + MLA_ROPE) ** -0.5
LOG2E = math.log2(math.e)

HY_W = 256
HY_ORDER = 2
HY_EMB = 33
HY_DECAY_TARGET = 1e-2
HY_FAST_PCT = 0.3
HY_SLOW_PCT = 1.5
SHORT_K = 3

SWA_HEADS = 4
SWA_KV_HEADS = 2
SWA_HD = 64
WINDOW = 128
SWA_SCALE = SWA_HD ** -0.5

S5_W = 256
S5_GC = 16
S5_GROUPS = S5_W // S5_GC
S5_P = 64
S5_BLOCK_ROWS = 512
S5_T = 16

LANE = 128
MXU = 256
VMEM_LIMIT = 56 * 1024 * 1024

_IN_SIZES = (MLA_KV_LORA, MLA_ROPE, SWA_KV_HEADS * SWA_HD, SWA_KV_HEADS * SWA_HD, S5_W, MLA_Q_LORA,
             SWA_HEADS * SWA_HD, (HY_ORDER + 1) * HY_W)
_IN_OFF = np.concatenate([[0], np.cumsum(_IN_SIZES)])
(I_CKV, I_KROPE, I_SWK, I_SWV, I_S5, I_CQ, I_SWQ, I_HY) = [int(v) for v in _IN_OFF[:-1]]

O_KR, O_CKV, O_CQ = 0, 128, 256
O_SQ, O_SK, O_SV = 512, 1024, 1280
O_S5, O_HY = 1536, 1792
N_BIG = O_HY + (HY_ORDER + 1) * HY_W


def _cparams(*sem):
    return pltpu.CompilerParams(dimension_semantics=sem, vmem_limit_bytes=VMEM_LIMIT)


def _resident(shape):
    nd = len(shape)
    return pl.BlockSpec(shape, lambda *_: (0,) * nd, pipeline_mode=pl.Buffered(1))


def _layer_resident(shape, layer):
    nd = len(shape) - 1
    return pl.BlockSpec((1,) + tuple(shape[1:]), lambda *_: (layer,) + (0,) * nd, pipeline_mode=pl.Buffered(1))


def _dot(a, b):
    return jnp.dot(a, b, preferred_element_type=F32)


def _dot_nt(a, b):
    return lax.dot_general(a, b, (((1,), (1,)), ((), ())), preferred_element_type=F32)


def _bdot(a, b):
    return lax.dot_general(a, b, (((2,), (1,)), ((0,), (0,))), preferred_element_type=F32)


def _rms(x, g):
    return x * lax.rsqrt(jnp.mean(x * x, axis=-1, keepdims=True) + EPS) * g


def _sigmoid(x):
    return 1.0 / (1.0 + jnp.exp(-x))


def _pre_mod(x, gpre, mod, sub):
    return _rms(x, gpre) * (1.0 + mod[3 * sub + 1:3 * sub + 2, :]) + mod[3 * sub:3 * sub + 1, :]


def _mod_kernel(c_ref, w_ref, b_ref, o_ref):
    c = c_ref[...]
    s = c * _sigmoid(c)
    w = w_ref[0]
    s_hi = s.astype(CDT)
    s_lo = (s - s_hi.astype(F32)).astype(CDT)
    w_hi = w.astype(CDT)
    w_lo = (w - w_hi.astype(F32)).astype(CDT)
    o_ref[0] = _dot(s_hi, w_hi) + _dot(s_hi, w_lo) + _dot(s_lo, w_hi) + b_ref[0]


def _modulation(cvec, w_ada, b_ada):
    depth, d, n = w_ada.shape
    rows = cvec.shape[0]
    tn = n // 8
    out = pl.pallas_call(
        _mod_kernel,
        out_shape=jax.ShapeDtypeStruct((depth, rows, n), F32),
        grid=(depth, n // tn),
        in_specs=[pl.BlockSpec((rows, d), lambda l, j: (0, 0)),
                  pl.BlockSpec((1, d, tn), lambda l, j: (l, 0, j)),
                  pl.BlockSpec((1, 1, tn), lambda l, j: (l, 0, j))],
        out_specs=pl.BlockSpec((1, rows, tn), lambda l, j: (l, 0, j)),
        compiler_params=_cparams("arbitrary", "arbitrary"),
        name="modulation",
    )(cvec, w_ada, b_ada.reshape(depth, 1, n))
    return out.reshape(depth, rows, N_MOD, D_MODEL)


FFN_CHUNK = 256


def _ffn_kernel(x_ref, mod_ref, gpre_ref, gpost_ref, wup_ref, wdn_ref, o_ref, *, sub):
    x = x_ref[0]
    mod = mod_ref[0]
    u = _pre_mod(x, gpre_ref[...], mod, sub).astype(CDT)
    acc = jnp.zeros(x.shape, F32)
    for c in range(D_FF // FFN_CHUNK):
        lo = c * FFN_CHUNK
        a = _dot(u, wup_ref[0, :, lo:lo + FFN_CHUNK])
        b = _dot(u, wup_ref[0, :, D_FF + lo:D_FF + lo + FFN_CHUNK])
        h = (a * _sigmoid(a) * b).astype(CDT)
        acc = acc + _dot(h, wdn_ref[0, lo:lo + FFN_CHUNK, :])
    gate = mod[3 * sub + 2:3 * sub + 3, :]
    o_ref[0] = x + MACARON_W * gate * _rms(acc, gpost_ref[...])


def _ffn(x, mod, mod_row, gpre, gpost, wup, wdn, layer, sub):
    b, l, d = x.shape
    tm = min(512, l)
    return pl.pallas_call(
        functools.partial(_ffn_kernel, sub=sub),
        out_shape=jax.ShapeDtypeStruct(x.shape, F32),
        grid=(b, l // tm),
        in_specs=[pl.BlockSpec((1, tm, d), lambda i, j: (i, j, 0)),
                  pl.BlockSpec((1, N_MOD, d), lambda i, j: (mod_row(i), 0, 0)),
                  _resident((1, d)), _resident((1, d)),
                  _layer_resident(wup.shape, layer), _layer_resident(wdn.shape, layer)],
        out_specs=pl.BlockSpec((1, tm, d), lambda i, j: (i, j, 0)),
        compiler_params=_cparams("parallel", "parallel"),
        name="ffn_sublayer",
    )(x, mod, gpre, gpost, wup, wdn)


def _rope(x, tabs, half):
    n = x.shape[1] // LANE
    cos, sin_up, sin_dn = [t if n == 1 else jnp.concatenate([t] * n, axis=1) for t in tabs]
    return x * cos + pltpu.roll(x, half, 1) * sin_up + pltpu.roll(x, x.shape[1] - half, 1) * sin_dn


def _inproj_kernel(x_ref, mod_ref, gpre_ref, wbig_ref, gkv_ref, gq_ref, wukv_ref, wuq_ref,
                   cm_ref, smu_ref, smd_ref, cw_ref, swu_ref, swd_ref,
                   qm_ref, km_ref, vm_ref, qw_ref, kw_ref, vw_ref, s5_ref, hy_ref, s5_scr):
    u = _pre_mod(x_ref[0], gpre_ref[...], mod_ref[0], 1).astype(CDT)

    def seg(off, n):
        return _dot(u, wbig_ref[:, off:off + n])

    rope_m = (cm_ref[...], smu_ref[...], smd_ref[...])
    rope_w = (cw_ref[...], swu_ref[...], swd_ref[...])
    kr = _rope(seg(O_KR, LANE), rope_m, MLA_ROPE // 2)
    ckv = seg(O_CKV, LANE)
    kvn = _rms(ckv, gkv_ref[...]).astype(CDT)
    kv = _dot(kvn, wukv_ref[...])
    ones_lane = (lax.broadcasted_iota(jnp.int32, (1, LANE), 1) == MLA_V).astype(F32)
    for h in range(MLA_HEADS):
        km_ref[0, h] = (kv[:, h * LANE:(h + 1) * LANE] + kr).astype(CDT)
        vm_ref[0, h] = (kv[:, (MLA_HEADS + h) * LANE:(MLA_HEADS + h + 1) * LANE] + ones_lane).T.astype(CDT)
    cq = seg(O_CQ, 2 * LANE)
    cqn = (cq * lax.rsqrt(jnp.sum(cq * cq, axis=-1, keepdims=True) * (1.0 / MLA_Q_LORA) + EPS)
           * gq_ref[...]).astype(CDT)
    qq = _rope(_dot(cqn, wuq_ref[...]), rope_m, MLA_ROPE // 2) * (MLA_SCALE * LOG2E)
    for h in range(MLA_HEADS):
        qm_ref[0, h] = qq[:, h * LANE:(h + 1) * LANE].T.astype(CDT)
    sq = _rope(seg(O_SQ, SWA_HEADS * LANE), rope_w, SWA_HD // 2) * (SWA_SCALE * LOG2E)
    for h in range(SWA_HEADS):
        qw_ref[0, h] = sq[:, h * LANE:(h + 1) * LANE].T.astype(CDT)
    sk = _rope(seg(O_SK, SWA_KV_HEADS * LANE), rope_w, SWA_HD // 2)
    sv = seg(O_SV, SWA_KV_HEADS * LANE)
    for h in range(SWA_KV_HEADS):
        kw_ref[0, h] = sk[:, h * LANE:(h + 1) * LANE].astype(CDT)
        vw_ref[0, h] = (sv[:, h * LANE:(h + 1) * LANE] + ones_lane).T.astype(CDT)
    s5 = seg(O_S5, S5_W)
    nchunk = s5_scr.shape[1] // S5_T
    for half in range(S5_W // LANE):
        s5_scr[half] = s5[:, half * LANE:(half + 1) * LANE]
        for sig in range(S5_T):
            s5_ref[half, 0, :, sig * LANE:(sig + 1) * LANE] = (
                s5_scr[half, pl.ds(sig, nchunk, stride=S5_T), :].astype(CDT))
    hy_ref[0] = seg(O_HY, (HY_ORDER + 1) * HY_W).astype(CDT)


def _inproj(x, mod, mod_row, gpre, wp, rope):
    b, l, d = x.shape
    tm = min(512, l)
    head = lambda n: jax.ShapeDtypeStruct((b, n, l, LANE), CDT)
    head_spec = lambda n: pl.BlockSpec((1, n, tm, LANE), lambda i, j: (i, 0, j, 0))
    tab_spec = pl.BlockSpec((tm, LANE), lambda i, j: (j, 0))
    row_spec = lambda n: pl.BlockSpec((1, tm, n), lambda i, j: (i, j, 0))
    head_t = lambda n: jax.ShapeDtypeStruct((b, n, LANE, l), CDT)
    head_t_spec = lambda n: pl.BlockSpec((1, n, LANE, tm), lambda i, j: (i, 0, 0, j))
    return pl.pallas_call(
        _inproj_kernel,
        out_shape=(head_t(MLA_HEADS), head(MLA_HEADS), head_t(MLA_HEADS), head_t(SWA_HEADS), head(SWA_KV_HEADS),
                   head_t(SWA_KV_HEADS), jax.ShapeDtypeStruct((S5_W // LANE, b, l // S5_T, S5_T * LANE), CDT),
                   jax.ShapeDtypeStruct((b, l, (HY_ORDER + 1) * HY_W), CDT)),
        grid=(b, l // tm),
        in_specs=[row_spec(d),
                  pl.BlockSpec((1, N_MOD, d), lambda i, j: (mod_row(i), 0, 0)),
                  _resident((1, d)), _resident(wp['w_big'].shape),
                  _resident((1, LANE)), _resident((1, 2 * LANE)),
                  _resident(wp['w_ukv'].shape), _resident(wp['w_uq'].shape),
                  ] + [tab_spec] * len(rope),
        out_specs=(head_t_spec(MLA_HEADS), head_spec(MLA_HEADS), head_t_spec(MLA_HEADS), head_t_spec(SWA_HEADS),
                   head_spec(SWA_KV_HEADS), head_t_spec(SWA_KV_HEADS),
                   pl.BlockSpec((S5_W // LANE, 1, tm // S5_T, S5_T * LANE), lambda i, j: (0, i, j, 0)),
                   row_spec((HY_ORDER + 1) * HY_W)),
        scratch_shapes=[pltpu.VMEM((S5_W // LANE, tm, LANE), F32)],
        compiler_params=_cparams("parallel", "parallel"),
        name="premod_inproj",
    )(x, mod, gpre, wp['w_big'], wp['g_kv'], wp['g_q'], wp['w_ukv'], wp['w_uq'], *rope)


MLA_KEY_CHUNK = 512
MLA_VROWS = 80


def _mla_scores(qt_ref, k_refs, s_buf, m_buf):
    qt = qt_ref[0, 0]
    m, off = None, 0
    for k_ref in k_refs:
        n = k_ref.shape[2]
        kc = min(MLA_KEY_CHUNK, n)
        for c in range(n // kc):
            s = _dot(k_ref[0, 0, c * kc:(c + 1) * kc, :], qt)
            s_buf[off + c * kc:off + (c + 1) * kc, :] = s
            cmax = s.max(axis=0, keepdims=True)
            m = cmax if m is None else jnp.maximum(m, cmax)
        off += n
    m_buf[...] = m


def _mla_values(vt_refs, s_buf, m_buf, o_ref):
    m = m_buf[...]
    acc, off = None, 0
    for vt_ref in vt_refs:
        n = vt_ref.shape[3]
        kc = min(MLA_KEY_CHUNK, n)
        for c in range(n // kc):
            p = jnp.exp2(s_buf[off + c * kc:off + (c + 1) * kc, :] - m).astype(CDT)
            pv = _dot(vt_ref[0, 0, 0:MLA_VROWS, c * kc:(c + 1) * kc], p)
            acc = pv if acc is None else acc + pv
        off += n
    o = acc / acc[MLA_V:MLA_V + 1, :]
    o = jnp.concatenate([o, jnp.zeros((LANE - MLA_VROWS, o.shape[1]), F32)], axis=0)
    o_ref[0] = o.T.astype(CDT)


def _mla_kernel(*refs, n_src, ntile):
    qt_ref, o_ref = refs[0], refs[1 + 2 * n_src]
    k_refs = [refs[1 + 2 * i] for i in range(n_src)]
    vt_refs = [refs[2 + 2 * i] for i in range(n_src)]
    s_bufs = refs[2 + 2 * n_src:4 + 2 * n_src]
    m_bufs = refs[4 + 2 * n_src:6 + 2 * n_src]
    t = pl.program_id(2)
    odd = t % 2 == 1

    @pl.when(t == 0)
    def _():
        _mla_scores(qt_ref, k_refs, s_bufs[0], m_bufs[0])

    @pl.when((t > 0) & (t < ntile) & odd)
    def _():
        _mla_scores(qt_ref, k_refs, s_bufs[1], m_bufs[1])
        _mla_values(vt_refs, s_bufs[0], m_bufs[0], o_ref)

    @pl.when((t > 0) & (t < ntile) & jnp.logical_not(odd))
    def _():
        _mla_scores(qt_ref, k_refs, s_bufs[0], m_bufs[0])
        _mla_values(vt_refs, s_bufs[1], m_bufs[1], o_ref)

    @pl.when(t == ntile)
    def _():
        _mla_values(vt_refs, s_bufs[(ntile - 1) % 2], m_bufs[(ntile - 1) % 2], o_ref)


def _mla(qt, kvs):
    b, h, _, l = qt.shape
    tq = min(512, l)
    ntile = l // tq
    in_specs = [pl.BlockSpec((1, 1, LANE, tq), lambda i, j, t: (i, j, 0, jnp.minimum(t, ntile - 1)))]
    args = [qt]
    for k, vt in kvs:
        n = k.shape[2]
        in_specs += [pl.BlockSpec((1, 1, n, LANE), lambda i, j, t: (i, j, 0, 0)),
                     pl.BlockSpec((1, 1, LANE, n), lambda i, j, t: (i, j, 0, 0))]
        args += [k, vt]
    nk = sum(k.shape[2] for k, _ in kvs)
    return pl.pallas_call(
        functools.partial(_mla_kernel, n_src=len(kvs), ntile=ntile),
        out_shape=jax.ShapeDtypeStruct((b, l, h * LANE), CDT),
        grid=(b, h, ntile + 1),
        in_specs=in_specs,
        out_specs=pl.BlockSpec((1, tq, LANE), lambda i, j, t: (i, jnp.maximum(t - 1, 0), j)),
        scratch_shapes=[pltpu.VMEM((nk, tq), F32), pltpu.VMEM((nk, tq), F32),
                        pltpu.VMEM((1, tq), F32), pltpu.VMEM((1, tq), F32)],
        compiler_params=_cparams("parallel", "parallel", "arbitrary"),
        name="mla_attention",
    )(*args)


def _swa_kernel(*refs, band, tq, nblk):
    sink_ref, qt_ref = refs[0], refs[1]
    o_ref = refs[-1]
    t = pl.program_id(1)
    g = SWA_HEADS // SWA_KV_HEADS
    lane = lax.broadcasted_iota(jnp.int32, (1, g * tq), 1)
    qi = jnp.where(lane < tq, lane, lane - tq)
    jp = lax.broadcasted_iota(jnp.int32, (WINDOW, 1), 0)
    jc = lax.broadcasted_iota(jnp.int32, (tq, 1), 0)
    for kv in range(SWA_KV_HEADS):
        qt = jnp.concatenate([qt_ref[0, kv * g + i] for i in range(g)], axis=-1)
        snk = jnp.where(lane < tq, sink_ref[kv * g], sink_ref[kv * g + 1]) * LOG2E
        ss, vts = [], []
        if band:
            kp, kc, kn, vtp, vtc, vtn, kx, vtx = [r[0, kv] for r in refs[2:10]]
            s_p = jnp.where((jp >= qi) & (t > 0), _dot(kp, qt), NEG_INF)
            s_c = jnp.where(jnp.abs(qi - jc) <= WINDOW, _dot(kc, qt), NEG_INF)
            s_n = jnp.where((jp <= qi - (tq - WINDOW)) & (t < nblk - 1), _dot(kn, qt), NEG_INF)
            ss += [s_p, s_c, s_n]
            vts += [vtp, vtc, vtn]
        else:
            kx, vtx = [r[0, kv] for r in refs[2:4]]
        ss.append(_dot(kx, qt))
        vts.append(vtx)
        m = snk
        for s in ss:
            m = jnp.maximum(m, s.max(axis=0, keepdims=True))
        acc = None
        for s, vt in zip(ss, vts):
            pv = _dot(vt[0:MLA_VROWS, :], jnp.exp2(s - m).astype(CDT))
            acc = pv if acc is None else acc + pv
        o = acc / (acc[SWA_HD:SWA_HD + 1, :] + jnp.exp2(snk - m))
        o = jnp.concatenate([o, jnp.zeros((LANE - MLA_VROWS, g * tq), F32)], axis=0)
        for i in range(g):
            o_ref[0, :, (kv * g + i) * LANE:(kv * g + i + 1) * LANE] = o[:, i * tq:(i + 1) * tq].T.astype(CDT)


def _swa(sink, qt, k, vt, kx, vtx, band):
    b, _, _, l = qt.shape
    hk = SWA_KV_HEADS
    tq = min(512, l)
    nblk = l // tq
    r = tq // WINDOW
    nw = l // WINDOW
    in_specs = [pl.BlockSpec(memory_space=pltpu.SMEM),
                pl.BlockSpec((1, SWA_HEADS, LANE, tq), lambda i, t: (i, 0, 0, t))]
    args = [sink, qt]
    if band:
        prev_i = lambda t: jnp.maximum(t * r - 1, 0)
        next_i = lambda t: jnp.minimum((t + 1) * r, nw - 1)
        in_specs += [pl.BlockSpec((1, hk, WINDOW, LANE), lambda i, t: (i, 0, prev_i(t), 0)),
                     pl.BlockSpec((1, hk, tq, LANE), lambda i, t: (i, 0, t, 0)),
                     pl.BlockSpec((1, hk, WINDOW, LANE), lambda i, t: (i, 0, next_i(t), 0)),
                     pl.BlockSpec((1, hk, LANE, WINDOW), lambda i, t: (i, 0, 0, prev_i(t))),
                     pl.BlockSpec((1, hk, LANE, tq), lambda i, t: (i, 0, 0, t)),
                     pl.BlockSpec((1, hk, LANE, WINDOW), lambda i, t: (i, 0, 0, next_i(t)))]
        args += [k, k, k, vt, vt, vt]
    nx = kx.shape[2]
    in_specs += [pl.BlockSpec((1, hk, nx, LANE), lambda i, t: (i, 0, 0, 0)),
                 pl.BlockSpec((1, hk, LANE, nx), lambda i, t: (i, 0, 0, 0))]
    args += [kx, vtx]
    return pl.pallas_call(
        functools.partial(_swa_kernel, band=band, tq=tq, nblk=nblk),
        out_shape=jax.ShapeDtypeStruct((b, l, SWA_HEADS * LANE), CDT),
        grid=(b, nblk),
        in_specs=in_specs,
        out_specs=pl.BlockSpec((1, tq, SWA_HEADS * LANE), lambda i, t: (i, t, 0)),
        compiler_params=_cparams("parallel", "parallel"),
        name="swa_attention",
    )(*args)


def _shortconv_kernel(x_ref, w_ref, b_ref, o_ref):
    x = x_ref[0].astype(F32)
    l = x.shape[0]
    t = lax.broadcasted_iota(jnp.int32, (l, 1), 0)
    prev = jnp.where(t == 0, 0.0, pltpu.roll(x, 1, 0))
    nxt = jnp.where(t == l - 1, 0.0, pltpu.roll(x, l - 1, 0))
    y = b_ref[...] + prev * w_ref[0:1, :] + x * w_ref[1:2, :] + nxt * w_ref[2:3, :]
    o_ref[0] = y.T.astype(CDT)


def _shortconv_t(x, w, bias):
    b, l, c = x.shape
    tc = 256
    return pl.pallas_call(
        _shortconv_kernel,
        out_shape=jax.ShapeDtypeStruct((b, c, l), CDT),
        grid=(b, c // tc),
        in_specs=[pl.BlockSpec((1, l, tc), lambda i, j: (i, 0, j)),
                  pl.BlockSpec((SHORT_K, tc), lambda i, j: (0, j)),
                  pl.BlockSpec((1, tc), lambda i, j: (0, j))],
        out_specs=pl.BlockSpec((1, tc, l), lambda i, j: (i, j, 0)),
        compiler_params=_cparams("parallel", "parallel"),
        name="hyena_shortconv",
    )(x, w, bias)


def _swap(x):
    return jnp.concatenate([x[..., LANE:], x[..., :LANE]], axis=-1)


def _comb(pq, n):
    p, q = pq[:, :n], pq[:, n:]
    return jnp.concatenate([p[..., :LANE] - q[..., LANE:], p[..., LANE:] + q[..., :LANE]], axis=-1)


def _hyena_kernel(v_ref, g1_ref, g2_ref, f1_ref, c1_ref, tw_ref, f2_ref, f2c_ref, ka_ref, kb_ref, bias_ref,
                  o_ref, *, ct, na, kin):
    def load(ref):
        return jnp.concatenate([ref[0, 0], ref[0, 1]], axis=-1).astype(F32)

    z = load(v_ref)
    gates = (g1_ref, g2_ref)
    twa, twb = tw_ref[0], tw_ref[1]
    for o in range(HY_ORDER):
        a = _comb(_bdot(f1_ref[...], z.astype(CDT)), na)
        a = a * twa + _swap(a) * twb
        x = _dot(a.reshape(ct * na, 2 * LANE).astype(CDT), f2_ref[...])
        y = x * ka_ref[o].reshape(ct * na, 2 * LANE) + _swap(x) * kb_ref[o].reshape(ct * na, 2 * LANE)
        bq = _dot(y.astype(CDT), f2c_ref[...]).reshape(ct, na, 2 * LANE)
        bq = bq * twa - _swap(bq) * twb
        yt = _comb(_bdot(c1_ref[...], bq.astype(CDT)), kin)
        z = load(gates[o]) * (yt + bias_ref[o] * z)
    o_ref[0, 0] = z[..., :LANE].astype(CDT)
    o_ref[0, 1] = z[..., LANE:].astype(CDT)


def _hyena_tables(na, kin, ct):
    f1, tw, f2, f2c, c1 = _hyena_dft_np(na, kin)
    bc = lambda m: jnp.broadcast_to(jnp.asarray(m, F32).astype(CDT)[None], (ct,) + m.shape)
    return (bc(f1), bc(c1), jnp.asarray(tw, F32), jnp.asarray(f2, F32).astype(CDT),
            jnp.asarray(f2c, F32).astype(CDT))


def _hyena_dft_np(na, kin):
    n = na * LANE
    ka = np.arange(na)[:, None]
    a = np.arange(kin)[None, :]
    ang1 = -2.0 * np.pi * ((ka * a) % na) / na
    f1 = np.concatenate([np.cos(ang1), np.sin(ang1)], axis=0)
    c1 = np.concatenate([np.cos(ang1).T, -np.sin(ang1).T], axis=0)
    bb = np.arange(LANE)[None, :]
    angt = -2.0 * np.pi * ((ka * bb) % n) / n
    tr, ti = np.cos(angt), np.sin(angt)
    tw = np.stack([np.concatenate([tr, tr], 1), np.concatenate([-ti, ti], 1)])
    b2 = np.arange(LANE)
    ang2 = -2.0 * np.pi * ((b2[:, None] * b2[None, :]) % LANE) / LANE
    fr, fi = np.cos(ang2), np.sin(ang2)
    f2 = np.block([[fr, fi], [-fi, fr]])
    f2c = np.block([[fr, -fi], [fi, fr]])
    return f1, tw, f2, f2c, c1


def _hyena_filter_taps(l, hp):
    t = jnp.linspace(0.0, 1.0, l, dtype=F32)[None, :]
    bands = (HY_EMB - 1) // 2
    w = 2.0 * math.pi * jnp.arange(l, dtype=F32) / l
    fr = jnp.linspace(1e-4, bands - 1, bands, dtype=F32)
    ang = fr[:, None] * w[None, :]
    z = jnp.concatenate([t, jnp.cos(ang), -jnp.sin(ang)], axis=0)
    freq = hp['hy_f_freq']
    hi = lax.Precision.HIGHEST
    h = jnp.sin(freq[0][:, None] * (jnp.dot(hp['hy_f_w1'].T, z, precision=hi) + hp['hy_f_b1'][:, None]))
    h = jnp.sin(freq[1][:, None] * (jnp.dot(hp['hy_f_w2'].T, h, precision=hi) + hp['hy_f_b2'][:, None]))
    h = jnp.dot(hp['hy_f_w3'].T, h, precision=hi).reshape(HY_ORDER, 2, HY_W, l)
    deltas = jnp.abs(jnp.linspace(math.log(HY_DECAY_TARGET) / HY_SLOW_PCT,
                                  math.log(HY_DECAY_TARGET) / HY_FAST_PCT, HY_W, dtype=F32))
    h = h * jnp.exp(-t * deltas[:, None])
    lag0 = (jnp.arange(l) > 0).astype(F32)
    return h[:, 0].reshape(HY_ORDER * HY_W, l), (h[:, 1] * lag0).reshape(HY_ORDER * HY_W, l)


def _split(x):
    hi = x.astype(CDT)
    return hi, (x - hi.astype(F32)).astype(CDT)


def _hyfilt_kernel(hf_ref, hb_ref, f1h_ref, f1l_ref, tw_ref, f2h_ref, f2l_ref, ka_ref, kb_ref, *, rt, na, inv_n):
    xh, xl = _split(jnp.concatenate([hf_ref[...], hb_ref[...]], axis=-1))
    f1h, f1l = f1h_ref[...], f1l_ref[...]
    pq = _bdot(f1h, xh) + _bdot(f1l, xh) + _bdot(f1h, xl)
    p, q = pq[:, :na], pq[:, na:]
    a = jnp.concatenate([jnp.concatenate([p[..., :LANE], q[..., :LANE]], -1),
                         jnp.concatenate([p[..., LANE:], q[..., LANE:]], -1)], axis=0)
    a = a * tw_ref[0] + _swap(a) * tw_ref[1]
    ah, al = _split(a.reshape(2 * rt * na, 2 * LANE))
    x = (_dot(ah, f2h_ref[...]) + _dot(al, f2h_ref[...]) + _dot(ah, f2l_ref[...])).reshape(2 * rt, na, 2 * LANE)
    xf, xb = x[:rt], x[rt:]
    kr = (xf[..., :LANE] + xb[..., :LANE]) * inv_n
    ki = (xf[..., LANE:] - xb[..., LANE:]) * inv_n
    ka_ref[...] = jnp.concatenate([kr, kr], -1)
    kb_ref[...] = jnp.concatenate([-ki, ki], -1)


def _hyena_filter_spectrum(hf, hb, na, kin):
    rows, l = hf.shape
    lp = kin * LANE
    if lp != l:
        hf, hb = [jnp.pad(h, ((0, 0), (0, lp - l))) for h in (hf, hb)]
    hf, hb = hf.reshape(rows, kin, LANE), hb.reshape(rows, kin, LANE)
    rt = 16
    f1, tw, f2 = _hyena_dft_np(na, kin)[:3]
    f1 = jnp.broadcast_to(jnp.asarray(f1, F32)[None], (rt,) + f1.shape)
    f1h, f1l = _split(f1)
    f2h, f2l = _split(jnp.asarray(f2, F32))
    tw = jnp.asarray(tw, F32)
    ka, kb = pl.pallas_call(
        functools.partial(_hyfilt_kernel, rt=rt, na=na, inv_n=1.0 / (na * LANE)),
        out_shape=(jax.ShapeDtypeStruct((rows, na, 2 * LANE), F32),) * 2,
        grid=(rows // rt,),
        in_specs=[pl.BlockSpec((rt, kin, LANE), lambda i: (i, 0, 0)),
                  pl.BlockSpec((rt, kin, LANE), lambda i: (i, 0, 0)),
                  _resident(f1h.shape), _resident(f1l.shape), _resident(tw.shape),
                  _resident(f2h.shape), _resident(f2l.shape)],
        out_specs=(pl.BlockSpec((rt, na, 2 * LANE), lambda i: (i, 0, 0)),) * 2,
        compiler_params=_cparams("parallel"),
        name="hyena_filter_spectrum",
    )(hf, hb, f1h, f1l, tw, f2h, f2l)
    shape = (HY_ORDER, HY_W, na, 2 * LANE)
    return ka.reshape(shape), kb.reshape(shape)


def _hyena(hy, hp, l_true):
    b, l, _ = hy.shape
    ut = _shortconv_t(hy, hp['hy_conv_w'], hp['hy_conv_b'].reshape(1, -1))
    kin = max(l // LANE, 16)
    na = 2 * kin
    lp = kin * LANE
    if lp != l:
        ut = jnp.pad(ut, ((0, 0), (0, 0), (0, lp - l)))
    ut = ut.reshape(b // 2, 2, (HY_ORDER + 1) * HY_W, kin, LANE)
    ct = 16
    nc = HY_W // ct
    f1, c1, tw, f2, f2c = _hyena_tables(na, kin, ct)
    ka, kb = hp['ka'], hp['kb']
    bias = jnp.broadcast_to(hp['hy_bias'].reshape(HY_ORDER, HY_W, 1, 1), (HY_ORDER, HY_W, 1, 2 * LANE))
    blk = lambda off: pl.BlockSpec((1, 2, ct, kin, LANE), lambda c, p: (p, 0, c + off * nc, 0, 0))
    out = pl.pallas_call(
        functools.partial(_hyena_kernel, ct=ct, na=na, kin=kin),
        out_shape=jax.ShapeDtypeStruct((b // 2, 2, HY_W, kin, LANE), CDT),
        grid=(nc, b // 2),
        in_specs=[blk(0), blk(1), blk(2),
                  _resident(f1.shape), _resident(c1.shape), _resident(tw.shape),
                  _resident(f2.shape), _resident(f2c.shape),
                  pl.BlockSpec((HY_ORDER, ct, na, 2 * LANE), lambda c, p: (0, c, 0, 0)),
                  pl.BlockSpec((HY_ORDER, ct, na, 2 * LANE), lambda c, p: (0, c, 0, 0)),
                  pl.BlockSpec((HY_ORDER, ct, 1, 2 * LANE), lambda c, p: (0, c, 0, 0))],
        out_specs=pl.BlockSpec((1, 2, ct, kin, LANE), lambda c, p: (p, 0, c, 0, 0)),
        compiler_params=_cparams("parallel", "arbitrary"),
        name="hyena_longconv",
    )(ut, ut, ut, f1, c1, tw, f2, f2c, ka, kb, bias)
    out = out.reshape(b, HY_W, lp)[:, :, :l]
    return jnp.swapaxes(out, 1, 2)


def _s5_kernel(u_ref, psel_ref, pselt_ref, tloc_ref, wx_ref, wout_ref, d_ref, h0_ref, y_ref, hfin_ref,
               x_scr, h_scr, up_scr, *, nchunk, nb):
    w = S5_T * S5_GC
    gb = max(1, min(nb, S5_BLOCK_ROWS // nchunk))
    for b0 in range(0, nb, gb):
        rs = slice(b0 * nchunk, (b0 + gb) * nchunk)
        up = _dot(u_ref[0, rs, :], psel_ref[0]).astype(CDT)
        up_scr[rs, :] = up
        xb = _dot(up, wx_ref[0])
        for i in range(gb):
            for k in range(4):
                x_scr[k, pl.ds(b0 + i, nchunk, stride=nb), :] = xb[i * nchunk:(i + 1) * nchunk,
                                                                   k * LANE:(k + 1) * LANE]
    d = d_ref[0]
    dfr, dfi, dbr, dbi = d[0:1], d[1:2], d[2:3], d[3:4]

    def body(j, carry):
        hr, hi, gr, gi = carry
        rf = pl.multiple_of(j * nb, nb)
        rb = pl.multiple_of((nchunk - 1 - j) * nb, nb)
        h_scr[0, pl.ds(rf, nb), :] = hr
        h_scr[1, pl.ds(rf, nb), :] = hi
        h_scr[2, pl.ds(rb, nb), :] = gr
        h_scr[3, pl.ds(rb, nb), :] = gi
        xr = x_scr[0, pl.ds(rf, nb), :]
        xi = x_scr[1, pl.ds(rf, nb), :]
        yr = x_scr[2, pl.ds(rb, nb), :]
        yi = x_scr[3, pl.ds(rb, nb), :]
        return (dfr * hr - dfi * hi + xr, dfr * hi + dfi * hr + xi,
                dbr * gr - dbi * gi + yr, dbr * gi + dbi * gr + yi)

    fin = lax.fori_loop(0, nchunk, body, tuple(h0_ref[0, k] for k in range(4)))
    for k in range(4):
        hfin_ref[0, k] = fin[k]
    r = pl.program_id(1)
    for b0 in range(0, nb, gb):
        rs = slice(b0 * nchunk, (b0 + gb) * nchunk)
        hs = jnp.concatenate(
            [jnp.concatenate([h_scr[k, pl.ds(b0 + i, nchunk, stride=nb), :] for k in range(4)], axis=-1)
             for i in range(gb)], axis=0).astype(CDT)
        y0 = _dot(up_scr[rs, :w], tloc_ref[0]) + _dot(hs, wout_ref[0, :, :w])
        y1 = _dot(up_scr[rs, w:], tloc_ref[1]) + _dot(hs, wout_ref[0, :, w:])
        contrib = _dot(jnp.concatenate([y0, y1], axis=-1).astype(CDT), pselt_ref[0]).astype(CDT)

        @pl.when(r == 0)
        def _():
            y_ref[0, rs, :] = contrib

        @pl.when(r > 0)
        def _():
            y_ref[0, rs, :] = y_ref[0, rs, :] + contrib


S5_HALF_GROUPS = LANE // S5_GC
S5_HALF_PAIRS = S5_HALF_GROUPS // 2


def _s5_select():
    ri = jnp.arange(S5_T * LANE)
    r_sig, r_grp, r_ch = ri // LANE, (ri % LANE) // S5_GC, ri % S5_GC
    ci = jnp.arange(2 * S5_T * S5_GC)
    c_grp, c_sig, c_ch = ci // (S5_T * S5_GC), (ci % (S5_T * S5_GC)) // S5_GC, ci % S5_GC
    same = (r_sig[:, None] == c_sig[None, :]) & (r_ch[:, None] == c_ch[None, :])
    sel = jnp.stack([(same & (r_grp[:, None] == 2 * q + c_grp[None, :])) for q in range(S5_HALF_PAIRS)])
    sel = sel.astype(CDT)
    return sel, jnp.swapaxes(sel, 1, 2)


def _s5(u, ops, h0):
    nhalf, b, nchunk, wide = u.shape
    w = S5_T * S5_GC
    rows = nchunk * b
    uh = u.reshape(nhalf, rows, wide)
    psel, pselt = _s5_select()
    pair = lambda h, r: h * S5_HALF_PAIRS + r
    y, hfin = pl.pallas_call(
        functools.partial(_s5_kernel, nchunk=nchunk, nb=b),
        out_shape=(jax.ShapeDtypeStruct((nhalf, rows, wide), CDT),
                   jax.ShapeDtypeStruct((S5_GROUPS // 2, 4, b, LANE), F32)),
        grid=(nhalf, S5_HALF_PAIRS),
        in_specs=[pl.BlockSpec((1, rows, wide), lambda h, r: (h, 0, 0), pipeline_mode=pl.Buffered(1)),
                  pl.BlockSpec((1, wide, 2 * w), lambda h, r: (r, 0, 0)),
                  pl.BlockSpec((1, 2 * w, wide), lambda h, r: (r, 0, 0)),
                  pl.BlockSpec((2, w, w), lambda h, r: (pair(h, r), 0, 0)),
                  pl.BlockSpec((1, 2 * w, 2 * w), lambda h, r: (pair(h, r), 0, 0)),
                  pl.BlockSpec((1, 2 * w, 2 * w), lambda h, r: (pair(h, r), 0, 0)),
                  pl.BlockSpec((1, 4, LANE), lambda h, r: (pair(h, r), 0, 0)),
                  pl.BlockSpec((1, 4, b, LANE), lambda h, r: (pair(h, r), 0, 0, 0))],
        out_specs=(pl.BlockSpec((1, rows, wide), lambda h, r: (h, 0, 0)),
                   pl.BlockSpec((1, 4, b, LANE), lambda h, r: (pair(h, r), 0, 0, 0))),
        scratch_shapes=[pltpu.VMEM((4, rows, LANE), F32), pltpu.VMEM((4, rows, LANE), F32),
                        pltpu.VMEM((rows, 2 * w), CDT)],
        compiler_params=_cparams("parallel", "arbitrary"),
        name="s5_chunked",
    )(uh, psel, pselt, ops['tloc'], ops['wx'], ops['wout'], ops['d16'], h0)
    return y.reshape(nhalf, b, nchunk, wide), hfin


def _s5_operators(p):
    g, pp, gc, t = S5_GROUPS, S5_P, S5_GC, S5_T
    npair = g // 2
    n = jnp.arange(t + 1, dtype=F32)[:, None, None]
    tops, wxs, wouts, d16 = [], [], [], []
    sig = jnp.arange(t)
    for d in range(2):
        a_re, a_im = p['s5_a_re'][d], p['s5_a_im'][d]
        dt = jnp.exp(p['s5_log_dt'][d])[:, None]
        mag1 = jnp.exp(dt * a_re)
        ab_re, ab_im = mag1 * jnp.cos(dt * a_im), mag1 * jnp.sin(dt * a_im)
        den = a_re * a_re + a_im * a_im
        f_re = ((ab_re - 1.0) * a_re + ab_im * a_im) / den
        f_im = (ab_im * a_re - (ab_re - 1.0) * a_im) / den
        mag = jnp.exp(n * (dt * a_re)[None])
        pr, pi = mag * jnp.cos(n * (dt * a_im)[None]), mag * jnp.sin(n * (dt * a_im)[None])
        b_re, b_im = p['s5_b_re'][d], p['s5_b_im'][d]
        bt_re = f_re[..., None] * b_re - f_im[..., None] * b_im
        bt_im = f_re[..., None] * b_im + f_im[..., None] * b_re
        c_re, c_im = p['s5_c_re'][d], p['s5_c_im'][d]
        ca_re = c_re[None] * pr[:, :, None, :] - c_im[None] * pi[:, :, None, :]
        ca_im = c_re[None] * pi[:, :, None, :] + c_im[None] * pr[:, :, None, :]
        hi = lax.Precision.HIGHEST
        lag_re = ca_re[:t] if d == 0 else ca_re[:t][::-1]
        lag_im = ca_im[:t] if d == 0 else ca_im[:t][::-1]
        tops.append(jnp.einsum('ngcp,gpk->gknc', lag_re, bt_re, precision=hi)
                    - jnp.einsum('ngcp,gpk->gknc', lag_im, bt_im, precision=hi))
        pwr, pwi = (pr[:t][::-1], pi[:t][::-1]) if d == 0 else (pr[:t], pi[:t])
        pwr, pwi = pwr.transpose(1, 0, 2)[:, :, None, :], pwi.transpose(1, 0, 2)[:, :, None, :]
        btr, bti = bt_re.transpose(0, 2, 1)[:, None], bt_im.transpose(0, 2, 1)[:, None]
        wxs.append(((pwr * btr - pwi * bti).reshape(g, t * gc, pp), (pwr * bti + pwi * btr).reshape(g, t * gc, pp)))
        out_re = ca_re[1:t + 1] if d == 0 else ca_re[1:t + 1][::-1]
        out_im = ca_im[1:t + 1] if d == 0 else ca_im[1:t + 1][::-1]
        wouts.append((out_re.transpose(1, 3, 0, 2).reshape(g, pp, t * gc),
                      -out_im.transpose(1, 3, 0, 2).reshape(g, pp, t * gc)))
        d16.append((pr[t], pi[t]))
    skip = jnp.eye(gc, dtype=F32)[None, :, None, :] * p['s5_d'].reshape(g, 1, 1, gc)
    kf, kb = tops
    kcomb = jnp.concatenate([kb[:, :, :t - 1], kf[:, :, 0:1] + kb[:, :, t - 1:t] + skip, kf[:, :, 1:]], axis=2)
    kcomb = kcomb.reshape(g, gc, (2 * t - 1) * gc)
    tloc = jnp.stack([kcomb[:, :, (t - 1 - s) * gc:(t - 1 - s) * gc + t * gc] for s in range(t)], axis=1)
    tloc = tloc.reshape(g, t * gc, t * gc)

    def pair_cols(m):
        m = m.reshape(npair, 2, m.shape[1], m.shape[2])
        z = jnp.zeros_like(m[:, 0])
        return jnp.concatenate([jnp.concatenate([m[:, 0], z], -1), jnp.concatenate([z, m[:, 1]], -1)], -2)

    wx = jnp.concatenate([pair_cols(wxs[0][0]), pair_cols(wxs[0][1]), pair_cols(wxs[1][0]), pair_cols(wxs[1][1])],
                         axis=-1)
    wout = jnp.concatenate([pair_cols(wouts[0][0]), pair_cols(wouts[0][1]), pair_cols(wouts[1][0]),
                            pair_cols(wouts[1][1])], axis=-2)
    dd = jnp.stack([d16[0][0], d16[0][1], d16[1][0], d16[1][1]], axis=0)
    dd = dd.reshape(4, npair, 2 * pp).transpose(1, 0, 2)
    return dict(tloc=tloc.astype(CDT), wx=wx.astype(CDT), wout=wout.astype(CDT), d16=dd)


def _merge_kernel(x_ref, mod_ref, gpre_ref, gpost_ref, omla_ref, ohy_ref, oswa_ref, ys5_ref,
                  wgate_ref, bgate_ref, wmla_ref, why_ref, wswa_ref, ws5_ref, gluw_ref, glub_ref, wout_ref, o_ref,
                  y_scr):
    x = x_ref[0]
    mod = mod_ref[0]
    u = _pre_mod(x, gpre_ref[...], mod, 1).astype(CDT)
    nchunk = y_scr.shape[1] // S5_T
    for sig in range(S5_T):
        for half in range(S5_W // LANE):
            y_scr[half, pl.ds(sig, nchunk, stride=S5_T), :] = (
                ys5_ref[half, 0, :, sig * LANE:(sig + 1) * LANE].astype(F32))
    y = jnp.concatenate([y_scr[half] for half in range(S5_W // LANE)], axis=-1)
    g = 0.5 * y * (1.0 + jnp.tanh(math.sqrt(2.0 / math.pi) * (y + 0.044715 * (y * y * y))))
    o_s5 = (g * _sigmoid(_dot(g.astype(CDT), gluw_ref[...]) + glub_ref[...])).astype(CDT)
    outs = (omla_ref[0], ohy_ref[0], oswa_ref[0], o_s5)
    wbr = (wmla_ref, why_ref, wswa_ref, ws5_ref)
    m = jnp.zeros(x.shape, F32)
    for i in range(4):
        gate = _sigmoid(_dot(u, wgate_ref[0, i]) + bgate_ref[:, i * D_MODEL:(i + 1) * D_MODEL])
        m = m + gate * _dot(outs[i], wbr[i][...])
    f = _dot(m.astype(CDT), wout_ref[0])
    o_ref[0] = x + mod[5:6, :] * _rms(f, gpost_ref[...])


def _merge(x, mod, mod_row, gpre, gpost, o_mla, o_hy, o_swa, y_s5, wp, layer):
    b, l, d = x.shape
    tm = min(512, l)
    row_spec = lambda n: pl.BlockSpec((1, tm, n), lambda i, j: (i, j, 0))
    names = ('w_gate', 'b_gate', 'w_br_mla', 'w_br_hy', 'w_br_swa', 'w_br_s5', 'glu_w', 'glu_b', 'w_out')
    stacked = ('w_gate', 'w_out')
    wspec = lambda k: _layer_resident(wp[k].shape, layer) if k in stacked else _resident(wp[k].shape)
    return pl.pallas_call(
        _merge_kernel,
        out_shape=jax.ShapeDtypeStruct(x.shape, F32),
        grid=(b, l // tm),
        in_specs=[row_spec(d), pl.BlockSpec((1, N_MOD, d), lambda i, j: (mod_row(i), 0, 0)),
                  _resident((1, d)), _resident((1, d)),
                  row_spec(o_mla.shape[-1]), row_spec(o_hy.shape[-1]), row_spec(o_swa.shape[-1]),
                  pl.BlockSpec((S5_W // LANE, 1, tm // S5_T, S5_T * LANE), lambda i, j: (0, i, j, 0))]
        + [wspec(k) for k in names],
        out_specs=row_spec(d),
        scratch_shapes=[pltpu.VMEM((S5_W // LANE, tm, LANE), F32)],
        compiler_params=_cparams("parallel", "parallel"),
        name="merge_out",
    )(x, mod, gpre, gpost, o_mla, o_hy, o_swa, y_s5, *[wp[k] for k in names])


def _pad_cols(w, n):
    return jnp.pad(w, ((0, 0), (0, n - w.shape[1])))


def _pad_rows(w, n):
    return jnp.pad(w, ((0, n - w.shape[0]), (0, 0)))


def _prep_inproj(w_in, w_ukv, w_uq, g_kv, g_q):
    d = w_in.shape[0]
    zeros = lambda n: jnp.zeros((d, n), F32)
    krope = w_in[:, I_KROPE:I_KROPE + MLA_ROPE]
    kr = jnp.concatenate([zeros(MLA_NOPE), krope, zeros(LANE - MLA_NOPE - MLA_ROPE)], 1)
    ckv = w_in[:, I_CKV:I_CKV + MLA_KV_LORA]
    cq = _pad_cols(w_in[:, I_CQ:I_CQ + MLA_Q_LORA], 2 * LANE)

    def heads(w, nh):
        return jnp.concatenate([_pad_cols(w[:, h * SWA_HD:(h + 1) * SWA_HD], LANE) for h in range(nh)], 1)

    swq = w_in[:, I_SWQ:I_SWQ + SWA_HEADS * SWA_HD]
    swk = w_in[:, I_SWK:I_SWK + SWA_KV_HEADS * SWA_HD]
    swv = w_in[:, I_SWV:I_SWV + SWA_KV_HEADS * SWA_HD]
    w_big = jnp.concatenate([kr, ckv, cq, heads(swq, SWA_HEADS), heads(swk, SWA_KV_HEADS),
                             heads(swv, SWA_KV_HEADS),
                             w_in[:, I_S5:I_S5 + S5_W], w_in[:, I_HY:I_HY + (HY_ORDER + 1) * HY_W]], axis=1)
    assert w_big.shape[1] == N_BIG
    kvw = w_ukv.reshape(MLA_KV_LORA, MLA_HEADS, MLA_NOPE + MLA_V)
    kslots = [_pad_cols(kvw[:, h, :MLA_NOPE], LANE) for h in range(MLA_HEADS)]
    vslots = [_pad_cols(kvw[:, h, MLA_NOPE:], LANE) for h in range(MLA_HEADS)]
    w_ukv_p = jnp.concatenate(kslots + vslots, axis=1)
    qw = w_uq.reshape(MLA_Q_LORA, MLA_HEADS, MLA_NOPE + MLA_ROPE)
    qslots = [_pad_cols(qw[:, h], LANE) for h in range(MLA_HEADS)]
    w_uq_p = _pad_rows(jnp.concatenate(qslots, axis=1), 2 * LANE)
    return dict(w_big=w_big.astype(CDT), w_ukv=w_ukv_p.astype(CDT), w_uq=w_uq_p.astype(CDT),
                g_kv=g_kv.reshape(1, -1), g_q=_pad_cols(g_q.reshape(1, -1), 2 * LANE))


def _pad_head_rows(w, nh, hd):
    w = w.reshape(nh, hd, w.shape[-1])
    return jnp.pad(w, ((0, 0), (0, LANE - hd), (0, 0))).reshape(nh * LANE, -1)


def _rope_tables(n_tokens, use_rope):
    ones = jnp.ones((n_tokens, LANE), F32)
    zeros = jnp.zeros((n_tokens, LANE), F32)
    if not use_rope:
        return ones, zeros, zeros, ones, zeros, zeros

    def axial(rot_dim):
        rows = n_tokens // GRID_W
        r = jnp.repeat(jnp.arange(rows, dtype=F32), GRID_W)
        col = jnp.tile(jnp.arange(GRID_W, dtype=F32), rows)
        n_freq = rot_dim // 4
        freqs = ROPE_THETA ** (-jnp.arange(n_freq, dtype=F32) / n_freq)
        ang = jnp.concatenate([r[:, None] * freqs, col[:, None] * freqs], axis=-1)
        return jnp.cos(ang), jnp.sin(ang)

    c, s = axial(MLA_ROPE)
    pad = LANE - MLA_NOPE - MLA_ROPE
    half = MLA_ROPE // 2
    cm = jnp.concatenate([ones[:, :MLA_NOPE], c, c, ones[:, :pad]], 1)
    smu = jnp.concatenate([zeros[:, :MLA_NOPE + half], s, zeros[:, :pad]], 1)
    smd = jnp.concatenate([zeros[:, :MLA_NOPE], -s, zeros[:, :half + pad]], 1)
    c, s = axial(SWA_HD)
    half = SWA_HD // 2
    cw = jnp.concatenate([c, c, ones[:, :LANE - SWA_HD]], 1)
    swu = jnp.concatenate([zeros[:, :half], s, zeros[:, :LANE - SWA_HD]], 1)
    swd = jnp.concatenate([-s, zeros[:, :LANE - half]], 1)
    return cm, smu, smd, cw, swu, swd


def _hyena_params(p, l):
    kin = max(l // LANE, 16)
    na = 2 * kin
    hf, hb = _hyena_filter_taps(l, p)
    ka, kb = _hyena_filter_spectrum(hf, hb, na, kin)
    return dict(hy_conv_w=p['hy_conv_w'], hy_conv_b=p['hy_conv_b'], hy_bias=p['hy_bias'], ka=ka, kb=kb)


def kernel(x, c, ctx, c_ctx, w_ada, b_ada, norm_pre, norm_post, ffn1_up, ffn1_down, ffn2_up, ffn2_down,
           w_in, mla_q_norm, mla_kv_norm, mla_w_uq, mla_w_ukv, hy_conv_w, hy_conv_b, hy_f_w1, hy_f_b1,
           hy_f_freq, hy_f_w2, hy_f_b2, hy_f_w3, hy_bias, swa_sink, s5_a_re, s5_a_im, s5_log_dt,
           s5_b_re, s5_b_im, s5_c_re, s5_c_im, s5_d, s5_glu_w, s5_glu_b, w_gate, b_gate,
           w_br_mla, w_br_hy, w_br_swa, w_br_s5, w_out):
    stacked = dict(w_ada=w_ada, b_ada=b_ada, norm_pre=norm_pre, norm_post=norm_post,
                   ffn1_up=ffn1_up, ffn1_down=ffn1_down, ffn2_up=ffn2_up, ffn2_down=ffn2_down,
                   w_in=w_in, mla_q_norm=mla_q_norm, mla_kv_norm=mla_kv_norm, mla_w_uq=mla_w_uq,
                   mla_w_ukv=mla_w_ukv, hy_conv_w=hy_conv_w, hy_conv_b=hy_conv_b, hy_f_w1=hy_f_w1,
                   hy_f_b1=hy_f_b1, hy_f_freq=hy_f_freq, hy_f_w2=hy_f_w2, hy_f_b2=hy_f_b2, hy_f_w3=hy_f_w3,
                   hy_bias=hy_bias, swa_sink=swa_sink, s5_a_re=s5_a_re, s5_a_im=s5_a_im,
                   s5_log_dt=s5_log_dt, s5_b_re=s5_b_re, s5_b_im=s5_b_im, s5_c_re=s5_c_re,
                   s5_c_im=s5_c_im, s5_d=s5_d, s5_glu_w=s5_glu_w, s5_glu_b=s5_glu_b,
                   w_gate=w_gate, b_gate=b_gate, w_br_mla=w_br_mla, w_br_hy=w_br_hy,
                   w_br_swa=w_br_swa, w_br_s5=w_br_s5, w_out=w_out)
    depth = w_ada.shape[0]
    nb, seq, d = x.shape
    nctx = ctx.shape[1]
    assert nb % 2 == 0 and seq % 256 == 0 and nctx % 256 == 0

    rows = -(-(nb + 1) // 8) * 8
    cvec = jnp.zeros((rows, d), F32).at[:nb].set(c).at[nb].set(c_ctx)
    mods = _modulation(cvec, w_ada, b_ada)
    lat_row = lambda i: i
    ctx_row = lambda i: nb

    rope_l = _rope_tables(seq, True)
    rope_c = _rope_tables(nctx, False)
    h0_zero = jnp.zeros((S5_GROUPS // 2, 4, nb, LANE), F32)

    f1u, f1d, f2u, f2d = [w.astype(CDT) for w in (ffn1_up, ffn1_down, ffn2_up, ffn2_down)]
    w_gate_all, w_out_all = w_gate.astype(CDT), w_out.astype(CDT)

    xl, xc = x, ctx
    for l in range(depth):
        p = {name: arr[l] for name, arr in stacked.items()}
        mod = mods[l]
        ctx_out = l < depth - 1
        gpre = [p['norm_pre'][i].reshape(1, d) for i in range(N_SUB)]
        gpost = [p['norm_post'][i].reshape(1, d) for i in range(N_SUB)]
        wp_in = _prep_inproj(p['w_in'], p['mla_w_ukv'], p['mla_w_uq'], p['mla_kv_norm'], p['mla_q_norm'])
        wp_mg = dict(
            w_gate=w_gate_all,
            b_gate=p['b_gate'].reshape(1, -1),
            w_br_mla=_pad_head_rows(p['w_br_mla'], MLA_HEADS, MLA_V).astype(CDT),
            w_br_hy=p['w_br_hy'].astype(CDT),
            w_br_swa=_pad_head_rows(p['w_br_swa'], SWA_HEADS, SWA_HD).astype(CDT),
            w_br_s5=p['w_br_s5'].astype(CDT),
            glu_w=p['s5_glu_w'].astype(CDT), glu_b=p['s5_glu_b'].reshape(1, -1),
            w_out=w_out_all)
        s5_ops = _s5_operators(p)

        xl = _ffn(xl, mod, lat_row, gpre[0], gpost[0], f1u, f1d, l, 0)
        xc = _ffn(xc, mod, ctx_row, gpre[0], gpost[0], f1u, f1d, l, 0)

        qm_c, km_c, vm_c, qw_c, kw_c, vw_c, s5u_c, hy_c = _inproj(xc, mod, ctx_row, gpre[1], wp_in, rope_c)
        qm_l, km_l, vm_l, qw_l, kw_l, vw_l, s5u_l, hy_l = _inproj(xl, mod, lat_row, gpre[1], wp_in, rope_l)

        ys5_c, h_ctx = _s5(s5u_c, s5_ops, h0_zero)
        ys5_l, _ = _s5(s5u_l, s5_ops, h_ctx)
        o_mla = _mla(qm_l, [(km_l, vm_l), (km_c, vm_c)])
        o_swa = _swa(p['swa_sink'], qw_l, kw_l, vw_l, kw_c, vw_c, True)
        o_hy = _hyena(hy_l, _hyena_params(p, seq), seq)
        xl = _merge(xl, mod, lat_row, gpre[1], gpost[1], o_mla, o_hy, o_swa, ys5_l, wp_mg, l)
        xl = _ffn(xl, mod, lat_row, gpre[2], gpost[2], f2u, f2d, l, 2)
        if ctx_out:
            o_mla_c = _mla(qm_c, [(km_c, vm_c)])
            o_swa_c = _swa(p['swa_sink'], qw_c, kw_c, vw_c, kw_c, vw_c, False)
            o_hy_c = _hyena(hy_c, _hyena_params(p, nctx), nctx)
            xc = _merge(xc, mod, ctx_row, gpre[1], gpost[1], o_mla_c, o_hy_c, o_swa_c, ys5_c, wp_mg, l)
            xc = _ffn(xc, mod, ctx_row, gpre[2], gpost[2], f2u, f2d, l, 2)
    return xl
```

```python
import functools
import math

import numpy as np
import jax
import jax.numpy as jnp
from jax import lax
from jax.experimental import pallas as pl
from jax.experimental.pallas import tpu as pltpu

F32 = jnp.float32
CDT = jnp.bfloat16

D_MODEL = 1024
D_FF = 2816
N_SUB = 3
N_MOD = 3 * N_SUB
MACARON_W = 0.5
ROPE_THETA = 10000.0
GRID_W = 64
EPS = 1e-6
NEG_INF = -1e30

MLA_HEADS = 4
MLA_NOPE = 64
MLA_ROPE = 32
MLA_V = 64
MLA_Q_LORA = 192
MLA_KV_LORA = 128
MLA_SCALE = (MLA_NOPE + MLA_ROPE) ** -0.5
LOG2E = math.log2(math.e)

HY_W = 256
HY_ORDER = 2
HY_EMB = 33
HY_DECAY_TARGET = 1e-2
HY_FAST_PCT = 0.3
HY_SLOW_PCT = 1.5
SHORT_K = 3

SWA_HEADS = 4
SWA_KV_HEADS = 2
SWA_HD = 64
WINDOW = 128
SWA_SCALE = SWA_HD ** -0.5

S5_W = 256
S5_GC = 16
S5_GROUPS = S5_W // S5_GC
S5_P = 64
S5_BLOCK_ROWS = 512
S5_T = 16

LANE = 128
MXU = 256
VMEM_LIMIT = 56 * 1024 * 1024

_IN_SIZES = (MLA_KV_LORA, MLA_ROPE, SWA_KV_HEADS * SWA_HD, SWA_KV_HEADS * SWA_HD, S5_W, MLA_Q_LORA,
             SWA_HEADS * SWA_HD, (HY_ORDER + 1) * HY_W)
_IN_OFF = np.concatenate([[0], np.cumsum(_IN_SIZES)])
(I_CKV, I_KROPE, I_SWK, I_SWV, I_S5, I_CQ, I_SWQ, I_HY) = [int(v) for v in _IN_OFF[:-1]]

O_KR, O_CKV, O_CQ = 0, 128, 256
O_SQ, O_SK, O_SV = 512, 1024, 1280
O_S5, O_HY = 1536, 1792
N_BIG = O_HY + (HY_ORDER + 1) * HY_W


def _cparams(*sem):
    return pltpu.CompilerParams(dimension_semantics=sem, vmem_limit_bytes=VMEM_LIMIT)


def _resident(shape):
    nd = len(shape)
    return pl.BlockSpec(shape, lambda *_: (0,) * nd, pipeline_mode=pl.Buffered(1))


def _layer_resident(shape, layer):
    nd = len(shape) - 1
    return pl.BlockSpec((1,) + tuple(shape[1:]), lambda *_: (layer,) + (0,) * nd, pipeline_mode=pl.Buffered(1))


def _dot(a, b):
    return jnp.dot(a, b, preferred_element_type=F32)


def _dot_nt(a, b):
    return lax.dot_general(a, b, (((1,), (1,)), ((), ())), preferred_element_type=F32)


def _bdot(a, b):
    return lax.dot_general(a, b, (((2,), (1,)), ((0,), (0,))), preferred_element_type=F32)


def _rms(x, g):
    return x * lax.rsqrt(jnp.mean(x * x, axis=-1, keepdims=True) + EPS) * g


def _sigmoid(x):
    return 1.0 / (1.0 + jnp.exp(-x))


def _pre_mod(x, gpre, mod, sub):
    return _rms(x, gpre) * (1.0 + mod[3 * sub + 1:3 * sub + 2, :]) + mod[3 * sub:3 * sub + 1, :]


def _mod_kernel(c_ref, w_ref, b_ref, o_ref):
    c = c_ref[...]
    s = c * _sigmoid(c)
    w = w_ref[0]
    s_hi = s.astype(CDT)
    s_lo = (s - s_hi.astype(F32)).astype(CDT)
    w_hi = w.astype(CDT)
    w_lo = (w - w_hi.astype(F32)).astype(CDT)
    o_ref[0] = _dot(s_hi, w_hi) + _dot(s_hi, w_lo) + _dot(s_lo, w_hi) + b_ref[0]


def _modulation(cvec, w_ada, b_ada):
    depth, d, n = w_ada.shape
    rows = cvec.shape[0]
    tn = n // 8
    out = pl.pallas_call(
        _mod_kernel,
        out_shape=jax.ShapeDtypeStruct((depth, rows, n), F32),
        grid=(depth, n // tn),
        in_specs=[pl.BlockSpec((rows, d), lambda l, j: (0, 0)),
                  pl.BlockSpec((1, d, tn), lambda l, j: (l, 0, j)),
                  pl.BlockSpec((1, 1, tn), lambda l, j: (l, 0, j))],
        out_specs=pl.BlockSpec((1, rows, tn), lambda l, j: (l, 0, j)),
        compiler_params=_cparams("arbitrary", "arbitrary"),
        name="modulation",
    )(cvec, w_ada, b_ada.reshape(depth, 1, n))
    return out.reshape(depth, rows, N_MOD, D_MODEL)


FFN_CHUNK = 256


def _ffn_kernel(x_ref, mod_ref, gpre_ref, gpost_ref, wup_ref, wdn_ref, o_ref, *, sub):
    x = x_ref[0]
    mod = mod_ref[0]
    u = _pre_mod(x, gpre_ref[...], mod, sub).astype(CDT)
    acc = jnp.zeros(x.shape, F32)
    for c in range(D_FF // FFN_CHUNK):
        lo = c * FFN_CHUNK
        a = _dot(u, wup_ref[0, :, lo:lo + FFN_CHUNK])
        b = _dot(u, wup_ref[0, :, D_FF + lo:D_FF + lo + FFN_CHUNK])
        h = (a * _sigmoid(a) * b).astype(CDT)
        acc = acc + _dot(h, wdn_ref[0, lo:lo + FFN_CHUNK, :])
    gate = mod[3 * sub + 2:3 * sub + 3, :]
    o_ref[0] = x + MACARON_W * gate * _rms(acc, gpost_ref[...])


def _ffn(x, mod, mod_row, gpre, gpost, wup, wdn, layer, sub):
    b, l, d = x.shape
    tm = min(512, l)
    return pl.pallas_call(
        functools.partial(_ffn_kernel, sub=sub),
        out_shape=jax.ShapeDtypeStruct(x.shape, F32),
        grid=(b, l // tm),
        in_specs=[pl.BlockSpec((1, tm, d), lambda i, j: (i, j, 0)),
                  pl.BlockSpec((1, N_MOD, d), lambda i, j: (mod_row(i), 0, 0)),
                  _resident((1, d)), _resident((1, d)),
                  _layer_resident(wup.shape, layer), _layer_resident(wdn.shape, layer)],
        out_specs=pl.BlockSpec((1, tm, d), lambda i, j: (i, j, 0)),
        compiler_params=_cparams("parallel", "parallel"),
        name="ffn_sublayer",
    )(x, mod, gpre, gpost, wup, wdn)


def _rope(x, tabs, half):
    n = x.shape[1] // LANE
    cos, sin_up, sin_dn = [t if n == 1 else jnp.concatenate([t] * n, axis=1) for t in tabs]
    return x * cos + pltpu.roll(x, half, 1) * sin_up + pltpu.roll(x, x.shape[1] - half, 1) * sin_dn


def _inproj_kernel(x_ref, mod_ref, gpre_ref, wbig_ref, gkv_ref, gq_ref, wukv_ref, wuq_ref,
                   cm_ref, smu_ref, smd_ref, cw_ref, swu_ref, swd_ref,
                   qm_ref, km_ref, vm_ref, qw_ref, kw_ref, vw_ref, s5_ref, hy_ref, s5_scr):
    u = _pre_mod(x_ref[0], gpre_ref[...], mod_ref[0], 1).astype(CDT)

    def seg(off, n):
        return _dot(u, wbig_ref[:, off:off + n])

    rope_m = (cm_ref[...], smu_ref[...], smd_ref[...])
    rope_w = (cw_ref[...], swu_ref[...], swd_ref[...])
    kr = _rope(seg(O_KR, LANE), rope_m, MLA_ROPE // 2)
    ckv = seg(O_CKV, LANE)
    kvn = _rms(ckv, gkv_ref[...]).astype(CDT)
    kv = _dot(kvn, wukv_ref[...])
    ones_lane = (lax.broadcasted_iota(jnp.int32, (1, LANE), 1) == MLA_V).astype(F32)
    for h in range(MLA_HEADS):
        km_ref[0, h] = (kv[:, h * LANE:(h + 1) * LANE] + kr).astype(CDT)
        vm_ref[0, h] = (kv[:, (MLA_HEADS + h) * LANE:(MLA_HEADS + h + 1) * LANE] + ones_lane).T.astype(CDT)
    cq = seg(O_CQ, 2 * LANE)
    cqn = (cq * lax.rsqrt(jnp.sum(cq * cq, axis=-1, keepdims=True) * (1.0 / MLA_Q_LORA) + EPS)
           * gq_ref[...]).astype(CDT)
    qq = _rope(_dot(cqn, wuq_ref[...]), rope_m, MLA_ROPE // 2) * (MLA_SCALE * LOG2E)
    for h in range(MLA_HEADS):
        qm_ref[0, h] = qq[:, h * LANE:(h + 1) * LANE].T.astype(CDT)
    sq = _rope(seg(O_SQ, SWA_HEADS * LANE), rope_w, SWA_HD // 2) * (SWA_SCALE * LOG2E)
    for h in range(SWA_HEADS):
        qw_ref[0, h] = sq[:, h * LANE:(h + 1) * LANE].T.astype(CDT)
    sk = _rope(seg(O_SK, SWA_KV_HEADS * LANE), rope_w, SWA_HD // 2)
    sv = seg(O_SV, SWA_KV_HEADS * LANE)
    for h in range(SWA_KV_HEADS):
        kw_ref[0, h] = sk[:, h * LANE:(h + 1) * LANE].astype(CDT)
        vw_ref[0, h] = (sv[:, h * LANE:(h + 1) * LANE] + ones_lane).T.astype(CDT)
    s5 = seg(O_S5, S5_W)
    nchunk = s5_scr.shape[1] // S5_T
    for half in range(S5_W // LANE):
        s5_scr[half] = s5[:, half * LANE:(half + 1) * LANE]
        for sig in range(S5_T):
            s5_ref[half, 0, :, sig * LANE:(sig + 1) * LANE] = (
                s5_scr[half, pl.ds(sig, nchunk, stride=S5_T), :].astype(CDT))
    hy_ref[0] = seg(O_HY, (HY_ORDER + 1) * HY_W).astype(CDT)


def _inproj(x, mod, mod_row, gpre, wp, rope):
    b, l, d = x.shape
    tm = min(512, l)
    head = lambda n: jax.ShapeDtypeStruct((b, n, l, LANE), CDT)
    head_spec = lambda n: pl.BlockSpec((1, n, tm, LANE), lambda i, j: (i, 0, j, 0))
    tab_spec = pl.BlockSpec((tm, LANE), lambda i, j: (j, 0))
    row_spec = lambda n: pl.BlockSpec((1, tm, n), lambda i, j: (i, j, 0))
    head_t = lambda n: jax.ShapeDtypeStruct((b, n, LANE, l), CDT)
    head_t_spec = lambda n: pl.BlockSpec((1, n, LANE, tm), lambda i, j: (i, 0, 0, j))
    return pl.pallas_call(
        _inproj_kernel,
        out_shape=(head_t(MLA_HEADS), head(MLA_HEADS), head_t(MLA_HEADS), head_t(SWA_HEADS), head(SWA_KV_HEADS),
                   head_t(SWA_KV_HEADS), jax.ShapeDtypeStruct((S5_W // LANE, b, l // S5_T, S5_T * LANE), CDT),
                   jax.ShapeDtypeStruct((b, l, (HY_ORDER + 1) * HY_W), CDT)),
        grid=(b, l // tm),
        in_specs=[row_spec(d),
                  pl.BlockSpec((1, N_MOD, d), lambda i, j: (mod_row(i), 0, 0)),
                  _resident((1, d)), _resident(wp['w_big'].shape),
                  _resident((1, LANE)), _resident((1, 2 * LANE)),
                  _resident(wp['w_ukv'].shape), _resident(wp['w_uq'].shape),
                  ] + [tab_spec] * len(rope),
        out_specs=(head_t_spec(MLA_HEADS), head_spec(MLA_HEADS), head_t_spec(MLA_HEADS), head_t_spec(SWA_HEADS),
                   head_spec(SWA_KV_HEADS), head_t_spec(SWA_KV_HEADS),
                   pl.BlockSpec((S5_W // LANE, 1, tm // S5_T, S5_T * LANE), lambda i, j: (0, i, j, 0)),
                   row_spec((HY_ORDER + 1) * HY_W)),
        scratch_shapes=[pltpu.VMEM((S5_W // LANE, tm, LANE), F32)],
        compiler_params=_cparams("parallel", "parallel"),
        name="premod_inproj",
    )(x, mod, gpre, wp['w_big'], wp['g_kv'], wp['g_q'], wp['w_ukv'], wp['w_uq'], *rope)


MLA_KEY_CHUNK = 512
MLA_VROWS = 80


def _mla_scores(qt_ref, k_refs, s_buf, m_buf):
    qt = qt_ref[0, 0]
    m, off = None, 0
    for k_ref in k_refs:
        n = k_ref.shape[2]
        kc = min(MLA_KEY_CHUNK, n)
        for c in range(n // kc):
            s = _dot(k_ref[0, 0, c * kc:(c + 1) * kc, :], qt)
            s_buf[off + c * kc:off + (c + 1) * kc, :] = s
            cmax = s.max(axis=0, keepdims=True)
            m = cmax if m is None else jnp.maximum(m, cmax)
        off += n
    m_buf[...] = m


def _mla_values(vt_refs, s_buf, m_buf, o_ref):
    m = m_buf[...]
    acc, off = None, 0
    for vt_ref in vt_refs:
        n = vt_ref.shape[3]
        kc = min(MLA_KEY_CHUNK, n)
        for c in range(n // kc):
            p = jnp.exp2(s_buf[off + c * kc:off + (c + 1) * kc, :] - m).astype(CDT)
            pv = _dot(vt_ref[0, 0, 0:MLA_VROWS, c * kc:(c + 1) * kc], p)
            acc = pv if acc is None else acc + pv
        off += n
    o = acc / acc[MLA_V:MLA_V + 1, :]
    o = jnp.concatenate([o, jnp.zeros((LANE - MLA_VROWS, o.shape[1]), F32)], axis=0)
    o_ref[0] = o.T.astype(CDT)


def _mla_kernel(*refs, n_src, ntile):
    qt_ref, o_ref = refs[0], refs[1 + 2 * n_src]
    k_refs = [refs[1 + 2 * i] for i in range(n_src)]
    vt_refs = [refs[2 + 2 * i] for i in range(n_src)]
    s_bufs = refs[2 + 2 * n_src:4 + 2 * n_src]
    m_bufs = refs[4 + 2 * n_src:6 + 2 * n_src]
    t = pl.program_id(2)
    odd = t % 2 == 1

    @pl.when(t == 0)
    def _():
        _mla_scores(qt_ref, k_refs, s_bufs[0], m_bufs[0])

    @pl.when((t > 0) & (t < ntile) & odd)
    def _():
        _mla_scores(qt_ref, k_refs, s_bufs[1], m_bufs[1])
        _mla_values(vt_refs, s_bufs[0], m_bufs[0], o_ref)

    @pl.when((t > 0) & (t < ntile) & jnp.logical_not(odd))
    def _():
        _mla_scores(qt_ref, k_refs, s_bufs[0], m_bufs[0])
        _mla_values(vt_refs, s_bufs[1], m_bufs[1], o_ref)

    @pl.when(t == ntile)
    def _():
        _mla_values(vt_refs, s_bufs[(ntile - 1) % 2], m_bufs[(ntile - 1) % 2], o_ref)


def _mla(qt, kvs):
    b, h, _, l = qt.shape
    tq = min(512, l)
    ntile = l // tq
    in_specs = [pl.BlockSpec((1, 1, LANE, tq), lambda i, j, t: (i, j, 0, jnp.minimum(t, ntile - 1)))]
    args = [qt]
    for k, vt in kvs:
        n = k.shape[2]
        in_specs += [pl.BlockSpec((1, 1, n, LANE), lambda i, j, t: (i, j, 0, 0)),
                     pl.BlockSpec((1, 1, LANE, n), lambda i, j, t: (i, j, 0, 0))]
        args += [k, vt]
    nk = sum(k.shape[2] for k, _ in kvs)
    return pl.pallas_call(
        functools.partial(_mla_kernel, n_src=len(kvs), ntile=ntile),
        out_shape=jax.ShapeDtypeStruct((b, l, h * LANE), CDT),
        grid=(b, h, ntile + 1),
        in_specs=in_specs,
        out_specs=pl.BlockSpec((1, tq, LANE), lambda i, j, t: (i, jnp.maximum(t - 1, 0), j)),
        scratch_shapes=[pltpu.VMEM((nk, tq), F32), pltpu.VMEM((nk, tq), F32),
                        pltpu.VMEM((1, tq), F32), pltpu.VMEM((1, tq), F32)],
        compiler_params=_cparams("parallel", "parallel", "arbitrary"),
        name="mla_attention",
    )(*args)


def _swa_kernel(*refs, band, tq, nblk):
    sink_ref, qt_ref = refs[0], refs[1]
    o_ref = refs[-1]
    t = pl.program_id(1)
    g = SWA_HEADS // SWA_KV_HEADS
    lane = lax.broadcasted_iota(jnp.int32, (1, g * tq), 1)
    qi = jnp.where(lane < tq, lane, lane - tq)
    jp = lax.broadcasted_iota(jnp.int32, (WINDOW, 1), 0)
    jc = lax.broadcasted_iota(jnp.int32, (tq, 1), 0)
    for kv in range(SWA_KV_HEADS):
        qt = jnp.concatenate([qt_ref[0, kv * g + i] for i in range(g)], axis=-1)
        snk = jnp.where(lane < tq, sink_ref[kv * g], sink_ref[kv * g + 1]) * LOG2E
        ss, vts = [], []
        if band:
            kp, kc, kn, vtp, vtc, vtn, kx, vtx = [r[0, kv] for r in refs[2:10]]
            s_p = jnp.where((jp >= qi) & (t > 0), _dot(kp, qt), NEG_INF)
            s_c = jnp.where(jnp.abs(qi - jc) <= WINDOW, _dot(kc, qt), NEG_INF)
            s_n = jnp.where((jp <= qi - (tq - WINDOW)) & (t < nblk - 1), _dot(kn, qt), NEG_INF)
            ss += [s_p, s_c, s_n]
            vts += [vtp, vtc, vtn]
        else:
            kx, vtx = [r[0, kv] for r in refs[2:4]]
        ss.append(_dot(kx, qt))
        vts.append(vtx)
        m = snk
        for s in ss:
            m = jnp.maximum(m, s.max(axis=0, keepdims=True))
        acc = None
        for s, vt in zip(ss, vts):
            pv = _dot(vt[0:MLA_VROWS, :], jnp.exp2(s - m).astype(CDT))
            acc = pv if acc is None else acc + pv
        o = acc / (acc[SWA_HD:SWA_HD + 1, :] + jnp.exp2(snk - m))
        o = jnp.concatenate([o, jnp.zeros((LANE - MLA_VROWS, g * tq), F32)], axis=0)
        for i in range(g):
            o_ref[0, :, (kv * g + i) * LANE:(kv * g + i + 1) * LANE] = o[:, i * tq:(i + 1) * tq].T.astype(CDT)


def _swa(sink, qt, k, vt, kx, vtx, band):
    b, _, _, l = qt.shape
    hk = SWA_KV_HEADS
    tq = min(512, l)
    nblk = l // tq
    r = tq // WINDOW
    nw = l // WINDOW
    in_specs = [pl.BlockSpec(memory_space=pltpu.SMEM),
                pl.BlockSpec((1, SWA_HEADS, LANE, tq), lambda i, t: (i, 0, 0, t))]
    args = [sink, qt]
    if band:
        prev_i = lambda t: jnp.maximum(t * r - 1, 0)
        next_i = lambda t: jnp.minimum((t + 1) * r, nw - 1)
        in_specs += [pl.BlockSpec((1, hk, WINDOW, LANE), lambda i, t: (i, 0, prev_i(t), 0)),
                     pl.BlockSpec((1, hk, tq, LANE), lambda i, t: (i, 0, t, 0)),
                     pl.BlockSpec((1, hk, WINDOW, LANE), lambda i, t: (i, 0, next_i(t), 0)),
                     pl.BlockSpec((1, hk, LANE, WINDOW), lambda i, t: (i, 0, 0, prev_i(t))),
                     pl.BlockSpec((1, hk, LANE, tq), lambda i, t: (i, 0, 0, t)),
                     pl.BlockSpec((1, hk, LANE, WINDOW), lambda i, t: (i, 0, 0, next_i(t)))]
        args += [k, k, k, vt, vt, vt]
    nx = kx.shape[2]
    in_specs += [pl.BlockSpec((1, hk, nx, LANE), lambda i, t: (i, 0, 0, 0)),
                 pl.BlockSpec((1, hk, LANE, nx), lambda i, t: (i, 0, 0, 0))]
    args += [kx, vtx]
    return pl.pallas_call(
        functools.partial(_swa_kernel, band=band, tq=tq, nblk=nblk),
        out_shape=jax.ShapeDtypeStruct((b, l, SWA_HEADS * LANE), CDT),
        grid=(b, nblk),
        in_specs=in_specs,
        out_specs=pl.BlockSpec((1, tq, SWA_HEADS * LANE), lambda i, t: (i, t, 0)),
        compiler_params=_cparams("parallel", "parallel"),
        name="swa_attention",
    )(*args)


def _shortconv_kernel(x_ref, w_ref, b_ref, o_ref):
    x = x_ref[0].astype(F32)
    l = x.shape[0]
    t = lax.broadcasted_iota(jnp.int32, (l, 1), 0)
    prev = jnp.where(t == 0, 0.0, pltpu.roll(x, 1, 0))
    nxt = jnp.where(t == l - 1, 0.0, pltpu.roll(x, l - 1, 0))
    y = b_ref[...] + prev * w_ref[0:1, :] + x * w_ref[1:2, :] + nxt * w_ref[2:3, :]
    o_ref[0] = y.T.astype(CDT)


def _shortconv_t(x, w, bias):
    b, l, c = x.shape
    tc = 256
    return pl.pallas_call(
        _shortconv_kernel,
        out_shape=jax.ShapeDtypeStruct((b, c, l), CDT),
        grid=(b, c // tc),
        in_specs=[pl.BlockSpec((1, l, tc), lambda i, j: (i, 0, j)),
                  pl.BlockSpec((SHORT_K, tc), lambda i, j: (0, j)),
                  pl.BlockSpec((1, tc), lambda i, j: (0, j))],
        out_specs=pl.BlockSpec((1, tc, l), lambda i, j: (i, j, 0)),
        compiler_params=_cparams("parallel", "parallel"),
        name="hyena_shortconv",
    )(x, w, bias)


def _swap(x):
    return jnp.concatenate([x[..., LANE:], x[..., :LANE]], axis=-1)


def _comb(pq, n):
    p, q = pq[:, :n], pq[:, n:]
    return jnp.concatenate([p[..., :LANE] - q[..., LANE:], p[..., LANE:] + q[..., :LANE]], axis=-1)


def _hyena_kernel(v_ref, g1_ref, g2_ref, f1_ref, c1_ref, tw_ref, f2_ref, f2c_ref, ka_ref, kb_ref, bias_ref,
                  o_ref, *, ct, na, kin):
    def load(ref):
        return jnp.concatenate([ref[0, 0], ref[0, 1]], axis=-1).astype(F32)

    z = load(v_ref)
    gates = (g1_ref, g2_ref)
    twa, twb = tw_ref[0], tw_ref[1]
    for o in range(HY_ORDER):
        a = _comb(_bdot(f1_ref[...], z.astype(CDT)), na)
        a = a * twa + _swap(a) * twb
        x = _dot(a.reshape(ct * na, 2 * LANE).astype(CDT), f2_ref[...])
        y = x * ka_ref[o].reshape(ct * na, 2 * LANE) + _swap(x) * kb_ref[o].reshape(ct * na, 2 * LANE)
        bq = _dot(y.astype(CDT), f2c_ref[...]).reshape(ct, na, 2 * LANE)
        bq = bq * twa - _swap(bq) * twb
        yt = _comb(_bdot(c1_ref[...], bq.astype(CDT)), kin)
        z = load(gates[o]) * (yt + bias_ref[o] * z)
    o_ref[0, 0] = z[..., :LANE].astype(CDT)
    o_ref[0, 1] = z[..., LANE:].astype(CDT)


def _hyena_tables(na, kin, ct):
    f1, tw, f2, f2c, c1 = _hyena_dft_np(na, kin)
    bc = lambda m: jnp.broadcast_to(jnp.asarray(m, F32).astype(CDT)[None], (ct,) + m.shape)
    return (bc(f1), bc(c1), jnp.asarray(tw, F32), jnp.asarray(f2, F32).astype(CDT),
            jnp.asarray(f2c, F32).astype(CDT))


def _hyena_dft_np(na, kin):
    n = na * LANE
    ka = np.arange(na)[:, None]
    a = np.arange(kin)[None, :]
    ang1 = -2.0 * np.pi * ((ka * a) % na) / na
    f1 = np.concatenate([np.cos(ang1), np.sin(ang1)], axis=0)
    c1 = np.concatenate([np.cos(ang1).T, -np.sin(ang1).T], axis=0)
    bb = np.arange(LANE)[None, :]
    angt = -2.0 * np.pi * ((ka * bb) % n) / n
    tr, ti = np.cos(angt), np.sin(angt)
    tw = np.stack([np.concatenate([tr, tr], 1), np.concatenate([-ti, ti], 1)])
    b2 = np.arange(LANE)
    ang2 = -2.0 * np.pi * ((b2[:, None] * b2[None, :]) % LANE) / LANE
    fr, fi = np.cos(ang2), np.sin(ang2)
    f2 = np.block([[fr, fi], [-fi, fr]])
    f2c = np.block([[fr, -fi], [fi, fr]])
    return f1, tw, f2, f2c, c1


def _hyena_filter_taps(l, hp):
    t = jnp.linspace(0.0, 1.0, l, dtype=F32)[None, :]
    bands = (HY_EMB - 1) // 2
    w = 2.0 * math.pi * jnp.arange(l, dtype=F32) / l
    fr = jnp.linspace(1e-4, bands - 1, bands, dtype=F32)
    ang = fr[:, None] * w[None, :]
    z = jnp.concatenate([t, jnp.cos(ang), -jnp.sin(ang)], axis=0)
    freq = hp['hy_f_freq']
    hi = lax.Precision.HIGHEST
    h = jnp.sin(freq[0][:, None] * (jnp.dot(hp['hy_f_w1'].T, z, precision=hi) + hp['hy_f_b1'][:, None]))
    h = jnp.sin(freq[1][:, None] * (jnp.dot(hp['hy_f_w2'].T, h, precision=hi) + hp['hy_f_b2'][:, None]))
    h = jnp.dot(hp['hy_f_w3'].T, h, precision=hi).reshape(HY_ORDER, 2, HY_W, l)
    deltas = jnp.abs(jnp.linspace(math.log(HY_DECAY_TARGET) / HY_SLOW_PCT,
                                  math.log(HY_DECAY_TARGET) / HY_FAST_PCT, HY_W, dtype=F32))
    h = h * jnp.exp(-t * deltas[:, None])
    lag0 = (jnp.arange(l) > 0).astype(F32)
    return h[:, 0].reshape(HY_ORDER * HY_W, l), (h[:, 1] * lag0).reshape(HY_ORDER * HY_W, l)


def _split(x):
    hi = x.astype(CDT)
    return hi, (x - hi.astype(F32)).astype(CDT)


def _hyfilt_kernel(hf_ref, hb_ref, f1h_ref, f1l_ref, tw_ref, f2h_ref, f2l_ref, ka_ref, kb_ref, *, rt, na, inv_n):
    xh, xl = _split(jnp.concatenate([hf_ref[...], hb_ref[...]], axis=-1))
    f1h, f1l = f1h_ref[...], f1l_ref[...]
    pq = _bdot(f1h, xh) + _bdot(f1l, xh) + _bdot(f1h, xl)
    p, q = pq[:, :na], pq[:, na:]
    a = jnp.concatenate([jnp.concatenate([p[..., :LANE], q[..., :LANE]], -1),
                         jnp.concatenate([p[..., LANE:], q[..., LANE:]], -1)], axis=0)
    a = a * tw_ref[0] + _swap(a) * tw_ref[1]
    ah, al = _split(a.reshape(2 * rt * na, 2 * LANE))
    x = (_dot(ah, f2h_ref[...]) + _dot(al, f2h_ref[...]) + _dot(ah, f2l_ref[...])).reshape(2 * rt, na, 2 * LANE)
    xf, xb = x[:rt], x[rt:]
    kr = (xf[..., :LANE] + xb[..., :LANE]) * inv_n
    ki = (xf[..., LANE:] - xb[..., LANE:]) * inv_n
    ka_ref[...] = jnp.concatenate([kr, kr], -1)
    kb_ref[...] = jnp.concatenate([-ki, ki], -1)


def _hyena_filter_spectrum(hf, hb, na, kin):
    rows, l = hf.shape
    lp = kin * LANE
    if lp != l:
        hf, hb = [jnp.pad(h, ((0, 0), (0, lp - l))) for h in (hf, hb)]
    hf, hb = hf.reshape(rows, kin, LANE), hb.reshape(rows, kin, LANE)
    rt = 16
    f1, tw, f2 = _hyena_dft_np(na, kin)[:3]
    f1 = jnp.broadcast_to(jnp.asarray(f1, F32)[None], (rt,) + f1.shape)
    f1h, f1l = _split(f1)
    f2h, f2l = _split(jnp.asarray(f2, F32))
    tw = jnp.asarray(tw, F32)
    ka, kb = pl.pallas_call(
        functools.partial(_hyfilt_kernel, rt=rt, na=na, inv_n=1.0 / (na * LANE)),
        out_shape=(jax.ShapeDtypeStruct((rows, na, 2 * LANE), F32),) * 2,
        grid=(rows // rt,),
        in_specs=[pl.BlockSpec((rt, kin, LANE), lambda i: (i, 0, 0)),
                  pl.BlockSpec((rt, kin, LANE), lambda i: (i, 0, 0)),
                  _resident(f1h.shape), _resident(f1l.shape), _resident(tw.shape),
                  _resident(f2h.shape), _resident(f2l.shape)],
        out_specs=(pl.BlockSpec((rt, na, 2 * LANE), lambda i: (i, 0, 0)),) * 2,
        compiler_params=_cparams("parallel"),
        name="hyena_filter_spectrum",
    )(hf, hb, f1h, f1l, tw, f2h, f2l)
    shape = (HY_ORDER, HY_W, na, 2 * LANE)
    return ka.reshape(shape), kb.reshape(shape)


def _hyena(hy, hp, l_true):
    b, l, _ = hy.shape
    ut = _shortconv_t(hy, hp['hy_conv_w'], hp['hy_conv_b'].reshape(1, -1))
    kin = max(l // LANE, 16)
    na = 2 * kin
    lp = kin * LANE
    if lp != l:
        ut = jnp.pad(ut, ((0, 0), (0, 0), (0, lp - l)))
    ut = ut.reshape(b // 2, 2, (HY_ORDER + 1) * HY_W, kin, LANE)
    ct = 32
    nc = HY_W // ct
    f1, c1, tw, f2, f2c = _hyena_tables(na, kin, ct)
    ka, kb = hp['ka'], hp['kb']
    bias = jnp.broadcast_to(hp['hy_bias'].reshape(HY_ORDER, HY_W, 1, 1), (HY_ORDER, HY_W, 1, 2 * LANE))
    blk = lambda off: pl.BlockSpec((1, 2, ct, kin, LANE), lambda c, p: (p, 0, c + off * nc, 0, 0))
    out = pl.pallas_call(
        functools.partial(_hyena_kernel, ct=ct, na=na, kin=kin),
        out_shape=jax.ShapeDtypeStruct((b // 2, 2, HY_W, kin, LANE), CDT),
        grid=(nc, b // 2),
        in_specs=[blk(0), blk(1), blk(2),
                  _resident(f1.shape), _resident(c1.shape), _resident(tw.shape),
                  _resident(f2.shape), _resident(f2c.shape),
                  pl.BlockSpec((HY_ORDER, ct, na, 2 * LANE), lambda c, p: (0, c, 0, 0)),
                  pl.BlockSpec((HY_ORDER, ct, na, 2 * LANE), lambda c, p: (0, c, 0, 0)),
                  pl.BlockSpec((HY_ORDER, ct, 1, 2 * LANE), lambda c, p: (0, c, 0, 0))],
        out_specs=pl.BlockSpec((1, 2, ct, kin, LANE), lambda c, p: (p, 0, c, 0, 0)),
        compiler_params=_cparams("parallel", "arbitrary"),
        name="hyena_longconv",
    )(ut, ut, ut, f1, c1, tw, f2, f2c, ka, kb, bias)
    out = out.reshape(b, HY_W, lp)
    return out if lp == l else out[:, :, :l]


def _s5_kernel(u_ref, psel_ref, pselt_ref, tloc_ref, wx_ref, wout_ref, d_ref, h0_ref, y_ref, hfin_ref,
               x_scr, h_scr, up_scr, *, nchunk, nb):
    w = S5_T * S5_GC
    gb = max(1, min(nb, S5_BLOCK_ROWS // nchunk))
    for b0 in range(0, nb, gb):
        rs = slice(b0 * nchunk, (b0 + gb) * nchunk)
        up = _dot(u_ref[0, rs, :], psel_ref[0]).astype(CDT)
        up_scr[rs, :] = up
        xb = _dot(up, wx_ref[0])
        for i in range(gb):
            for k in range(4):
                x_scr[k, pl.ds(b0 + i, nchunk, stride=nb), :] = xb[i * nchunk:(i + 1) * nchunk,
                                                                   k * LANE:(k + 1) * LANE]
    d = d_ref[0]
    dfr, dfi, dbr, dbi = d[0:1], d[1:2], d[2:3], d[3:4]

    def body(j, carry):
        hr, hi, gr, gi = carry
        rf = pl.multiple_of(j * nb, nb)
        rb = pl.multiple_of((nchunk - 1 - j) * nb, nb)
        h_scr[0, pl.ds(rf, nb), :] = hr
        h_scr[1, pl.ds(rf, nb), :] = hi
        h_scr[2, pl.ds(rb, nb), :] = gr
        h_scr[3, pl.ds(rb, nb), :] = gi
        xr = x_scr[0, pl.ds(rf, nb), :]
        xi = x_scr[1, pl.ds(rf, nb), :]
        yr = x_scr[2, pl.ds(rb, nb), :]
        yi = x_scr[3, pl.ds(rb, nb), :]
        return (dfr * hr - dfi * hi + xr, dfr * hi + dfi * hr + xi,
                dbr * gr - dbi * gi + yr, dbr * gi + dbi * gr + yi)

    fin = lax.fori_loop(0, nchunk, body, tuple(h0_ref[0, k] for k in range(4)))
    for k in range(4):
        hfin_ref[0, k] = fin[k]
    r = pl.program_id(1)
    for b0 in range(0, nb, gb):
        rs = slice(b0 * nchunk, (b0 + gb) * nchunk)
        hs = jnp.concatenate(
            [jnp.concatenate([h_scr[k, pl.ds(b0 + i, nchunk, stride=nb), :] for k in range(4)], axis=-1)
             for i in range(gb)], axis=0).astype(CDT)
        y0 = _dot(up_scr[rs, :w], tloc_ref[0]) + _dot(hs, wout_ref[0, :, :w])
        y1 = _dot(up_scr[rs, w:], tloc_ref[1]) + _dot(hs, wout_ref[0, :, w:])
        contrib = _dot(jnp.concatenate([y0, y1], axis=-1).astype(CDT), pselt_ref[0]).astype(CDT)

        @pl.when(r == 0)
        def _():
            y_ref[0, rs, :] = contrib

        @pl.when(r > 0)
        def _():
            y_ref[0, rs, :] = y_ref[0, rs, :] + contrib


S5_HALF_GROUPS = LANE // S5_GC
S5_HALF_PAIRS = S5_HALF_GROUPS // 2


def _s5_select():
    ri = jnp.arange(S5_T * LANE)
    r_sig, r_grp, r_ch = ri // LANE, (ri % LANE) // S5_GC, ri % S5_GC
    ci = jnp.arange(2 * S5_T * S5_GC)
    c_grp, c_sig, c_ch = ci // (S5_T * S5_GC), (ci % (S5_T * S5_GC)) // S5_GC, ci % S5_GC
    same = (r_sig[:, None] == c_sig[None, :]) & (r_ch[:, None] == c_ch[None, :])
    sel = jnp.stack([(same & (r_grp[:, None] == 2 * q + c_grp[None, :])) for q in range(S5_HALF_PAIRS)])
    sel = sel.astype(CDT)
    return sel, jnp.swapaxes(sel, 1, 2)


def _s5(u, ops, h0):
    nhalf, b, nchunk, wide = u.shape
    w = S5_T * S5_GC
    rows = nchunk * b
    uh = u.reshape(nhalf, rows, wide)
    psel, pselt = _s5_select()
    pair = lambda h, r: h * S5_HALF_PAIRS + r
    y, hfin = pl.pallas_call(
        functools.partial(_s5_kernel, nchunk=nchunk, nb=b),
        out_shape=(jax.ShapeDtypeStruct((nhalf, rows, wide), CDT),
                   jax.ShapeDtypeStruct((S5_GROUPS // 2, 4, b, LANE), F32)),
        grid=(nhalf, S5_HALF_PAIRS),
        in_specs=[pl.BlockSpec((1, rows, wide), lambda h, r: (h, 0, 0), pipeline_mode=pl.Buffered(1)),
                  pl.BlockSpec((1, wide, 2 * w), lambda h, r: (r, 0, 0)),
                  pl.BlockSpec((1, 2 * w, wide), lambda h, r: (r, 0, 0)),
                  pl.BlockSpec((2, w, w), lambda h, r: (pair(h, r), 0, 0)),
                  pl.BlockSpec((1, 2 * w, 2 * w), lambda h, r: (pair(h, r), 0, 0)),
                  pl.BlockSpec((1, 2 * w, 2 * w), lambda h, r: (pair(h, r), 0, 0)),
                  pl.BlockSpec((1, 4, LANE), lambda h, r: (pair(h, r), 0, 0)),
                  pl.BlockSpec((1, 4, b, LANE), lambda h, r: (pair(h, r), 0, 0, 0))],
        out_specs=(pl.BlockSpec((1, rows, wide), lambda h, r: (h, 0, 0)),
                   pl.BlockSpec((1, 4, b, LANE), lambda h, r: (pair(h, r), 0, 0, 0))),
        scratch_shapes=[pltpu.VMEM((4, rows, LANE), F32), pltpu.VMEM((4, rows, LANE), F32),
                        pltpu.VMEM((rows, 2 * w), CDT)],
        compiler_params=_cparams("parallel", "arbitrary"),
        name="s5_chunked",
    )(uh, psel, pselt, ops['tloc'], ops['wx'], ops['wout'], ops['d16'], h0)
    return y.reshape(nhalf, b, nchunk, wide), hfin


def _s5_operators(p):
    g, pp, gc, t = S5_GROUPS, S5_P, S5_GC, S5_T
    npair = g // 2
    n = jnp.arange(t + 1, dtype=F32)[:, None, None]
    tops, wxs, wouts, d16 = [], [], [], []
    sig = jnp.arange(t)
    for d in range(2):
        a_re, a_im = p['s5_a_re'][d], p['s5_a_im'][d]
        dt = jnp.exp(p['s5_log_dt'][d])[:, None]
        mag1 = jnp.exp(dt * a_re)
        ab_re, ab_im = mag1 * jnp.cos(dt * a_im), mag1 * jnp.sin(dt * a_im)
        den = a_re * a_re + a_im * a_im
        f_re = ((ab_re - 1.0) * a_re + ab_im * a_im) / den
        f_im = (ab_im * a_re - (ab_re - 1.0) * a_im) / den
        mag = jnp.exp(n * (dt * a_re)[None])
        pr, pi = mag * jnp.cos(n * (dt * a_im)[None]), mag * jnp.sin(n * (dt * a_im)[None])
        b_re, b_im = p['s5_b_re'][d], p['s5_b_im'][d]
        bt_re = f_re[..., None] * b_re - f_im[..., None] * b_im
        bt_im = f_re[..., None] * b_im + f_im[..., None] * b_re
        c_re, c_im = p['s5_c_re'][d], p['s5_c_im'][d]
        ca_re = c_re[None] * pr[:, :, None, :] - c_im[None] * pi[:, :, None, :]
        ca_im = c_re[None] * pi[:, :, None, :] + c_im[None] * pr[:, :, None, :]
        hi = lax.Precision.HIGHEST
        lag_re = ca_re[:t] if d == 0 else ca_re[:t][::-1]
        lag_im = ca_im[:t] if d == 0 else ca_im[:t][::-1]
        tops.append(jnp.einsum('ngcp,gpk->gknc', lag_re, bt_re, precision=hi)
                    - jnp.einsum('ngcp,gpk->gknc', lag_im, bt_im, precision=hi))
        pwr, pwi = (pr[:t][::-1], pi[:t][::-1]) if d == 0 else (pr[:t], pi[:t])
        pwr, pwi = pwr.transpose(1, 0, 2)[:, :, None, :], pwi.transpose(1, 0, 2)[:, :, None, :]
        btr, bti = bt_re.transpose(0, 2, 1)[:, None], bt_im.transpose(0, 2, 1)[:, None]
        wxs.append(((pwr * btr - pwi * bti).reshape(g, t * gc, pp), (pwr * bti + pwi * btr).reshape(g, t * gc, pp)))
        out_re = ca_re[1:t + 1] if d == 0 else ca_re[1:t + 1][::-1]
        out_im = ca_im[1:t + 1] if d == 0 else ca_im[1:t + 1][::-1]
        wouts.append((out_re.transpose(1, 3, 0, 2).reshape(g, pp, t * gc),
                      -out_im.transpose(1, 3, 0, 2).reshape(g, pp, t * gc)))
        d16.append((pr[t], pi[t]))
    skip = jnp.eye(gc, dtype=F32)[None, :, None, :] * p['s5_d'].reshape(g, 1, 1, gc)
    kf, kb = tops
    kcomb = jnp.concatenate([kb[:, :, :t - 1], kf[:, :, 0:1] + kb[:, :, t - 1:t] + skip, kf[:, :, 1:]], axis=2)
    kcomb = kcomb.reshape(g, gc, (2 * t - 1) * gc)
    tloc = jnp.stack([kcomb[:, :, (t - 1 - s) * gc:(t - 1 - s) * gc + t * gc] for s in range(t)], axis=1)
    tloc = tloc.reshape(g, t * gc, t * gc)

    def pair_cols(m):
        m = m.reshape(npair, 2, m.shape[1], m.shape[2])
        z = jnp.zeros_like(m[:, 0])
        return jnp.concatenate([jnp.concatenate([m[:, 0], z], -1), jnp.concatenate([z, m[:, 1]], -1)], -2)

    wx = jnp.concatenate([pair_cols(wxs[0][0]), pair_cols(wxs[0][1]), pair_cols(wxs[1][0]), pair_cols(wxs[1][1])],
                         axis=-1)
    wout = jnp.concatenate([pair_cols(wouts[0][0]), pair_cols(wouts[0][1]), pair_cols(wouts[1][0]),
                            pair_cols(wouts[1][1])], axis=-2)
    dd = jnp.stack([d16[0][0], d16[0][1], d16[1][0], d16[1][1]], axis=0)
    dd = dd.reshape(4, npair, 2 * pp).transpose(1, 0, 2)
    return dict(tloc=tloc.astype(CDT), wx=wx.astype(CDT), wout=wout.astype(CDT), d16=dd)


def _merge_kernel(x_ref, mod_ref, gpre_ref, gpost_ref, omla_ref, ohy_ref, oswa_ref, ys5_ref,
                  wgate_ref, bgate_ref, wmla_ref, why_ref, wswa_ref, ws5_ref, gluw_ref, glub_ref, wout_ref, o_ref,
                  y_scr):
    x = x_ref[0]
    mod = mod_ref[0]
    u = _pre_mod(x, gpre_ref[...], mod, 1).astype(CDT)
    nchunk = y_scr.shape[1] // S5_T
    for sig in range(S5_T):
        for half in range(S5_W // LANE):
            y_scr[half, pl.ds(sig, nchunk, stride=S5_T), :] = (
                ys5_ref[half, 0, :, sig * LANE:(sig + 1) * LANE].astype(F32))
    y = jnp.concatenate([y_scr[half] for half in range(S5_W // LANE)], axis=-1)
    g = 0.5 * y * (1.0 + jnp.tanh(math.sqrt(2.0 / math.pi) * (y + 0.044715 * (y * y * y))))
    o_s5 = (g * _sigmoid(_dot(g.astype(CDT), gluw_ref[...]) + glub_ref[...])).astype(CDT)
    o_hy = ohy_ref[0].astype(F32).T.astype(CDT)
    outs = (omla_ref[0], o_hy, oswa_ref[0], o_s5)
    wbr = (wmla_ref, why_ref, wswa_ref, ws5_ref)
    m = jnp.zeros(x.shape, F32)
    for i in range(4):
        gate = _sigmoid(_dot(u, wgate_ref[0, i]) + bgate_ref[:, i * D_MODEL:(i + 1) * D_MODEL])
        m = m + gate * _dot(outs[i], wbr[i][...])
    f = _dot(m.astype(CDT), wout_ref[0])
    o_ref[0] = x + mod[5:6, :] * _rms(f, gpost_ref[...])


def _merge(x, mod, mod_row, gpre, gpost, o_mla, o_hy, o_swa, y_s5, wp, layer):
    b, l, d = x.shape
    tm = min(512, l)
    row_spec = lambda n: pl.BlockSpec((1, tm, n), lambda i, j: (i, j, 0))
    names = ('w_gate', 'b_gate', 'w_br_mla', 'w_br_hy', 'w_br_swa', 'w_br_s5', 'glu_w', 'glu_b', 'w_out')
    stacked = ('w_gate', 'w_out')
    wspec = lambda k: _layer_resident(wp[k].shape, layer) if k in stacked else _resident(wp[k].shape)
    return pl.pallas_call(
        _merge_kernel,
        out_shape=jax.ShapeDtypeStruct(x.shape, F32),
        grid=(b, l // tm),
        in_specs=[row_spec(d), pl.BlockSpec((1, N_MOD, d), lambda i, j: (mod_row(i), 0, 0)),
                  _resident((1, d)), _resident((1, d)),
                  row_spec(o_mla.shape[-1]), pl.BlockSpec((1, HY_W, tm), lambda i, j: (i, 0, j)),
                  row_spec(o_swa.shape[-1]),
                  pl.BlockSpec((S5_W // LANE, 1, tm // S5_T, S5_T * LANE), lambda i, j: (0, i, j, 0))]
        + [wspec(k) for k in names],
        out_specs=row_spec(d),
        scratch_shapes=[pltpu.VMEM((S5_W // LANE, tm, LANE), F32)],
        compiler_params=_cparams("parallel", "parallel"),
        name="merge_out",
    )(x, mod, gpre, gpost, o_mla, o_hy, o_swa, y_s5, *[wp[k] for k in names])


def _pad_cols(w, n):
    return jnp.pad(w, ((0, 0), (0, n - w.shape[1])))


def _pad_rows(w, n):
    return jnp.pad(w, ((0, n - w.shape[0]), (0, 0)))


def _prep_inproj(w_in, w_ukv, w_uq, g_kv, g_q):
    d = w_in.shape[0]
    zeros = lambda n: jnp.zeros((d, n), F32)
    krope = w_in[:, I_KROPE:I_KROPE + MLA_ROPE]
    kr = jnp.concatenate([zeros(MLA_NOPE), krope, zeros(LANE - MLA_NOPE - MLA_ROPE)], 1)
    ckv = w_in[:, I_CKV:I_CKV + MLA_KV_LORA]
    cq = _pad_cols(w_in[:, I_CQ:I_CQ + MLA_Q_LORA], 2 * LANE)

    def heads(w, nh):
        return jnp.concatenate([_pad_cols(w[:, h * SWA_HD:(h + 1) * SWA_HD], LANE) for h in range(nh)], 1)

    swq = w_in[:, I_SWQ:I_SWQ + SWA_HEADS * SWA_HD]
    swk = w_in[:, I_SWK:I_SWK + SWA_KV_HEADS * SWA_HD]
    swv = w_in[:, I_SWV:I_SWV + SWA_KV_HEADS * SWA_HD]
    w_big = jnp.concatenate([kr, ckv, cq, heads(swq, SWA_HEADS), heads(swk, SWA_KV_HEADS),
                             heads(swv, SWA_KV_HEADS),
                             w_in[:, I_S5:I_S5 + S5_W], w_in[:, I_HY:I_HY + (HY_ORDER + 1) * HY_W]], axis=1)
    assert w_big.shape[1] == N_BIG
    kvw = w_ukv.reshape(MLA_KV_LORA, MLA_HEADS, MLA_NOPE + MLA_V)
    kslots = [_pad_cols(kvw[:, h, :MLA_NOPE], LANE) for h in range(MLA_HEADS)]
    vslots = [_pad_cols(kvw[:, h, MLA_NOPE:], LANE) for h in range(MLA_HEADS)]
    w_ukv_p = jnp.concatenate(kslots + vslots, axis=1)
    qw = w_uq.reshape(MLA_Q_LORA, MLA_HEADS, MLA_NOPE + MLA_ROPE)
    qslots = [_pad_cols(qw[:, h], LANE) for h in range(MLA_HEADS)]
    w_uq_p = _pad_rows(jnp.concatenate(qslots, axis=1), 2 * LANE)
    return dict(w_big=w_big.astype(CDT), w_ukv=w_ukv_p.astype(CDT), w_uq=w_uq_p.astype(CDT),
                g_kv=g_kv.reshape(1, -1), g_q=_pad_cols(g_q.reshape(1, -1), 2 * LANE))


def _pad_head_rows(w, nh, hd):
    w = w.reshape(nh, hd, w.shape[-1])
    return jnp.pad(w, ((0, 0), (0, LANE - hd), (0, 0))).reshape(nh * LANE, -1)


def _rope_tables(n_tokens, use_rope):
    ones = jnp.ones((n_tokens, LANE), F32)
    zeros = jnp.zeros((n_tokens, LANE), F32)
    if not use_rope:
        return ones, zeros, zeros, ones, zeros, zeros

    def axial(rot_dim):
        rows = n_tokens // GRID_W
        r = jnp.repeat(jnp.arange(rows, dtype=F32), GRID_W)
        col = jnp.tile(jnp.arange(GRID_W, dtype=F32), rows)
        n_freq = rot_dim // 4
        freqs = ROPE_THETA ** (-jnp.arange(n_freq, dtype=F32) / n_freq)
        ang = jnp.concatenate([r[:, None] * freqs, col[:, None] * freqs], axis=-1)
        return jnp.cos(ang), jnp.sin(ang)

    c, s = axial(MLA_ROPE)
    pad = LANE - MLA_NOPE - MLA_ROPE
    half = MLA_ROPE // 2
    cm = jnp.concatenate([ones[:, :MLA_NOPE], c, c, ones[:, :pad]], 1)
    smu = jnp.concatenate([zeros[:, :MLA_NOPE + half], s, zeros[:, :pad]], 1)
    smd = jnp.concatenate([zeros[:, :MLA_NOPE], -s, zeros[:, :half + pad]], 1)
    c, s = axial(SWA_HD)
    half = SWA_HD // 2
    cw = jnp.concatenate([c, c, ones[:, :LANE - SWA_HD]], 1)
    swu = jnp.concatenate([zeros[:, :half], s, zeros[:, :LANE - SWA_HD]], 1)
    swd = jnp.concatenate([-s, zeros[:, :LANE - half]], 1)
    return cm, smu, smd, cw, swu, swd


def _hyena_params(p, l):
    kin = max(l // LANE, 16)
    na = 2 * kin
    hf, hb = _hyena_filter_taps(l, p)
    ka, kb = _hyena_filter_spectrum(hf, hb, na, kin)
    return dict(hy_conv_w=p['hy_conv_w'], hy_conv_b=p['hy_conv_b'], hy_bias=p['hy_bias'], ka=ka, kb=kb)


def kernel(x, c, ctx, c_ctx, w_ada, b_ada, norm_pre, norm_post, ffn1_up, ffn1_down, ffn2_up, ffn2_down,
           w_in, mla_q_norm, mla_kv_norm, mla_w_uq, mla_w_ukv, hy_conv_w, hy_conv_b, hy_f_w1, hy_f_b1,
           hy_f_freq, hy_f_w2, hy_f_b2, hy_f_w3, hy_bias, swa_sink, s5_a_re, s5_a_im, s5_log_dt,
           s5_b_re, s5_b_im, s5_c_re, s5_c_im, s5_d, s5_glu_w, s5_glu_b, w_gate, b_gate,
           w_br_mla, w_br_hy, w_br_swa, w_br_s5, w_out):
    stacked = dict(w_ada=w_ada, b_ada=b_ada, norm_pre=norm_pre, norm_post=norm_post,
                   ffn1_up=ffn1_up, ffn1_down=ffn1_down, ffn2_up=ffn2_up, ffn2_down=ffn2_down,
                   w_in=w_in, mla_q_norm=mla_q_norm, mla_kv_norm=mla_kv_norm, mla_w_uq=mla_w_uq,
                   mla_w_ukv=mla_w_ukv, hy_conv_w=hy_conv_w, hy_conv_b=hy_conv_b, hy_f_w1=hy_f_w1,
                   hy_f_b1=hy_f_b1, hy_f_freq=hy_f_freq, hy_f_w2=hy_f_w2, hy_f_b2=hy_f_b2, hy_f_w3=hy_f_w3,
                   hy_bias=hy_bias, swa_sink=swa_sink, s5_a_re=s5_a_re, s5_a_im=s5_a_im,
                   s5_log_dt=s5_log_dt, s5_b_re=s5_b_re, s5_b_im=s5_b_im, s5_c_re=s5_c_re,
                   s5_c_im=s5_c_im, s5_d=s5_d, s5_glu_w=s5_glu_w, s5_glu_b=s5_glu_b,
                   w_gate=w_gate, b_gate=b_gate, w_br_mla=w_br_mla, w_br_hy=w_br_hy,
                   w_br_swa=w_br_swa, w_br_s5=w_br_s5, w_out=w_out)
    depth = w_ada.shape[0]
    nb, seq, d = x.shape
    nctx = ctx.shape[1]
    assert nb % 2 == 0 and seq % 256 == 0 and nctx % 256 == 0

    rows = -(-(nb + 1) // 8) * 8
    cvec = jnp.zeros((rows, d), F32).at[:nb].set(c).at[nb].set(c_ctx)
    mods = _modulation(cvec, w_ada, b_ada)
    lat_row = lambda i: i
    ctx_row = lambda i: nb

    rope_l = _rope_tables(seq, True)
    rope_c = _rope_tables(nctx, False)
    h0_zero = jnp.zeros((S5_GROUPS // 2, 4, nb, LANE), F32)

    f1u, f1d, f2u, f2d = [w.astype(CDT) for w in (ffn1_up, ffn1_down, ffn2_up, ffn2_down)]
    w_gate_all, w_out_all = w_gate.astype(CDT), w_out.astype(CDT)

    s5_names = ('s5_a_re', 's5_a_im', 's5_log_dt', 's5_b_re', 's5_b_im', 's5_c_re', 's5_c_im', 's5_d')
    s5_ops_all = jax.vmap(_s5_operators)({k: stacked[k] for k in s5_names})

    xl, xc = x, ctx
    for l in range(depth):
        p = {name: arr[l] for name, arr in stacked.items()}
        mod = mods[l]
        ctx_out = l < depth - 1
        gpre = [p['norm_pre'][i].reshape(1, d) for i in range(N_SUB)]
        gpost = [p['norm_post'][i].reshape(1, d) for i in range(N_SUB)]
        wp_in = _prep_inproj(p['w_in'], p['mla_w_ukv'], p['mla_w_uq'], p['mla_kv_norm'], p['mla_q_norm'])
        wp_mg = dict(
            w_gate=w_gate_all,
            b_gate=p['b_gate'].reshape(1, -1),
            w_br_mla=_pad_head_rows(p['w_br_mla'], MLA_HEADS, MLA_V).astype(CDT),
            w_br_hy=p['w_br_hy'].astype(CDT),
            w_br_swa=_pad_head_rows(p['w_br_swa'], SWA_HEADS, SWA_HD).astype(CDT),
            w_br_s5=p['w_br_s5'].astype(CDT),
            glu_w=p['s5_glu_w'].astype(CDT), glu_b=p['s5_glu_b'].reshape(1, -1),
            w_out=w_out_all)
        s5_ops = {k: v[l] for k, v in s5_ops_all.items()}

        xl = _ffn(xl, mod, lat_row, gpre[0], gpost[0], f1u, f1d, l, 0)
        xc = _ffn(xc, mod, ctx_row, gpre[0], gpost[0], f1u, f1d, l, 0)

        qm_c, km_c, vm_c, qw_c, kw_c, vw_c, s5u_c, hy_c = _inproj(xc, mod, ctx_row, gpre[1], wp_in, rope_c)
        qm_l, km_l, vm_l, qw_l, kw_l, vw_l, s5u_l, hy_l = _inproj(xl, mod, lat_row, gpre[1], wp_in, rope_l)

        ys5_c, h_ctx = _s5(s5u_c, s5_ops, h0_zero)
        ys5_l, _ = _s5(s5u_l, s5_ops, h_ctx)
        o_mla = _mla(qm_l, [(km_l, vm_l), (km_c, vm_c)])
        o_swa = _swa(p['swa_sink'], qw_l, kw_l, vw_l, kw_c, vw_c, True)
        o_hy = _hyena(hy_l, _hyena_params(p, seq), seq)
        xl = _merge(xl, mod, lat_row, gpre[1], gpost[1], o_mla, o_hy, o_swa, ys5_l, wp_mg, l)
        xl = _ffn(xl, mod, lat_row, gpre[2], gpost[2], f2u, f2d, l, 2)
        if ctx_out:
            o_mla_c = _mla(qm_c, [(km_c, vm_c)])
            o_swa_c = _swa(p['swa_sink'], qw_c, kw_c, vw_c, kw_c, vw_c, False)
            o_hy_c = _hyena(hy_c, _hyena_params(p, nctx), nctx)
            xc = _merge(xc, mod, ctx_row, gpre[1], gpost[1], o_mla_c, o_hy_c, o_swa_c, ys5_c, wp_mg, l)
            xc = _ffn(xc, mod, ctx_row, gpre[2], gpost[2], f2u, f2d, l, 2)
    return xl
```

```python
import functools
import math

import numpy as np
import jax
import jax.numpy as jnp
from jax import lax
from jax.experimental import pallas as pl
from jax.experimental.pallas import tpu as pltpu

F32 = jnp.float32
CDT = jnp.bfloat16

D_MODEL = 1024
D_FF = 2816
N_SUB = 3
N_MOD = 3 * N_SUB
MACARON_W = 0.5
ROPE_THETA = 10000.0
GRID_W = 64
EPS = 1e-6
NEG_INF = -1e30

MLA_HEADS = 4
MLA_NOPE = 64
MLA_ROPE = 32
MLA_V = 64
MLA_Q_LORA = 192
MLA_KV_LORA = 128
MLA_SCALE = (MLA_NOPE + MLA_ROPE) ** -0.5
LOG2E = math.log2(math.e)

HY_W = 256
HY_ORDER = 2
HY_EMB = 33
HY_DECAY_TARGET = 1e-2
HY_FAST_PCT = 0.3
HY_SLOW_PCT = 1.5
SHORT_K = 3

SWA_HEADS = 4
SWA_KV_HEADS = 2
SWA_HD = 64
WINDOW = 128
SWA_SCALE = SWA_HD ** -0.5

S5_W = 256
S5_GC = 16
S5_GROUPS = S5_W // S5_GC
S5_P = 64
S5_BLOCK_ROWS = 512
S5_T = 16

LANE = 128
MXU = 256
VMEM_LIMIT = 56 * 1024 * 1024

_IN_SIZES = (MLA_KV_LORA, MLA_ROPE, SWA_KV_HEADS * SWA_HD, SWA_KV_HEADS * SWA_HD, S5_W, MLA_Q_LORA,
             SWA_HEADS * SWA_HD, (HY_ORDER + 1) * HY_W)
_IN_OFF = np.concatenate([[0], np.cumsum(_IN_SIZES)])
(I_CKV, I_KROPE, I_SWK, I_SWV, I_S5, I_CQ, I_SWQ, I_HY) = [int(v) for v in _IN_OFF[:-1]]

O_KR, O_CKV, O_CQ = 0, 128, 256
O_SQ, O_SK, O_SV = 512, 1024, 1280
O_S5, O_HY = 1536, 1792
N_BIG = O_HY + (HY_ORDER + 1) * HY_W


def _cparams(*sem):
    return pltpu.CompilerParams(dimension_semantics=sem, vmem_limit_bytes=VMEM_LIMIT)


def _resident(shape):
    nd = len(shape)
    return pl.BlockSpec(shape, lambda *_: (0,) * nd, pipeline_mode=pl.Buffered(1))


def _layer_resident(shape, layer):
    nd = len(shape) - 1
    return pl.BlockSpec((1,) + tuple(shape[1:]), lambda *_: (layer,) + (0,) * nd, pipeline_mode=pl.Buffered(1))


def _dot(a, b):
    return jnp.dot(a, b, preferred_element_type=F32)


def _dot_nt(a, b):
    return lax.dot_general(a, b, (((1,), (1,)), ((), ())), preferred_element_type=F32)


def _bdot(a, b):
    return lax.dot_general(a, b, (((2,), (1,)), ((0,), (0,))), preferred_element_type=F32)


def _rms(x, g):
    return x * lax.rsqrt(jnp.mean(x * x, axis=-1, keepdims=True) + EPS) * g


def _sigmoid(x):
    return 1.0 / (1.0 + jnp.exp(-x))


def _pre_mod(x, gpre, mod, sub):
    return _rms(x, gpre) * (1.0 + mod[3 * sub + 1:3 * sub + 2, :]) + mod[3 * sub:3 * sub + 1, :]


def _mod_kernel(c_ref, w_ref, b_ref, o_ref):
    c = c_ref[...]
    s = c * _sigmoid(c)
    w = w_ref[0]
    s_hi = s.astype(CDT)
    s_lo = (s - s_hi.astype(F32)).astype(CDT)
    w_hi = w.astype(CDT)
    w_lo = (w - w_hi.astype(F32)).astype(CDT)
    o_ref[0] = _dot(s_hi, w_hi) + _dot(s_hi, w_lo) + _dot(s_lo, w_hi) + b_ref[0]


def _modulation(cvec, w_ada, b_ada):
    depth, d, n = w_ada.shape
    rows = cvec.shape[0]
    tn = n // 8
    out = pl.pallas_call(
        _mod_kernel,
        out_shape=jax.ShapeDtypeStruct((depth, rows, n), F32),
        grid=(depth, n // tn),
        in_specs=[pl.BlockSpec((rows, d), lambda l, j: (0, 0)),
                  pl.BlockSpec((1, d, tn), lambda l, j: (l, 0, j)),
                  pl.BlockSpec((1, 1, tn), lambda l, j: (l, 0, j))],
        out_specs=pl.BlockSpec((1, rows, tn), lambda l, j: (l, 0, j)),
        compiler_params=_cparams("arbitrary", "arbitrary"),
        name="modulation",
    )(cvec, w_ada, b_ada.reshape(depth, 1, n))
    return out.reshape(depth, rows, N_MOD, D_MODEL)


FFN_CHUNK = 256


def _ffn_kernel(x_ref, mod_ref, gpre_ref, gpost_ref, wup_ref, wdn_ref, o_ref, *, sub):
    x = x_ref[0]
    mod = mod_ref[0]
    u = _pre_mod(x, gpre_ref[...], mod, sub).astype(CDT)
    acc = jnp.zeros(x.shape, F32)
    for c in range(D_FF // FFN_CHUNK):
        lo = c * FFN_CHUNK
        a = _dot(u, wup_ref[0, :, lo:lo + FFN_CHUNK])
        b = _dot(u, wup_ref[0, :, D_FF + lo:D_FF + lo + FFN_CHUNK])
        h = (a * _sigmoid(a) * b).astype(CDT)
        acc = acc + _dot(h, wdn_ref[0, lo:lo + FFN_CHUNK, :])
    gate = mod[3 * sub + 2:3 * sub + 3, :]
    o_ref[0] = x + MACARON_W * gate * _rms(acc, gpost_ref[...])


def _ffn(x, mod, mod_row, gpre, gpost, wup, wdn, layer, sub):
    b, l, d = x.shape
    tm = min(512, l)
    return pl.pallas_call(
        functools.partial(_ffn_kernel, sub=sub),
        out_shape=jax.ShapeDtypeStruct(x.shape, F32),
        grid=(b, l // tm),
        in_specs=[pl.BlockSpec((1, tm, d), lambda i, j: (i, j, 0)),
                  pl.BlockSpec((1, N_MOD, d), lambda i, j: (mod_row(i), 0, 0)),
                  _resident((1, d)), _resident((1, d)),
                  _layer_resident(wup.shape, layer), _layer_resident(wdn.shape, layer)],
        out_specs=pl.BlockSpec((1, tm, d), lambda i, j: (i, j, 0)),
        compiler_params=_cparams("parallel", "parallel"),
        name="ffn_sublayer",
    )(x, mod, gpre, gpost, wup, wdn)


def _rope(x, tabs, half):
    n = x.shape[1] // LANE
    cos, sin_up, sin_dn = [t if n == 1 else jnp.concatenate([t] * n, axis=1) for t in tabs]
    return x * cos + pltpu.roll(x, half, 1) * sin_up + pltpu.roll(x, x.shape[1] - half, 1) * sin_dn


def _inproj_kernel(x_ref, mod_ref, gpre_ref, wbig_ref, gkv_ref, gq_ref, wukv_ref, wuq_ref,
                   cm_ref, smu_ref, smd_ref, cw_ref, swu_ref, swd_ref,
                   qm_ref, km_ref, vm_ref, qw_ref, kw_ref, vw_ref, s5_ref, hy_ref, s5_scr):
    u = _pre_mod(x_ref[0], gpre_ref[...], mod_ref[0], 1).astype(CDT)

    def seg(off, n):
        return _dot(u, wbig_ref[:, off:off + n])

    rope_m = (cm_ref[...], smu_ref[...], smd_ref[...])
    rope_w = (cw_ref[...], swu_ref[...], swd_ref[...])
    kr = _rope(seg(O_KR, LANE), rope_m, MLA_ROPE // 2)
    ckv = seg(O_CKV, LANE)
    kvn = _rms(ckv, gkv_ref[...]).astype(CDT)
    ones_lane = (lax.broadcasted_iota(jnp.int32, (1, LANE), 1) == MLA_V).astype(F32)
    for hp in range(0, MLA_HEADS, 2):
        kk = _dot(kvn, wukv_ref[:, hp * LANE:(hp + 2) * LANE])
        vv = _dot(kvn, wukv_ref[:, (MLA_HEADS + hp) * LANE:(MLA_HEADS + hp + 2) * LANE])
        for i in range(2):
            km_ref[0, hp + i] = (kk[:, i * LANE:(i + 1) * LANE] + kr).astype(CDT)
            vm_ref[0, hp + i] = (vv[:, i * LANE:(i + 1) * LANE] + ones_lane).T.astype(CDT)
    cq = seg(O_CQ, 2 * LANE)
    cqn = (cq * lax.rsqrt(jnp.sum(cq * cq, axis=-1, keepdims=True) * (1.0 / MLA_Q_LORA) + EPS)
           * gq_ref[...]).astype(CDT)
    for hp in range(0, MLA_HEADS, 2):
        qq = _rope(_dot(cqn, wuq_ref[:, hp * LANE:(hp + 2) * LANE]), rope_m, MLA_ROPE // 2) * (MLA_SCALE * LOG2E)
        for i in range(2):
            qm_ref[0, hp + i] = qq[:, i * LANE:(i + 1) * LANE].T.astype(CDT)
    for hp in range(0, SWA_HEADS, 2):
        sq = _rope(seg(O_SQ + hp * LANE, 2 * LANE), rope_w, SWA_HD // 2) * (SWA_SCALE * LOG2E)
        for i in range(2):
            qw_ref[0, hp + i] = sq[:, i * LANE:(i + 1) * LANE].T.astype(CDT)
    sk = _rope(seg(O_SK, SWA_KV_HEADS * LANE), rope_w, SWA_HD // 2)
    sv = seg(O_SV, SWA_KV_HEADS * LANE)
    for h in range(SWA_KV_HEADS):
        kw_ref[0, h] = sk[:, h * LANE:(h + 1) * LANE].astype(CDT)
        vw_ref[0, h] = (sv[:, h * LANE:(h + 1) * LANE] + ones_lane).T.astype(CDT)
    s5 = seg(O_S5, S5_W)
    nchunk = s5_scr.shape[1] // S5_T
    for half in range(S5_W // LANE):
        s5_scr[half] = s5[:, half * LANE:(half + 1) * LANE]
        for sig in range(S5_T):
            s5_ref[half, 0, :, sig * LANE:(sig + 1) * LANE] = (
                s5_scr[half, pl.ds(sig, nchunk, stride=S5_T), :].astype(CDT))
    for o in range(HY_ORDER + 1):
        hy_ref[0, :, o * HY_W:(o + 1) * HY_W] = seg(O_HY + o * HY_W, HY_W).astype(CDT)


def _inproj(x, mod, mod_row, gpre, wp, rope):
    b, l, d = x.shape
    tm = min(512, l)
    head = lambda n: jax.ShapeDtypeStruct((b, n, l, LANE), CDT)
    head_spec = lambda n: pl.BlockSpec((1, n, tm, LANE), lambda i, j: (i, 0, j, 0))
    tab_spec = pl.BlockSpec((tm, LANE), lambda i, j: (j, 0))
    row_spec = lambda n: pl.BlockSpec((1, tm, n), lambda i, j: (i, j, 0))
    head_t = lambda n: jax.ShapeDtypeStruct((b, n, LANE, l), CDT)
    head_t_spec = lambda n: pl.BlockSpec((1, n, LANE, tm), lambda i, j: (i, 0, 0, j))
    return pl.pallas_call(
        _inproj_kernel,
        out_shape=(head_t(MLA_HEADS), head(MLA_HEADS), head_t(MLA_HEADS), head_t(SWA_HEADS), head(SWA_KV_HEADS),
                   head_t(SWA_KV_HEADS), jax.ShapeDtypeStruct((S5_W // LANE, b, l // S5_T, S5_T * LANE), CDT),
                   jax.ShapeDtypeStruct((b, l, (HY_ORDER + 1) * HY_W), CDT)),
        grid=(b, l // tm),
        in_specs=[row_spec(d),
                  pl.BlockSpec((1, N_MOD, d), lambda i, j: (mod_row(i), 0, 0)),
                  _resident((1, d)), _resident(wp['w_big'].shape),
                  _resident((1, LANE)), _resident((1, 2 * LANE)),
                  _resident(wp['w_ukv'].shape), _resident(wp['w_uq'].shape),
                  ] + [tab_spec] * len(rope),
        out_specs=(head_t_spec(MLA_HEADS), head_spec(MLA_HEADS), head_t_spec(MLA_HEADS), head_t_spec(SWA_HEADS),
                   head_spec(SWA_KV_HEADS), head_t_spec(SWA_KV_HEADS),
                   pl.BlockSpec((S5_W // LANE, 1, tm // S5_T, S5_T * LANE), lambda i, j: (0, i, j, 0)),
                   row_spec((HY_ORDER + 1) * HY_W)),
        scratch_shapes=[pltpu.VMEM((S5_W // LANE, tm, LANE), F32)],
        compiler_params=_cparams("parallel", "parallel"),
        name="premod_inproj",
    )(x, mod, gpre, wp['w_big'], wp['g_kv'], wp['g_q'], wp['w_ukv'], wp['w_uq'], *rope)


MLA_KEY_CHUNK = 512
MLA_VROWS = 80


def _mla_scores(qt_ref, k_refs, s_buf, m_buf):
    qt = qt_ref[0, 0]
    m, off = None, 0
    for k_ref in k_refs:
        n = k_ref.shape[2]
        kc = min(MLA_KEY_CHUNK, n)
        for c in range(n // kc):
            s = _dot(k_ref[0, 0, c * kc:(c + 1) * kc, :], qt)
            s_buf[off + c * kc:off + (c + 1) * kc, :] = s
            cmax = s.max(axis=0, keepdims=True)
            m = cmax if m is None else jnp.maximum(m, cmax)
        off += n
    m_buf[...] = m


def _mla_values(vt_refs, s_buf, m_buf, o_ref):
    m = m_buf[...]
    acc, off = None, 0
    for vt_ref in vt_refs:
        n = vt_ref.shape[3]
        kc = min(MLA_KEY_CHUNK, n)
        for c in range(n // kc):
            p = jnp.exp2(s_buf[off + c * kc:off + (c + 1) * kc, :] - m).astype(CDT)
            pv = _dot(vt_ref[0, 0, 0:MLA_VROWS, c * kc:(c + 1) * kc], p)
            acc = pv if acc is None else acc + pv
        off += n
    o = acc / acc[MLA_V:MLA_V + 1, :]
    o = jnp.concatenate([o, jnp.zeros((LANE - MLA_VROWS, o.shape[1]), F32)], axis=0)
    o_ref[0] = o.T.astype(CDT)


def _mla_kernel(*refs, n_src, ntile):
    qt_ref, o_ref = refs[0], refs[1 + 2 * n_src]
    k_refs = [refs[1 + 2 * i] for i in range(n_src)]
    vt_refs = [refs[2 + 2 * i] for i in range(n_src)]
    s_bufs = refs[2 + 2 * n_src:4 + 2 * n_src]
    m_bufs = refs[4 + 2 * n_src:6 + 2 * n_src]
    t = pl.program_id(2)
    odd = t % 2 == 1

    @pl.when(t == 0)
    def _():
        _mla_scores(qt_ref, k_refs, s_bufs[0], m_bufs[0])

    @pl.when((t > 0) & (t < ntile) & odd)
    def _():
        _mla_scores(qt_ref, k_refs, s_bufs[1], m_bufs[1])
        _mla_values(vt_refs, s_bufs[0], m_bufs[0], o_ref)

    @pl.when((t > 0) & (t < ntile) & jnp.logical_not(odd))
    def _():
        _mla_scores(qt_ref, k_refs, s_bufs[0], m_bufs[0])
        _mla_values(vt_refs, s_bufs[1], m_bufs[1], o_ref)

    @pl.when(t == ntile)
    def _():
        _mla_values(vt_refs, s_bufs[(ntile - 1) % 2], m_bufs[(ntile - 1) % 2], o_ref)


def _mla(qt, kvs):
    b, h, _, l = qt.shape
    tq = min(512, l)
    ntile = l // tq
    in_specs = [pl.BlockSpec((1, 1, LANE, tq), lambda i, j, t: (i, j, 0, jnp.minimum(t, ntile - 1)))]
    args = [qt]
    for k, vt in kvs:
        n = k.shape[2]
        in_specs += [pl.BlockSpec((1, 1, n, LANE), lambda i, j, t: (i, j, 0, 0)),
                     pl.BlockSpec((1, 1, LANE, n), lambda i, j, t: (i, j, 0, 0))]
        args += [k, vt]
    nk = sum(k.shape[2] for k, _ in kvs)
    return pl.pallas_call(
        functools.partial(_mla_kernel, n_src=len(kvs), ntile=ntile),
        out_shape=jax.ShapeDtypeStruct((b, l, h * LANE), CDT),
        grid=(b, h, ntile + 1),
        in_specs=in_specs,
        out_specs=pl.BlockSpec((1, tq, LANE), lambda i, j, t: (i, jnp.maximum(t - 1, 0), j)),
        scratch_shapes=[pltpu.VMEM((nk, tq), F32), pltpu.VMEM((nk, tq), F32),
                        pltpu.VMEM((1, tq), F32), pltpu.VMEM((1, tq), F32)],
        compiler_params=_cparams("parallel", "parallel", "arbitrary"),
        name="mla_attention",
    )(*args)


def _swa_kernel(*refs, band, tq, nblk):
    sink_ref, qt_ref = refs[0], refs[1]
    o_ref = refs[-1]
    t = pl.program_id(1)
    g = SWA_HEADS // SWA_KV_HEADS
    lane = lax.broadcasted_iota(jnp.int32, (1, g * tq), 1)
    qi = jnp.where(lane < tq, lane, lane - tq)
    jp = lax.broadcasted_iota(jnp.int32, (WINDOW, 1), 0)
    jc = lax.broadcasted_iota(jnp.int32, (tq, 1), 0)
    for kv in range(SWA_KV_HEADS):
        qt = jnp.concatenate([qt_ref[0, kv * g + i] for i in range(g)], axis=-1)
        snk = jnp.where(lane < tq, sink_ref[kv * g], sink_ref[kv * g + 1]) * LOG2E
        ss, vts = [], []
        if band:
            kp, kc, kn, vtp, vtc, vtn, kx, vtx = [r[0, kv] for r in refs[2:10]]
            s_p = jnp.where((jp >= qi) & (t > 0), _dot(kp, qt), NEG_INF)
            s_c = jnp.where(jnp.abs(qi - jc) <= WINDOW, _dot(kc, qt), NEG_INF)
            s_n = jnp.where((jp <= qi - (tq - WINDOW)) & (t < nblk - 1), _dot(kn, qt), NEG_INF)
            ss += [s_p, s_c, s_n]
            vts += [vtp, vtc, vtn]
        else:
            kx, vtx = [r[0, kv] for r in refs[2:4]]
        ss.append(_dot(kx, qt))
        vts.append(vtx)
        m = snk
        for s in ss:
            m = jnp.maximum(m, s.max(axis=0, keepdims=True))
        acc = None
        for s, vt in zip(ss, vts):
            pv = _dot(vt[0:MLA_VROWS, :], jnp.exp2(s - m).astype(CDT))
            acc = pv if acc is None else acc + pv
        o = acc / (acc[SWA_HD:SWA_HD + 1, :] + jnp.exp2(snk - m))
        o = jnp.concatenate([o, jnp.zeros((LANE - MLA_VROWS, g * tq), F32)], axis=0)
        for i in range(g):
            o_ref[0, :, (kv * g + i) * LANE:(kv * g + i + 1) * LANE] = o[:, i * tq:(i + 1) * tq].T.astype(CDT)


def _swa(sink, qt, k, vt, kx, vtx, band):
    b, _, _, l = qt.shape
    hk = SWA_KV_HEADS
    tq = min(512, l)
    nblk = l // tq
    r = tq // WINDOW
    nw = l // WINDOW
    in_specs = [pl.BlockSpec(memory_space=pltpu.SMEM),
                pl.BlockSpec((1, SWA_HEADS, LANE, tq), lambda i, t: (i, 0, 0, t))]
    args = [sink, qt]
    if band:
        prev_i = lambda t: jnp.maximum(t * r - 1, 0)
        next_i = lambda t: jnp.minimum((t + 1) * r, nw - 1)
        in_specs += [pl.BlockSpec((1, hk, WINDOW, LANE), lambda i, t: (i, 0, prev_i(t), 0)),
                     pl.BlockSpec((1, hk, tq, LANE), lambda i, t: (i, 0, t, 0)),
                     pl.BlockSpec((1, hk, WINDOW, LANE), lambda i, t: (i, 0, next_i(t), 0)),
                     pl.BlockSpec((1, hk, LANE, WINDOW), lambda i, t: (i, 0, 0, prev_i(t))),
                     pl.BlockSpec((1, hk, LANE, tq), lambda i, t: (i, 0, 0, t)),
                     pl.BlockSpec((1, hk, LANE, WINDOW), lambda i, t: (i, 0, 0, next_i(t)))]
        args += [k, k, k, vt, vt, vt]
    nx = kx.shape[2]
    in_specs += [pl.BlockSpec((1, hk, nx, LANE), lambda i, t: (i, 0, 0, 0)),
                 pl.BlockSpec((1, hk, LANE, nx), lambda i, t: (i, 0, 0, 0))]
    args += [kx, vtx]
    return pl.pallas_call(
        functools.partial(_swa_kernel, band=band, tq=tq, nblk=nblk),
        out_shape=jax.ShapeDtypeStruct((b, l, SWA_HEADS * LANE), CDT),
        grid=(b, nblk),
        in_specs=in_specs,
        out_specs=pl.BlockSpec((1, tq, SWA_HEADS * LANE), lambda i, t: (i, t, 0)),
        compiler_params=_cparams("parallel", "parallel"),
        name="swa_attention",
    )(*args)


def _shortconv_kernel(x_ref, w_ref, b_ref, o_ref):
    x = x_ref[0].astype(F32)
    l = x.shape[0]
    t = lax.broadcasted_iota(jnp.int32, (l, 1), 0)
    prev = jnp.where(t == 0, 0.0, pltpu.roll(x, 1, 0))
    nxt = jnp.where(t == l - 1, 0.0, pltpu.roll(x, l - 1, 0))
    y = b_ref[...] + prev * w_ref[0:1, :] + x * w_ref[1:2, :] + nxt * w_ref[2:3, :]
    o_ref[0] = y.T.astype(CDT)


def _shortconv_t(x, w, bias):
    b, l, c = x.shape
    tc = 256
    return pl.pallas_call(
        _shortconv_kernel,
        out_shape=jax.ShapeDtypeStruct((b, c, l), CDT),
        grid=(b, c // tc),
        in_specs=[pl.BlockSpec((1, l, tc), lambda i, j: (i, 0, j)),
                  pl.BlockSpec((SHORT_K, tc), lambda i, j: (0, j)),
                  pl.BlockSpec((1, tc), lambda i, j: (0, j))],
        out_specs=pl.BlockSpec((1, tc, l), lambda i, j: (i, j, 0)),
        compiler_params=_cparams("parallel", "parallel"),
        name="hyena_shortconv",
    )(x, w, bias)


def _swap(x):
    return jnp.concatenate([x[..., LANE:], x[..., :LANE]], axis=-1)


def _comb(pq, n):
    p, q = pq[:, :n], pq[:, n:]
    return jnp.concatenate([p[..., :LANE] - q[..., LANE:], p[..., LANE:] + q[..., :LANE]], axis=-1)


def _hyena_kernel(v_ref, g1_ref, g2_ref, f1_ref, c1_ref, tw_ref, f2_ref, f2c_ref, ka_ref, kb_ref, bias_ref,
                  o_ref, *, ct, na, kin):
    def load(ref):
        return jnp.concatenate([ref[0, 0], ref[0, 1]], axis=-1).astype(F32)

    z = load(v_ref)
    gates = (g1_ref, g2_ref)
    twa, twb = tw_ref[0], tw_ref[1]
    for o in range(HY_ORDER):
        a = _comb(_bdot(f1_ref[...], z.astype(CDT)), na)
        a = a * twa + _swap(a) * twb
        x = _dot(a.reshape(ct * na, 2 * LANE).astype(CDT), f2_ref[...])
        y = x * ka_ref[o].reshape(ct * na, 2 * LANE) + _swap(x) * kb_ref[o].reshape(ct * na, 2 * LANE)
        bq = _dot(y.astype(CDT), f2c_ref[...]).reshape(ct, na, 2 * LANE)
        bq = bq * twa - _swap(bq) * twb
        yt = _comb(_bdot(c1_ref[...], bq.astype(CDT)), kin)
        z = load(gates[o]) * (yt + bias_ref[o] * z)
    o_ref[0, 0] = z[..., :LANE].astype(CDT)
    o_ref[0, 1] = z[..., LANE:].astype(CDT)


def _hyena_tables(na, kin, ct):
    f1, tw, f2, f2c, c1 = _hyena_dft_np(na, kin)
    bc = lambda m: jnp.broadcast_to(jnp.asarray(m, F32).astype(CDT)[None], (ct,) + m.shape)
    return (bc(f1), bc(c1), jnp.asarray(tw, F32), jnp.asarray(f2, F32).astype(CDT),
            jnp.asarray(f2c, F32).astype(CDT))


def _hyena_dft_np(na, kin):
    n = na * LANE
    ka = np.arange(na)[:, None]
    a = np.arange(kin)[None, :]
    ang1 = -2.0 * np.pi * ((ka * a) % na) / na
    f1 = np.concatenate([np.cos(ang1), np.sin(ang1)], axis=0)
    c1 = np.concatenate([np.cos(ang1).T, -np.sin(ang1).T], axis=0)
    bb = np.arange(LANE)[None, :]
    angt = -2.0 * np.pi * ((ka * bb) % n) / n
    tr, ti = np.cos(angt), np.sin(angt)
    tw = np.stack([np.concatenate([tr, tr], 1), np.concatenate([-ti, ti], 1)])
    b2 = np.arange(LANE)
    ang2 = -2.0 * np.pi * ((b2[:, None] * b2[None, :]) % LANE) / LANE
    fr, fi = np.cos(ang2), np.sin(ang2)
    f2 = np.block([[fr, fi], [-fi, fr]])
    f2c = np.block([[fr, -fi], [fi, fr]])
    return f1, tw, f2, f2c, c1


def _hyena_filter_taps(l, hp):
    t = jnp.linspace(0.0, 1.0, l, dtype=F32)[None, :]
    bands = (HY_EMB - 1) // 2
    w = 2.0 * math.pi * jnp.arange(l, dtype=F32) / l
    fr = jnp.linspace(1e-4, bands - 1, bands, dtype=F32)
    ang = fr[:, None] * w[None, :]
    z = jnp.concatenate([t, jnp.cos(ang), -jnp.sin(ang)], axis=0)
    freq = hp['hy_f_freq']
    hi = lax.Precision.HIGHEST
    h = jnp.sin(freq[0][:, None] * (jnp.dot(hp['hy_f_w1'].T, z, precision=hi) + hp['hy_f_b1'][:, None]))
    h = jnp.sin(freq[1][:, None] * (jnp.dot(hp['hy_f_w2'].T, h, precision=hi) + hp['hy_f_b2'][:, None]))
    h = jnp.dot(hp['hy_f_w3'].T, h, precision=hi).reshape(HY_ORDER, 2, HY_W, l)
    deltas = jnp.abs(jnp.linspace(math.log(HY_DECAY_TARGET) / HY_SLOW_PCT,
                                  math.log(HY_DECAY_TARGET) / HY_FAST_PCT, HY_W, dtype=F32))
    h = h * jnp.exp(-t * deltas[:, None])
    lag0 = (jnp.arange(l) > 0).astype(F32)
    return h[:, 0].reshape(HY_ORDER * HY_W, l), (h[:, 1] * lag0).reshape(HY_ORDER * HY_W, l)


def _split(x):
    hi = x.astype(CDT)
    return hi, (x - hi.astype(F32)).astype(CDT)


def _hyfilt_kernel(hf_ref, hb_ref, f1h_ref, f1l_ref, tw_ref, f2h_ref, f2l_ref, ka_ref, kb_ref, *, rt, na, inv_n):
    xh, xl = _split(jnp.concatenate([hf_ref[...], hb_ref[...]], axis=-1))
    f1h, f1l = f1h_ref[...], f1l_ref[...]
    pq = _bdot(f1h, xh) + _bdot(f1l, xh) + _bdot(f1h, xl)
    p, q = pq[:, :na], pq[:, na:]
    a = jnp.concatenate([jnp.concatenate([p[..., :LANE], q[..., :LANE]], -1),
                         jnp.concatenate([p[..., LANE:], q[..., LANE:]], -1)], axis=0)
    a = a * tw_ref[0] + _swap(a) * tw_ref[1]
    ah, al = _split(a.reshape(2 * rt * na, 2 * LANE))
    x = (_dot(ah, f2h_ref[...]) + _dot(al, f2h_ref[...]) + _dot(ah, f2l_ref[...])).reshape(2 * rt, na, 2 * LANE)
    xf, xb = x[:rt], x[rt:]
    kr = (xf[..., :LANE] + xb[..., :LANE]) * inv_n
    ki = (xf[..., LANE:] - xb[..., LANE:]) * inv_n
    ka_ref[...] = jnp.concatenate([kr, kr], -1)
    kb_ref[...] = jnp.concatenate([-ki, ki], -1)


def _hyena_filter_spectrum(hf, hb, na, kin):
    rows, l = hf.shape
    lp = kin * LANE
    if lp != l:
        hf, hb = [jnp.pad(h, ((0, 0), (0, lp - l))) for h in (hf, hb)]
    hf, hb = hf.reshape(rows, kin, LANE), hb.reshape(rows, kin, LANE)
    rt = 16
    f1, tw, f2 = _hyena_dft_np(na, kin)[:3]
    f1 = jnp.broadcast_to(jnp.asarray(f1, F32)[None], (rt,) + f1.shape)
    f1h, f1l = _split(f1)
    f2h, f2l = _split(jnp.asarray(f2, F32))
    tw = jnp.asarray(tw, F32)
    ka, kb = pl.pallas_call(
        functools.partial(_hyfilt_kernel, rt=rt, na=na, inv_n=1.0 / (na * LANE)),
        out_shape=(jax.ShapeDtypeStruct((rows, na, 2 * LANE), F32),) * 2,
        grid=(rows // rt,),
        in_specs=[pl.BlockSpec((rt, kin, LANE), lambda i: (i, 0, 0)),
                  pl.BlockSpec((rt, kin, LANE), lambda i: (i, 0, 0)),
                  _resident(f1h.shape), _resident(f1l.shape), _resident(tw.shape),
                  _resident(f2h.shape), _resident(f2l.shape)],
        out_specs=(pl.BlockSpec((rt, na, 2 * LANE), lambda i: (i, 0, 0)),) * 2,
        compiler_params=_cparams("parallel"),
        name="hyena_filter_spectrum",
    )(hf, hb, f1h, f1l, tw, f2h, f2l)
    shape = (HY_ORDER, HY_W, na, 2 * LANE)
    return ka.reshape(shape), kb.reshape(shape)


def _hyena(hy, hp, l_true):
    b, l, _ = hy.shape
    ut = _shortconv_t(hy, hp['hy_conv_w'], hp['hy_conv_b'].reshape(1, -1))
    kin = max(l // LANE, 16)
    na = 2 * kin
    lp = kin * LANE
    if lp != l:
        ut = jnp.pad(ut, ((0, 0), (0, 0), (0, lp - l)))
    ut = ut.reshape(b // 2, 2, (HY_ORDER + 1) * HY_W, kin, LANE)
    ct = 32
    nc = HY_W // ct
    f1, c1, tw, f2, f2c = _hyena_tables(na, kin, ct)
    ka, kb = hp['ka'], hp['kb']
    bias = jnp.broadcast_to(hp['hy_bias'].reshape(HY_ORDER, HY_W, 1, 1), (HY_ORDER, HY_W, 1, 2 * LANE))
    blk = lambda off: pl.BlockSpec((1, 2, ct, kin, LANE), lambda c, p: (p, 0, c + off * nc, 0, 0))
    out = pl.pallas_call(
        functools.partial(_hyena_kernel, ct=ct, na=na, kin=kin),
        out_shape=jax.ShapeDtypeStruct((b // 2, 2, HY_W, kin, LANE), CDT),
        grid=(nc, b // 2),
        in_specs=[blk(0), blk(1), blk(2),
                  _resident(f1.shape), _resident(c1.shape), _resident(tw.shape),
                  _resident(f2.shape), _resident(f2c.shape),
                  pl.BlockSpec((HY_ORDER, ct, na, 2 * LANE), lambda c, p: (0, c, 0, 0)),
                  pl.BlockSpec((HY_ORDER, ct, na, 2 * LANE), lambda c, p: (0, c, 0, 0)),
                  pl.BlockSpec((HY_ORDER, ct, 1, 2 * LANE), lambda c, p: (0, c, 0, 0))],
        out_specs=pl.BlockSpec((1, 2, ct, kin, LANE), lambda c, p: (p, 0, c, 0, 0)),
        compiler_params=_cparams("parallel", "arbitrary"),
        name="hyena_longconv",
    )(ut, ut, ut, f1, c1, tw, f2, f2c, ka, kb, bias)
    out = out.reshape(b, HY_W, lp)
    return out if lp == l else out[:, :, :l]


def _s5_kernel(u_ref, psel_ref, pselt_ref, tloc_ref, wx_ref, wout_ref, d_ref, h0_ref, y_ref, hfin_ref,
               x_scr, h_scr, up_scr, *, nchunk, nb):
    w = S5_T * S5_GC
    gb = max(1, min(nb, S5_BLOCK_ROWS // nchunk))
    for b0 in range(0, nb, gb):
        rs = slice(b0 * nchunk, (b0 + gb) * nchunk)
        up = _dot(u_ref[0, rs, :], psel_ref[0]).astype(CDT)
        up_scr[rs, :] = up
        xb = _dot(up, wx_ref[0])
        for i in range(gb):
            for k in range(4):
                x_scr[k, pl.ds(b0 + i, nchunk, stride=nb), :] = xb[i * nchunk:(i + 1) * nchunk,
                                                                   k * LANE:(k + 1) * LANE]
    d = d_ref[0]
    dfr, dfi, dbr, dbi = d[0:1], d[1:2], d[2:3], d[3:4]

    def body(j, carry):
        hr, hi, gr, gi = carry
        rf = pl.multiple_of(j * nb, nb)
        rb = pl.multiple_of((nchunk - 1 - j) * nb, nb)
        h_scr[0, pl.ds(rf, nb), :] = hr
        h_scr[1, pl.ds(rf, nb), :] = hi
        h_scr[2, pl.ds(rb, nb), :] = gr
        h_scr[3, pl.ds(rb, nb), :] = gi
        xr = x_scr[0, pl.ds(rf, nb), :]
        xi = x_scr[1, pl.ds(rf, nb), :]
        yr = x_scr[2, pl.ds(rb, nb), :]
        yi = x_scr[3, pl.ds(rb, nb), :]
        return (dfr * hr - dfi * hi + xr, dfr * hi + dfi * hr + xi,
                dbr * gr - dbi * gi + yr, dbr * gi + dbi * gr + yi)

    fin = lax.fori_loop(0, nchunk, body, tuple(h0_ref[0, k] for k in range(4)))
    for k in range(4):
        hfin_ref[0, k] = fin[k]
    r = pl.program_id(1)
    for b0 in range(0, nb, gb):
        rs = slice(b0 * nchunk, (b0 + gb) * nchunk)
        hs = jnp.concatenate(
            [jnp.concatenate([h_scr[k, pl.ds(b0 + i, nchunk, stride=nb), :] for k in range(4)], axis=-1)
             for i in range(gb)], axis=0).astype(CDT)
        y0 = _dot(up_scr[rs, :w], tloc_ref[0]) + _dot(hs, wout_ref[0, :, :w])
        y1 = _dot(up_scr[rs, w:], tloc_ref[1]) + _dot(hs, wout_ref[0, :, w:])
        contrib = _dot(jnp.concatenate([y0, y1], axis=-1).astype(CDT), pselt_ref[0]).astype(CDT)

        @pl.when(r == 0)
        def _():
            y_ref[0, rs, :] = contrib

        @pl.when(r > 0)
        def _():
            y_ref[0, rs, :] = y_ref[0, rs, :] + contrib


S5_HALF_GROUPS = LANE // S5_GC
S5_HALF_PAIRS = S5_HALF_GROUPS // 2


def _s5_select():
    ri = jnp.arange(S5_T * LANE)
    r_sig, r_grp, r_ch = ri // LANE, (ri % LANE) // S5_GC, ri % S5_GC
    ci = jnp.arange(2 * S5_T * S5_GC)
    c_grp, c_sig, c_ch = ci // (S5_T * S5_GC), (ci % (S5_T * S5_GC)) // S5_GC, ci % S5_GC
    same = (r_sig[:, None] == c_sig[None, :]) & (r_ch[:, None] == c_ch[None, :])
    sel = jnp.stack([(same & (r_grp[:, None] == 2 * q + c_grp[None, :])) for q in range(S5_HALF_PAIRS)])
    sel = sel.astype(CDT)
    return sel, jnp.swapaxes(sel, 1, 2)


def _s5(u, ops, h0):
    nhalf, b, nchunk, wide = u.shape
    w = S5_T * S5_GC
    rows = nchunk * b
    uh = u.reshape(nhalf, rows, wide)
    psel, pselt = _s5_select()
    pair = lambda h, r: h * S5_HALF_PAIRS + r
    y, hfin = pl.pallas_call(
        functools.partial(_s5_kernel, nchunk=nchunk, nb=b),
        out_shape=(jax.ShapeDtypeStruct((nhalf, rows, wide), CDT),
                   jax.ShapeDtypeStruct((S5_GROUPS // 2, 4, b, LANE), F32)),
        grid=(nhalf, S5_HALF_PAIRS),
        in_specs=[pl.BlockSpec((1, rows, wide), lambda h, r: (h, 0, 0), pipeline_mode=pl.Buffered(1)),
                  pl.BlockSpec((1, wide, 2 * w), lambda h, r: (r, 0, 0)),
                  pl.BlockSpec((1, 2 * w, wide), lambda h, r: (r, 0, 0)),
                  pl.BlockSpec((2, w, w), lambda h, r: (pair(h, r), 0, 0)),
                  pl.BlockSpec((1, 2 * w, 2 * w), lambda h, r: (pair(h, r), 0, 0)),
                  pl.BlockSpec((1, 2 * w, 2 * w), lambda h, r: (pair(h, r), 0, 0)),
                  pl.BlockSpec((1, 4, LANE), lambda h, r: (pair(h, r), 0, 0)),
                  pl.BlockSpec((1, 4, b, LANE), lambda h, r: (pair(h, r), 0, 0, 0))],
        out_specs=(pl.BlockSpec((1, rows, wide), lambda h, r: (h, 0, 0)),
                   pl.BlockSpec((1, 4, b, LANE), lambda h, r: (pair(h, r), 0, 0, 0))),
        scratch_shapes=[pltpu.VMEM((4, rows, LANE), F32), pltpu.VMEM((4, rows, LANE), F32),
                        pltpu.VMEM((rows, 2 * w), CDT)],
        compiler_params=_cparams("parallel", "arbitrary"),
        name="s5_chunked",
    )(uh, psel, pselt, ops['tloc'], ops['wx'], ops['wout'], ops['d16'], h0)
    return y.reshape(nhalf, b, nchunk, wide), hfin


def _s5_operators(p):
    g, pp, gc, t = S5_GROUPS, S5_P, S5_GC, S5_T
    npair = g // 2
    n = jnp.arange(t + 1, dtype=F32)[:, None, None]
    tops, wxs, wouts, d16 = [], [], [], []
    sig = jnp.arange(t)
    for d in range(2):
        a_re, a_im = p['s5_a_re'][d], p['s5_a_im'][d]
        dt = jnp.exp(p['s5_log_dt'][d])[:, None]
        mag1 = jnp.exp(dt * a_re)
        ab_re, ab_im = mag1 * jnp.cos(dt * a_im), mag1 * jnp.sin(dt * a_im)
        den = a_re * a_re + a_im * a_im
        f_re = ((ab_re - 1.0) * a_re + ab_im * a_im) / den
        f_im = (ab_im * a_re - (ab_re - 1.0) * a_im) / den
        mag = jnp.exp(n * (dt * a_re)[None])
        pr, pi = mag * jnp.cos(n * (dt * a_im)[None]), mag * jnp.sin(n * (dt * a_im)[None])
        b_re, b_im = p['s5_b_re'][d], p['s5_b_im'][d]
        bt_re = f_re[..., None] * b_re - f_im[..., None] * b_im
        bt_im = f_re[..., None] * b_im + f_im[..., None] * b_re
        c_re, c_im = p['s5_c_re'][d], p['s5_c_im'][d]
        ca_re = c_re[None] * pr[:, :, None, :] - c_im[None] * pi[:, :, None, :]
        ca_im = c_re[None] * pi[:, :, None, :] + c_im[None] * pr[:, :, None, :]
        hi = lax.Precision.HIGHEST
        lag_re = ca_re[:t] if d == 0 else ca_re[:t][::-1]
        lag_im = ca_im[:t] if d == 0 else ca_im[:t][::-1]
        tops.append(jnp.einsum('ngcp,gpk->gknc', lag_re, bt_re, precision=hi)
                    - jnp.einsum('ngcp,gpk->gknc', lag_im, bt_im, precision=hi))
        pwr, pwi = (pr[:t][::-1], pi[:t][::-1]) if d == 0 else (pr[:t], pi[:t])
        pwr, pwi = pwr.transpose(1, 0, 2)[:, :, None, :], pwi.transpose(1, 0, 2)[:, :, None, :]
        btr, bti = bt_re.transpose(0, 2, 1)[:, None], bt_im.transpose(0, 2, 1)[:, None]
        wxs.append(((pwr * btr - pwi * bti).reshape(g, t * gc, pp), (pwr * bti + pwi * btr).reshape(g, t * gc, pp)))
        out_re = ca_re[1:t + 1] if d == 0 else ca_re[1:t + 1][::-1]
        out_im = ca_im[1:t + 1] if d == 0 else ca_im[1:t + 1][::-1]
        wouts.append((out_re.transpose(1, 3, 0, 2).reshape(g, pp, t * gc),
                      -out_im.transpose(1, 3, 0, 2).reshape(g, pp, t * gc)))
        d16.append((pr[t], pi[t]))
    skip = jnp.eye(gc, dtype=F32)[None, :, None, :] * p['s5_d'].reshape(g, 1, 1, gc)
    kf, kb = tops
    kcomb = jnp.concatenate([kb[:, :, :t - 1], kf[:, :, 0:1] + kb[:, :, t - 1:t] + skip, kf[:, :, 1:]], axis=2)
    kcomb = kcomb.reshape(g, gc, (2 * t - 1) * gc)
    tloc = jnp.stack([kcomb[:, :, (t - 1 - s) * gc:(t - 1 - s) * gc + t * gc] for s in range(t)], axis=1)
    tloc = tloc.reshape(g, t * gc, t * gc)

    def pair_cols(m):
        m = m.reshape(npair, 2, m.shape[1], m.shape[2])
        z = jnp.zeros_like(m[:, 0])
        return jnp.concatenate([jnp.concatenate([m[:, 0], z], -1), jnp.concatenate([z, m[:, 1]], -1)], -2)

    wx = jnp.concatenate([pair_cols(wxs[0][0]), pair_cols(wxs[0][1]), pair_cols(wxs[1][0]), pair_cols(wxs[1][1])],
                         axis=-1)
    wout = jnp.concatenate([pair_cols(wouts[0][0]), pair_cols(wouts[0][1]), pair_cols(wouts[1][0]),
                            pair_cols(wouts[1][1])], axis=-2)
    dd = jnp.stack([d16[0][0], d16[0][1], d16[1][0], d16[1][1]], axis=0)
    dd = dd.reshape(4, npair, 2 * pp).transpose(1, 0, 2)
    return dict(tloc=tloc.astype(CDT), wx=wx.astype(CDT), wout=wout.astype(CDT), d16=dd)


def _merge_kernel(x_ref, mod_ref, gpre_ref, gpost_ref, omla_ref, ohy_ref, oswa_ref, ys5_ref,
                  wgate_ref, bgate_ref, wmla_ref, why_ref, wswa_ref, ws5_ref, gluw_ref, glub_ref, wout_ref, o_ref,
                  y_scr):
    x = x_ref[0]
    mod = mod_ref[0]
    u = _pre_mod(x, gpre_ref[...], mod, 1).astype(CDT)
    nchunk = y_scr.shape[1] // S5_T
    for sig in range(S5_T):
        for half in range(S5_W // LANE):
            y_scr[half, pl.ds(sig, nchunk, stride=S5_T), :] = (
                ys5_ref[half, 0, :, sig * LANE:(sig + 1) * LANE].astype(F32))
    y = jnp.concatenate([y_scr[half] for half in range(S5_W // LANE)], axis=-1)
    g = 0.5 * y * (1.0 + jnp.tanh(math.sqrt(2.0 / math.pi) * (y + 0.044715 * (y * y * y))))
    o_s5 = (g * _sigmoid(_dot(g.astype(CDT), gluw_ref[...]) + glub_ref[...])).astype(CDT)
    o_hy = ohy_ref[0].astype(F32).T.astype(CDT)
    outs = (omla_ref[0], o_hy, oswa_ref[0], o_s5)
    wbr = (wmla_ref, why_ref, wswa_ref, ws5_ref)
    m = jnp.zeros(x.shape, F32)
    for i in range(4):
        gate = _sigmoid(_dot(u, wgate_ref[0, i]) + bgate_ref[:, i * D_MODEL:(i + 1) * D_MODEL])
        m = m + gate * _dot(outs[i], wbr[i][...])
    f = _dot(m.astype(CDT), wout_ref[0])
    o_ref[0] = x + mod[5:6, :] * _rms(f, gpost_ref[...])


def _merge(x, mod, mod_row, gpre, gpost, o_mla, o_hy, o_swa, y_s5, wp, layer):
    b, l, d = x.shape
    tm = min(512, l)
    row_spec = lambda n: pl.BlockSpec((1, tm, n), lambda i, j: (i, j, 0))
    names = ('w_gate', 'b_gate', 'w_br_mla', 'w_br_hy', 'w_br_swa', 'w_br_s5', 'glu_w', 'glu_b', 'w_out')
    stacked = ('w_gate', 'w_out')
    wspec = lambda k: _layer_resident(wp[k].shape, layer) if k in stacked else _resident(wp[k].shape)
    return pl.pallas_call(
        _merge_kernel,
        out_shape=jax.ShapeDtypeStruct(x.shape, F32),
        grid=(b, l // tm),
        in_specs=[row_spec(d), pl.BlockSpec((1, N_MOD, d), lambda i, j: (mod_row(i), 0, 0)),
                  _resident((1, d)), _resident((1, d)),
                  row_spec(o_mla.shape[-1]), pl.BlockSpec((1, HY_W, tm), lambda i, j: (i, 0, j)),
                  row_spec(o_swa.shape[-1]),
                  pl.BlockSpec((S5_W // LANE, 1, tm // S5_T, S5_T * LANE), lambda i, j: (0, i, j, 0))]
        + [wspec(k) for k in names],
        out_specs=row_spec(d),
        scratch_shapes=[pltpu.VMEM((S5_W // LANE, tm, LANE), F32)],
        compiler_params=_cparams("parallel", "parallel"),
        name="merge_out",
    )(x, mod, gpre, gpost, o_mla, o_hy, o_swa, y_s5, *[wp[k] for k in names])


def _pad_cols(w, n):
    return jnp.pad(w, ((0, 0), (0, n - w.shape[1])))


def _pad_rows(w, n):
    return jnp.pad(w, ((0, n - w.shape[0]), (0, 0)))


def _prep_inproj(w_in, w_ukv, w_uq, g_kv, g_q):
    d = w_in.shape[0]
    zeros = lambda n: jnp.zeros((d, n), F32)
    krope = w_in[:, I_KROPE:I_KROPE + MLA_ROPE]
    kr = jnp.concatenate([zeros(MLA_NOPE), krope, zeros(LANE - MLA_NOPE - MLA_ROPE)], 1)
    ckv = w_in[:, I_CKV:I_CKV + MLA_KV_LORA]
    cq = _pad_cols(w_in[:, I_CQ:I_CQ + MLA_Q_LORA], 2 * LANE)

    def heads(w, nh):
        return jnp.concatenate([_pad_cols(w[:, h * SWA_HD:(h + 1) * SWA_HD], LANE) for h in range(nh)], 1)

    swq = w_in[:, I_SWQ:I_SWQ + SWA_HEADS * SWA_HD]
    swk = w_in[:, I_SWK:I_SWK + SWA_KV_HEADS * SWA_HD]
    swv = w_in[:, I_SWV:I_SWV + SWA_KV_HEADS * SWA_HD]
    w_big = jnp.concatenate([kr, ckv, cq, heads(swq, SWA_HEADS), heads(swk, SWA_KV_HEADS),
                             heads(swv, SWA_KV_HEADS),
                             w_in[:, I_S5:I_S5 + S5_W], w_in[:, I_HY:I_HY + (HY_ORDER + 1) * HY_W]], axis=1)
    assert w_big.shape[1] == N_BIG
    kvw = w_ukv.reshape(MLA_KV_LORA, MLA_HEADS, MLA_NOPE + MLA_V)
    kslots = [_pad_cols(kvw[:, h, :MLA_NOPE], LANE) for h in range(MLA_HEADS)]
    vslots = [_pad_cols(kvw[:, h, MLA_NOPE:], LANE) for h in range(MLA_HEADS)]
    w_ukv_p = jnp.concatenate(kslots + vslots, axis=1)
    qw = w_uq.reshape(MLA_Q_LORA, MLA_HEADS, MLA_NOPE + MLA_ROPE)
    qslots = [_pad_cols(qw[:, h], LANE) for h in range(MLA_HEADS)]
    w_uq_p = _pad_rows(jnp.concatenate(qslots, axis=1), 2 * LANE)
    return dict(w_big=w_big.astype(CDT), w_ukv=w_ukv_p.astype(CDT), w_uq=w_uq_p.astype(CDT),
                g_kv=g_kv.reshape(1, -1), g_q=_pad_cols(g_q.reshape(1, -1), 2 * LANE))


def _pad_head_rows(w, nh, hd):
    w = w.reshape(nh, hd, w.shape[-1])
    return jnp.pad(w, ((0, 0), (0, LANE - hd), (0, 0))).reshape(nh * LANE, -1)


def _rope_tables(n_tokens, use_rope):
    ones = jnp.ones((n_tokens, LANE), F32)
    zeros = jnp.zeros((n_tokens, LANE), F32)
    if not use_rope:
        return ones, zeros, zeros, ones, zeros, zeros

    def axial(rot_dim):
        rows = n_tokens // GRID_W
        r = jnp.repeat(jnp.arange(rows, dtype=F32), GRID_W)
        col = jnp.tile(jnp.arange(GRID_W, dtype=F32), rows)
        n_freq = rot_dim // 4
        freqs = ROPE_THETA ** (-jnp.arange(n_freq, dtype=F32) / n_freq)
        ang = jnp.concatenate([r[:, None] * freqs, col[:, None] * freqs], axis=-1)
        return jnp.cos(ang), jnp.sin(ang)

    c, s = axial(MLA_ROPE)
    pad = LANE - MLA_NOPE - MLA_ROPE
    half = MLA_ROPE // 2
    cm = jnp.concatenate([ones[:, :MLA_NOPE], c, c, ones[:, :pad]], 1)
    smu = jnp.concatenate([zeros[:, :MLA_NOPE + half], s, zeros[:, :pad]], 1)
    smd = jnp.concatenate([zeros[:, :MLA_NOPE], -s, zeros[:, :half + pad]], 1)
    c, s = axial(SWA_HD)
    half = SWA_HD // 2
    cw = jnp.concatenate([c, c, ones[:, :LANE - SWA_HD]], 1)
    swu = jnp.concatenate([zeros[:, :half], s, zeros[:, :LANE - SWA_HD]], 1)
    swd = jnp.concatenate([-s, zeros[:, :LANE - half]], 1)
    return cm, smu, smd, cw, swu, swd


def _hyena_params(p, l):
    kin = max(l // LANE, 16)
    na = 2 * kin
    hf, hb = _hyena_filter_taps(l, p)
    ka, kb = _hyena_filter_spectrum(hf, hb, na, kin)
    return dict(hy_conv_w=p['hy_conv_w'], hy_conv_b=p['hy_conv_b'], hy_bias=p['hy_bias'], ka=ka, kb=kb)


def kernel(x, c, ctx, c_ctx, w_ada, b_ada, norm_pre, norm_post, ffn1_up, ffn1_down, ffn2_up, ffn2_down,
           w_in, mla_q_norm, mla_kv_norm, mla_w_uq, mla_w_ukv, hy_conv_w, hy_conv_b, hy_f_w1, hy_f_b1,
           hy_f_freq, hy_f_w2, hy_f_b2, hy_f_w3, hy_bias, swa_sink, s5_a_re, s5_a_im, s5_log_dt,
           s5_b_re, s5_b_im, s5_c_re, s5_c_im, s5_d, s5_glu_w, s5_glu_b, w_gate, b_gate,
           w_br_mla, w_br_hy, w_br_swa, w_br_s5, w_out):
    stacked = dict(w_ada=w_ada, b_ada=b_ada, norm_pre=norm_pre, norm_post=norm_post,
                   ffn1_up=ffn1_up, ffn1_down=ffn1_down, ffn2_up=ffn2_up, ffn2_down=ffn2_down,
                   w_in=w_in, mla_q_norm=mla_q_norm, mla_kv_norm=mla_kv_norm, mla_w_uq=mla_w_uq,
                   mla_w_ukv=mla_w_ukv, hy_conv_w=hy_conv_w, hy_conv_b=hy_conv_b, hy_f_w1=hy_f_w1,
                   hy_f_b1=hy_f_b1, hy_f_freq=hy_f_freq, hy_f_w2=hy_f_w2, hy_f_b2=hy_f_b2, hy_f_w3=hy_f_w3,
                   hy_bias=hy_bias, swa_sink=swa_sink, s5_a_re=s5_a_re, s5_a_im=s5_a_im,
                   s5_log_dt=s5_log_dt, s5_b_re=s5_b_re, s5_b_im=s5_b_im, s5_c_re=s5_c_re,
                   s5_c_im=s5_c_im, s5_d=s5_d, s5_glu_w=s5_glu_w, s5_glu_b=s5_glu_b,
                   w_gate=w_gate, b_gate=b_gate, w_br_mla=w_br_mla, w_br_hy=w_br_hy,
                   w_br_swa=w_br_swa, w_br_s5=w_br_s5, w_out=w_out)
    depth = w_ada.shape[0]
    nb, seq, d = x.shape
    nctx = ctx.shape[1]
    assert nb % 2 == 0 and seq % 256 == 0 and nctx % 256 == 0

    rows = -(-(nb + 1) // 8) * 8
    cvec = jnp.zeros((rows, d), F32).at[:nb].set(c).at[nb].set(c_ctx)
    mods = _modulation(cvec, w_ada, b_ada)
    lat_row = lambda i: i
    ctx_row = lambda i: nb

    rope_l = _rope_tables(seq, True)
    rope_c = _rope_tables(nctx, False)
    h0_zero = jnp.zeros((S5_GROUPS // 2, 4, nb, LANE), F32)

    f1u, f1d, f2u, f2d = [w.astype(CDT) for w in (ffn1_up, ffn1_down, ffn2_up, ffn2_down)]
    w_gate_all, w_out_all = w_gate.astype(CDT), w_out.astype(CDT)

    s5_names = ('s5_a_re', 's5_a_im', 's5_log_dt', 's5_b_re', 's5_b_im', 's5_c_re', 's5_c_im', 's5_d')
    s5_ops_all = jax.vmap(_s5_operators)({k: stacked[k] for k in s5_names})

    xl, xc = x, ctx
    for l in range(depth):
        p = {name: arr[l] for name, arr in stacked.items()}
        mod = mods[l]
        ctx_out = l < depth - 1
        gpre = [p['norm_pre'][i].reshape(1, d) for i in range(N_SUB)]
        gpost = [p['norm_post'][i].reshape(1, d) for i in range(N_SUB)]
        wp_in = _prep_inproj(p['w_in'], p['mla_w_ukv'], p['mla_w_uq'], p['mla_kv_norm'], p['mla_q_norm'])
        wp_mg = dict(
            w_gate=w_gate_all,
            b_gate=p['b_gate'].reshape(1, -1),
            w_br_mla=_pad_head_rows(p['w_br_mla'], MLA_HEADS, MLA_V).astype(CDT),
            w_br_hy=p['w_br_hy'].astype(CDT),
            w_br_swa=_pad_head_rows(p['w_br_swa'], SWA_HEADS, SWA_HD).astype(CDT),
            w_br_s5=p['w_br_s5'].astype(CDT),
            glu_w=p['s5_glu_w'].astype(CDT), glu_b=p['s5_glu_b'].reshape(1, -1),
            w_out=w_out_all)
        s5_ops = {k: v[l] for k, v in s5_ops_all.items()}

        xl = _ffn(xl, mod, lat_row, gpre[0], gpost[0], f1u, f1d, l, 0)
        xc = _ffn(xc, mod, ctx_row, gpre[0], gpost[0], f1u, f1d, l, 0)

        qm_c, km_c, vm_c, qw_c, kw_c, vw_c, s5u_c, hy_c = _inproj(xc, mod, ctx_row, gpre[1], wp_in, rope_c)
        qm_l, km_l, vm_l, qw_l, kw_l, vw_l, s5u_l, hy_l = _inproj(xl, mod, lat_row, gpre[1], wp_in, rope_l)

        ys5_c, h_ctx = _s5(s5u_c, s5_ops, h0_zero)
        ys5_l, _ = _s5(s5u_l, s5_ops, h_ctx)
        o_mla = _mla(qm_l, [(km_l, vm_l), (km_c, vm_c)])
        o_swa = _swa(p['swa_sink'], qw_l, kw_l, vw_l, kw_c, vw_c, True)
        o_hy = _hyena(hy_l, _hyena_params(p, seq), seq)
        xl = _merge(xl, mod, lat_row, gpre[1], gpost[1], o_mla, o_hy, o_swa, ys5_l, wp_mg, l)
        xl = _ffn(xl, mod, lat_row, gpre[2], gpost[2], f2u, f2d, l, 2)
        if ctx_out:
            o_mla_c = _mla(qm_c, [(km_c, vm_c)])
            o_swa_c = _swa(p['swa_sink'], qw_c, kw_c, vw_c, kw_c, vw_c, False)
            o_hy_c = _hyena(hy_c, _hyena_params(p, nctx), nctx)
            xc = _merge(xc, mod, ctx_row, gpre[1], gpost[1], o_mla_c, o_hy_c, o_swa_c, ys5_c, wp_mg, l)
            xc = _ffn(xc, mod, ctx_row, gpre[2], gpost[2], f2u, f2d, l, 2)
    return xl
```
